```python
import math
import jax, jax.numpy as jnp
from jax import lax
import numpy as np

D_MODEL = 1024
BATCH = 4
SEQ = 4096
DEPTH = 1

ATTN_HEADS = 4
ATTN_HEAD_DIM = 64
ATTN_WIDTH = ATTN_HEADS * 2 * ATTN_HEAD_DIM
CONV_WIDTH = D_MODEL // 2
CONV_K = 3
Q_BLOCK = 128
NORM_EPS = 1e-6
LAMBDA_PARAM_STD = 0.1
IN_SPLITS = (
    ATTN_WIDTH,
    ATTN_WIDTH,
    ATTN_WIDTH,
    ATTN_WIDTH,
    CONV_WIDTH,
    CONV_WIDTH,
    CONV_WIDTH,
    CONV_WIDTH,
    D_MODEL,
    D_MODEL,
)
IN_COLS = sum(IN_SPLITS)

kernel_name = "hybrid_diffattn_shortconv_gated_merge"


def rms_norm(x, gain):
    x32 = x.astype(jnp.float32)
    y = x32 * lax.rsqrt(jnp.mean(x32 * x32, axis=-1, keepdims=True) + NORM_EPS)
    return (y * gain.astype(jnp.float32)).astype(x.dtype)


def alibi_slopes(n_heads):
    return jnp.asarray([2.0 ** (-8.0 * (i + 1) / n_heads) for i in range(n_heads)], dtype=jnp.float32)


def lambda_init_fn(layer_idx):
    return 0.8 - 0.6 * math.exp(-0.3 * layer_idx)


def diff_attention(q, k, v, lam, slopes):
    B, S, H, _, d = q.shape
    nb = S // Q_BLOCK
    scale = d ** -0.5
    qb = q.reshape(B, nb, Q_BLOCK, H, 2, d).transpose(1, 0, 3, 4, 2, 5)
    kt = k.transpose(0, 2, 3, 1, 4)
    vt = v.transpose(0, 2, 1, 3)
    kpos = jnp.arange(S)

    def block(args):
        qblk, blk = args
        qpos = blk * Q_BLOCK + jnp.arange(Q_BLOCK)
        s = jnp.einsum('bhmqd,bhmkd->bhmqk', qblk, kt).astype(jnp.float32) * scale
        dist = (qpos[:, None] - kpos[None, :]).astype(jnp.float32)
        s = s - slopes[:, None, None, None] * dist
        s = jnp.where(kpos[None, :] <= qpos[:, None], s, -jnp.inf)
        p = jax.nn.softmax(s, axis=-1)
        a = p[:, :, 0] - lam * p[:, :, 1]
        return jnp.einsum('bhqk,bhke->bhqe', a.astype(vt.dtype), vt)

    o = lax.map(block, (qb, jnp.arange(nb)))
    return o.transpose(1, 0, 3, 2, 4).reshape(B, S, H, 2 * d)


def causal_depthwise_conv(u, w):
    K, C = w.shape
    return lax.conv_general_dilated(
        u, w[:, None, :].astype(u.dtype), window_strides=(1,), padding=[(K - 1, 0)],
        dimension_numbers=('NWC', 'WIO', 'NWC'), feature_group_count=C)


def hybrid_layer(x, layer_idx, w_in, lambda_q1, lambda_k1, lambda_q2, lambda_k2, subln_gain,
                 conv_w, w_attn_o, w_conv_o, b_merge, w_out, g_pre, g_post):
    B, S, D = x.shape
    h = rms_norm(x, g_pre)
    proj = h @ w_in
    idx = list(np.cumsum(IN_SPLITS)[:-1])
    q, k, v, z_a, cb, cc, cx, z_c, ga, gc = jnp.split(proj, idx, axis=-1)

    lam_init = lambda_init_fn(layer_idx)
    lam = (jnp.exp(jnp.sum(lambda_q1.astype(jnp.float32) * lambda_k1.astype(jnp.float32)))
           - jnp.exp(jnp.sum(lambda_q2.astype(jnp.float32) * lambda_k2.astype(jnp.float32)))
           + lam_init)
    q = q.reshape(B, S, ATTN_HEADS, 2, ATTN_HEAD_DIM)
    k = k.reshape(B, S, ATTN_HEADS, 2, ATTN_HEAD_DIM)
    v = v.reshape(B, S, ATTN_HEADS, 2 * ATTN_HEAD_DIM)
    o = diff_attention(q, k, v, lam, alibi_slopes(ATTN_HEADS))
    o = rms_norm(o, subln_gain) * (1.0 - lam_init)
    y_attn = (o.reshape(B, S, ATTN_WIDTH) * jax.nn.silu(z_a)) @ w_attn_o

    u = causal_depthwise_conv(cc * cx, conv_w)
    y_conv = ((cb * u) * jax.nn.silu(z_c)) @ w_conv_o

    b_a, b_c = jnp.split(b_merge, 2)
    y = jax.nn.sigmoid(ga + b_a) * y_attn + jax.nn.sigmoid(gc + b_c) * y_conv
    out = y @ w_out
    return x + rms_norm(out, g_post)


def setup_inputs(seed: int = 0) -> dict:
    key = jax.random.key(seed)
    ks = jax.random.split(key, 16)
    f32 = jnp.float32
    nrm = lambda k, shape, s: jax.random.normal(k, shape, f32) * s
    return {
        "x": nrm(ks[0], (BATCH, SEQ, D_MODEL), 1.0),
        "w_in": nrm(ks[1], (DEPTH, D_MODEL, IN_COLS), D_MODEL ** -0.5),
        "lambda_q1": nrm(ks[2], (DEPTH, ATTN_HEAD_DIM), LAMBDA_PARAM_STD),
        "lambda_k1": nrm(ks[3], (DEPTH, ATTN_HEAD_DIM), LAMBDA_PARAM_STD),
        "lambda_q2": nrm(ks[4], (DEPTH, ATTN_HEAD_DIM), LAMBDA_PARAM_STD),
        "lambda_k2": nrm(ks[5], (DEPTH, ATTN_HEAD_DIM), LAMBDA_PARAM_STD),
        "subln_gain": 1.0 + nrm(ks[6], (DEPTH, 2 * ATTN_HEAD_DIM), 0.02),
        "conv_w": nrm(ks[7], (DEPTH, CONV_K, CONV_WIDTH), CONV_K ** -0.5),
        "w_attn_o": nrm(ks[8], (DEPTH, ATTN_WIDTH, D_MODEL), ATTN_WIDTH ** -0.5),
        "w_conv_o": nrm(ks[9], (DEPTH, CONV_WIDTH, D_MODEL), CONV_WIDTH ** -0.5),
        "b_merge": nrm(ks[10], (DEPTH, 2 * D_MODEL), 0.01),
        "w_out": nrm(ks[11], (DEPTH, D_MODEL, D_MODEL), D_MODEL ** -0.5),
        "g_pre": 1.0 + nrm(ks[12], (DEPTH, D_MODEL), 0.02),
        "g_post": 1.0 + nrm(ks[13], (DEPTH, D_MODEL), 0.02),
    }


def reference(x, w_in, lambda_q1, lambda_k1, lambda_q2, lambda_k2, subln_gain, conv_w,
              w_attn_o, w_conv_o, b_merge, w_out, g_pre, g_post):
    for l in range(DEPTH):
        x = hybrid_layer(x, l, w_in[l], lambda_q1[l], lambda_k1[l], lambda_q2[l], lambda_k2[l],
                         subln_gain[l], conv_w[l], w_attn_o[l], w_conv_o[l], b_merge[l],
                         w_out[l], g_pre[l], g_post[l])
    return x
```

```python
import functools
import math

import jax
import jax.numpy as jnp
from jax import lax
from jax.experimental import pallas as pl
from jax.experimental.pallas import tpu as pltpu

D_MODEL = 1024
ATTN_HEADS = 4
HEAD_DIM = 64
HEAD_COLS = 2 * HEAD_DIM
ATTN_WIDTH = ATTN_HEADS * HEAD_COLS
CONV_WIDTH = D_MODEL // 2
CONV_K = 3
NORM_EPS = 1e-6
QKV_COLS = 3 * ATTN_WIDTH
CONV_COLS = 4 * CONV_WIDTH
GATE_COLS = 2 * D_MODEL
IN_COLS = QKV_COLS + ATTN_WIDTH + CONV_COLS + GATE_COLS
MASK_VALUE = -1e30

PROJ_ROWS = 512
PROJ_COL_CHUNK = 512
ATTN_Q_ROWS = 512
ATTN_K_ROWS = 512
OUT_ROWS = 512
HALO_ROWS = 16
VMEM_LIMIT_BYTES = 56 * 1024 * 1024

BF16 = jnp.bfloat16
F32 = jnp.float32


def _lambda_init(layer_idx):
    return 0.8 - 0.6 * math.exp(-0.3 * layer_idx)


def _in_proj_kernel(x_ref, g_ref, w_ref, qkv_ref, za_ref, conv_ref, gate_ref, h_ref):
    x = x_ref[...]
    ms = jnp.mean(x * x, axis=-1, keepdims=True)
    h_ref[...] = (x * lax.rsqrt(ms + NORM_EPS) * g_ref[...]).astype(BF16)
    col = 0
    for out_ref in (qkv_ref, za_ref, conv_ref, gate_ref):
        for c in range(0, out_ref.shape[1], PROJ_COL_CHUNK):
            acc = jnp.dot(h_ref[...], w_ref[:, col:col + PROJ_COL_CHUNK],
                          preferred_element_type=F32)
            out_ref[:, c:c + PROJ_COL_CHUNK] = acc.astype(BF16)
            col += PROJ_COL_CHUNK


def _in_proj(x2d, g_pre, w_in_bf16):
    n = x2d.shape[0]
    row_block = lambda cols: pl.BlockSpec((PROJ_ROWS, cols), lambda i: (i, 0))
    return pl.pallas_call(
        _in_proj_kernel,
        grid=(n // PROJ_ROWS,),
        in_specs=[
            row_block(D_MODEL),
            pl.BlockSpec((1, D_MODEL), lambda i: (0, 0)),
            pl.BlockSpec((D_MODEL, IN_COLS), lambda i: (0, 0), pipeline_mode=pl.Buffered(1)),
        ],
        out_specs=[row_block(QKV_COLS), row_block(ATTN_WIDTH), row_block(CONV_COLS),
                   row_block(GATE_COLS)],
        out_shape=[jax.ShapeDtypeStruct((n, c), BF16)
                   for c in (QKV_COLS, ATTN_WIDTH, CONV_COLS, GATE_COLS)],
        scratch_shapes=[pltpu.VMEM((PROJ_ROWS, D_MODEL), BF16)],
        compiler_params=pltpu.CompilerParams(
            dimension_semantics=("arbitrary",), vmem_limit_bytes=VMEM_LIMIT_BYTES),
        name="in_proj",
    )(x2d, g_pre, w_in_bf16)


def _diff_attn_kernel(slopes_ref, lq1_ref, lk1_ref, lq2_ref, lk2_ref, gain_ref,
                      q_ref, k_ref, v_ref, o_ref, m_ref, l_ref, acc_ref, *, lam_init):
    tq, tk = ATTN_Q_ROWS, ATTN_K_ROWS
    head = pl.program_id(1)
    qi = pl.program_id(2)
    slope = slopes_ref[head]

    q = q_ref[...] * jnp.asarray(HEAD_DIM ** -0.5, BF16)
    lane = lax.broadcasted_iota(jnp.int32, q.shape, 1)
    zero = jnp.zeros_like(q)
    qq = jnp.concatenate([jnp.where(lane < HEAD_DIM, q, zero),
                          jnp.where(lane >= HEAD_DIM, q, zero)], axis=0)

    m_ref[...] = jnp.full(m_ref.shape, MASK_VALUE, F32)
    l_ref[...] = jnp.zeros(l_ref.shape, F32)
    acc_ref[...] = jnp.zeros(acc_ref.shape, F32)

    col_iota = lax.broadcasted_iota(jnp.int32, (1, tk), 1)

    def step(kj, masked):
        k = k_ref[pl.ds(pl.multiple_of(kj * tk, tk), tk), :]
        v = v_ref[pl.ds(pl.multiple_of(kj * tk, tk), tk), :]
        s = lax.dot_general(qq, k, (((1,), (1,)), ((), ())), preferred_element_type=F32)
        bias = (col_iota + (kj * tk - qi * tq)).astype(F32) * slope
        s = s + bias
        if masked:
            row = lax.broadcasted_iota(jnp.int32, (2 * tq, tk), 0)
            row = jnp.where(row >= tq, row - tq, row)
            colf = lax.broadcasted_iota(jnp.int32, (2 * tq, tk), 1)
            s = jnp.where(colf <= row, s, MASK_VALUE)
        m_prev = m_ref[...]
        m_new = jnp.maximum(m_prev, jnp.max(s, axis=-1, keepdims=True))
        alpha = jnp.exp(m_prev - m_new)
        p = jnp.exp(s - m_new)
        l_ref[...] = alpha * l_ref[...] + jnp.sum(p, axis=-1, keepdims=True)
        acc_ref[...] = alpha * acc_ref[...] + jnp.dot(p.astype(BF16), v,
                                                      preferred_element_type=F32)
        m_ref[...] = m_new

    def full_tile(kj, carry):
        step(kj, masked=False)
        return carry

    lax.fori_loop(0, qi, full_tile, 0)
    step(qi, masked=True)

    lam = (jnp.exp(jnp.sum(lq1_ref[...] * lk1_ref[...], axis=-1, keepdims=True))
           - jnp.exp(jnp.sum(lq2_ref[...] * lk2_ref[...], axis=-1, keepdims=True))
           + lam_init)
    out = acc_ref[...] / l_ref[...]
    o = out[:tq] - lam * out[tq:]
    ms = jnp.mean(o * o, axis=-1, keepdims=True)
    o = (o * lax.rsqrt(ms + NORM_EPS) * gain_ref[...]) * (1.0 - lam_init)
    o_ref[...] = o.astype(BF16)


def _diff_attn(qkv, slopes, lq1, lk1, lq2, lk2, gain, *, batch, seq, lam_init):
    nq = seq // ATTN_Q_ROWS
    small = lambda cols: pl.BlockSpec((1, cols), lambda b, h, i: (0, 0))
    return pl.pallas_call(
        functools.partial(_diff_attn_kernel, lam_init=lam_init),
        grid=(batch, ATTN_HEADS, nq),
        in_specs=[
            pl.BlockSpec(memory_space=pltpu.SMEM),
            small(HEAD_DIM), small(HEAD_DIM), small(HEAD_DIM), small(HEAD_DIM),
            small(HEAD_COLS),
            pl.BlockSpec((ATTN_Q_ROWS, HEAD_COLS), lambda b, h, i: (b * nq + i, h)),
            pl.BlockSpec((seq, HEAD_COLS), lambda b, h, i: (b, ATTN_HEADS + h)),
            pl.BlockSpec((seq, HEAD_COLS), lambda b, h, i: (b, 2 * ATTN_HEADS + h)),
        ],
        out_specs=pl.BlockSpec((ATTN_Q_ROWS, HEAD_COLS), lambda b, h, i: (b * nq + i, h)),
        out_shape=jax.ShapeDtypeStruct((batch * seq, ATTN_WIDTH), BF16),
        scratch_shapes=[pltpu.VMEM((2 * ATTN_Q_ROWS, 1), F32),
                        pltpu.VMEM((2 * ATTN_Q_ROWS, 1), F32),
                        pltpu.VMEM((2 * ATTN_Q_ROWS, HEAD_COLS), F32)],
        compiler_params=pltpu.CompilerParams(
            dimension_semantics=("arbitrary", "arbitrary", "arbitrary"),
            vmem_limit_bytes=VMEM_LIMIT_BYTES),
        name="diff_attn",
    )(slopes, lq1, lk1, lq2, lk2, gain, qkv, qkv, qkv)


def _sigmoid(z):
    return 1.0 / (1.0 + jnp.exp(-z))


def _out_proj_kernel(x_ref, o_ref, za_ref, conv_ref, halo_ref, gate_ref, cw_ref, wa_ref,
                     wc_ref, bm_ref, wo_ref, gp_ref, out_ref, u_ref, *, tiles_per_seq):
    tm, cwid = OUT_ROWS, CONV_WIDTH
    first = pl.program_id(0) % tiles_per_seq == 0

    z_a = za_ref[...].astype(F32)
    ya_in = o_ref[...].astype(F32) * (z_a * _sigmoid(z_a))
    y_attn = jnp.dot(ya_in.astype(BF16), wa_ref[...], preferred_element_type=F32)

    cb = conv_ref[:, 0:cwid].astype(F32)
    u = conv_ref[:, cwid:2 * cwid].astype(F32) * conv_ref[:, 2 * cwid:3 * cwid].astype(F32)
    z_c = conv_ref[:, 3 * cwid:4 * cwid].astype(F32)
    halo = (halo_ref[:, cwid:2 * cwid].astype(F32) * halo_ref[:, 2 * cwid:3 * cwid].astype(F32))
    u_ref[0:HALO_ROWS, :] = jnp.where(first, 0.0, halo)
    u_ref[HALO_ROWS:HALO_ROWS + tm, :] = u
    conv = cw_ref[CONV_K - 1:CONV_K, :] * u
    for k in range(CONV_K - 1):
        shift = CONV_K - 1 - k
        conv = conv + cw_ref[k:k + 1, :] * u_ref[HALO_ROWS - shift:HALO_ROWS - shift + tm, :]
    yc_in = (cb * conv) * (z_c * _sigmoid(z_c))
    y_conv = jnp.dot(yc_in.astype(BF16), wc_ref[...], preferred_element_type=F32)

    g_a = _sigmoid(gate_ref[:, 0:D_MODEL].astype(F32) + bm_ref[:, 0:D_MODEL])
    g_c = _sigmoid(gate_ref[:, D_MODEL:2 * D_MODEL].astype(F32) + bm_ref[:, D_MODEL:2 * D_MODEL])
    y = g_a * y_attn + g_c * y_conv
    out = jnp.dot(y.astype(BF16), wo_ref[...], preferred_element_type=F32)
    ms = jnp.mean(out * out, axis=-1, keepdims=True)
    out_ref[...] = x_ref[...] + out * lax.rsqrt(ms + NORM_EPS) * gp_ref[...]


def _out_proj(x2d, o, za, conv, gate, conv_w, wa, wc, b_merge, wo, g_post, *, seq):
    n = x2d.shape[0]
    tiles_per_seq = seq // OUT_ROWS
    halo_blocks = OUT_ROWS // HALO_ROWS
    row_block = lambda cols: pl.BlockSpec((OUT_ROWS, cols), lambda i: (i, 0))
    whole = lambda shape: pl.BlockSpec(shape, lambda i: (0, 0))
    return pl.pallas_call(
        functools.partial(_out_proj_kernel, tiles_per_seq=tiles_per_seq),
        grid=(n // OUT_ROWS,),
        in_specs=[
            row_block(D_MODEL), row_block(ATTN_WIDTH), row_block(ATTN_WIDTH), row_block(CONV_COLS),
            pl.BlockSpec((HALO_ROWS, CONV_COLS),
                         lambda i: (jnp.maximum(i * halo_blocks - 1, 0), 0)),
            row_block(GATE_COLS),
            whole((CONV_K, CONV_WIDTH)), whole((ATTN_WIDTH, D_MODEL)),
            whole((CONV_WIDTH, D_MODEL)), whole((1, 2 * D_MODEL)), whole((D_MODEL, D_MODEL)),
            whole((1, D_MODEL)),
        ],
        out_specs=row_block(D_MODEL),
        out_shape=jax.ShapeDtypeStruct((n, D_MODEL), F32),
        scratch_shapes=[pltpu.VMEM((HALO_ROWS + OUT_ROWS, CONV_WIDTH), F32)],
        compiler_params=pltpu.CompilerParams(
            dimension_semantics=("arbitrary",), vmem_limit_bytes=VMEM_LIMIT_BYTES),
        name="out_proj",
    )(x2d, o, za, conv, conv, gate, conv_w, wa, wc, b_merge, wo, g_post)


def _layer(x, layer_idx, w_in, lq1, lk1, lq2, lk2, subln_gain, conv_w, w_attn_o, w_conv_o,
           b_merge, w_out, g_pre, g_post):
    batch, seq, d = x.shape
    x2d = x.reshape(batch * seq, d)
    row = lambda a: a.reshape(1, -1).astype(F32)
    slopes = jnp.asarray([2.0 ** (-8.0 * (i + 1) / ATTN_HEADS) for i in range(ATTN_HEADS)], F32)

    qkv, za, conv, gate = _in_proj(x2d, row(g_pre), w_in.astype(BF16))
    o = _diff_attn(qkv, slopes, row(lq1), row(lk1), row(lq2), row(lk2), row(subln_gain),
                   batch=batch, seq=seq, lam_init=_lambda_init(layer_idx))
    out = _out_proj(x2d, o, za, conv, gate, conv_w.astype(F32), w_attn_o.astype(BF16),
                    w_conv_o.astype(BF16), row(b_merge), w_out.astype(BF16), row(g_post), seq=seq)
    return out.reshape(batch, seq, d)


def kernel(x, w_in, lambda_q1, lambda_k1, lambda_q2, lambda_k2, subln_gain, conv_w, w_attn_o,
           w_conv_o, b_merge, w_out, g_pre, g_post):
    for l in range(w_in.shape[0]):
        x = _layer(x, l, w_in[l], lambda_q1[l], lambda_k1[l], lambda_q2[l], lambda_k2[l],
                   subln_gain[l], conv_w[l], w_attn_o[l], w_conv_o[l], b_merge[l], w_out[l],
                   g_pre[l], g_post[l])
    return x
```

```python
import functools
import math

import jax
import jax.numpy as jnp
from jax import lax
from jax.experimental import pallas as pl
from jax.experimental.pallas import tpu as pltpu

D_MODEL = 1024
ATTN_HEADS = 4
HEAD_DIM = 64
HEAD_COLS = 2 * HEAD_DIM
ATTN_WIDTH = ATTN_HEADS * HEAD_COLS
CONV_WIDTH = D_MODEL // 2
CONV_K = 3
NORM_EPS = 1e-6
QKV_COLS = 3 * ATTN_WIDTH
CONV_COLS = 4 * CONV_WIDTH
GATE_COLS = 2 * D_MODEL
IN_COLS = QKV_COLS + ATTN_WIDTH + CONV_COLS + GATE_COLS
MASK_VALUE = -1e30

PROJ_ROWS = 512
PROJ_COL_CHUNK = 512
ATTN_Q_ROWS = 512
ATTN_K_ROWS = 512
OUT_ROWS = 512
HALO_ROWS = 16
VMEM_LIMIT_BYTES = 56 * 1024 * 1024

BF16 = jnp.bfloat16
F32 = jnp.float32


def _lambda_init(layer_idx):
    return 0.8 - 0.6 * math.exp(-0.3 * layer_idx)


def _in_proj_kernel(x_ref, g_ref, w_ref, qkv_ref, za_ref, conv_ref, gate_ref, h_ref):
    x = x_ref[...]
    ms = jnp.mean(x * x, axis=-1, keepdims=True)
    h_ref[...] = (x * lax.rsqrt(ms + NORM_EPS) * g_ref[...]).astype(BF16)
    col = 0
    for out_ref in (qkv_ref, za_ref, conv_ref, gate_ref):
        for c in range(0, out_ref.shape[1], PROJ_COL_CHUNK):
            acc = jnp.dot(h_ref[...], w_ref[:, col:col + PROJ_COL_CHUNK],
                          preferred_element_type=F32)
            out_ref[:, c:c + PROJ_COL_CHUNK] = acc.astype(BF16)
            col += PROJ_COL_CHUNK


def _in_proj(x2d, g_pre, w_in_bf16):
    n = x2d.shape[0]
    row_block = lambda cols: pl.BlockSpec((PROJ_ROWS, cols), lambda i: (i, 0))
    return pl.pallas_call(
        _in_proj_kernel,
        grid=(n // PROJ_ROWS,),
        in_specs=[
            row_block(D_MODEL),
            pl.BlockSpec((1, D_MODEL), lambda i: (0, 0)),
            pl.BlockSpec((D_MODEL, IN_COLS), lambda i: (0, 0), pipeline_mode=pl.Buffered(1)),
        ],
        out_specs=[row_block(QKV_COLS), row_block(ATTN_WIDTH), row_block(CONV_COLS),
                   row_block(GATE_COLS)],
        out_shape=[jax.ShapeDtypeStruct((n, c), BF16)
                   for c in (QKV_COLS, ATTN_WIDTH, CONV_COLS, GATE_COLS)],
        scratch_shapes=[pltpu.VMEM((PROJ_ROWS, D_MODEL), BF16)],
        compiler_params=pltpu.CompilerParams(
            dimension_semantics=("arbitrary",), vmem_limit_bytes=VMEM_LIMIT_BYTES),
        name="in_proj",
    )(x2d, g_pre, w_in_bf16)


def _diff_attn_kernel(slopes_ref, lq1_ref, lk1_ref, lq2_ref, lk2_ref, gain_ref,
                      q_ref, k_ref, v_ref, o_ref, ka_ref, kb_ref, m_ref, l_ref, acc_ref,
                      *, lam_init):
    tq, tk = ATTN_Q_ROWS, ATTN_K_ROWS
    seq = k_ref.shape[0]
    head = pl.program_id(1)
    qi = pl.program_id(2)
    slope = slopes_ref[head]
    one = jnp.ones((), BF16)
    zero = jnp.zeros((), BF16)

    @pl.when(qi == 0)
    def _build_keys():
        lane = lax.broadcasted_iota(jnp.int32, (tk, HEAD_COLS), 1)
        b = lax.broadcasted_iota(jnp.int32, (tk, HEAD_COLS), 0).astype(F32) * slope
        b1 = b.astype(BF16).astype(F32)
        b2 = (b - b1).astype(BF16).astype(F32)
        b3 = (b - b1 - b2).astype(BF16).astype(F32)

        def bias_lanes(first):
            return jnp.where(lane == first, b1, jnp.where(
                lane == first + 1, b2, jnp.where(lane == first + 2, b3, 0.0)))

        bias_a = bias_lanes(HEAD_DIM)
        bias_b = bias_lanes(0)
        for c in range(0, seq, tk):
            k = k_ref[c:c + tk, :].astype(F32)
            ka_ref[c:c + tk, :] = jnp.where(lane < HEAD_DIM, k, bias_a).astype(BF16)
            kb_ref[c:c + tk, :] = jnp.where(lane >= HEAD_DIM, k, bias_b).astype(BF16)

    q = q_ref[...].astype(F32) * HEAD_DIM ** -0.5
    qlane = lax.broadcasted_iota(jnp.int32, q.shape, 1)
    q_maps = (
        jnp.where(qlane < HEAD_DIM, q, jnp.where(qlane < HEAD_DIM + 3, 1.0, 0.0)).astype(BF16),
        jnp.where(qlane >= HEAD_DIM, q, jnp.where(qlane < 3, 1.0, 0.0)).astype(BF16),
    )
    key_refs = (ka_ref, kb_ref)

    m_ref[...] = jnp.full(m_ref.shape, MASK_VALUE, F32)
    l_ref[...] = jnp.zeros(l_ref.shape, F32)
    acc_ref[...] = jnp.zeros(acc_ref.shape, F32)

    def step(kj, masked):
        rows = pl.ds(pl.multiple_of(kj * tk, tk), tk)
        v = v_ref[rows, :]
        tile_bias = (kj * tk - qi * tq).astype(F32) * slope
        if masked:
            keep = (lax.broadcasted_iota(jnp.int32, (tq, tk), 1)
                    <= lax.broadcasted_iota(jnp.int32, (tq, tk), 0))
        for mp in range(2):
            s = lax.dot_general(q_maps[mp], key_refs[mp][rows, :], (((1,), (1,)), ((), ())),
                                preferred_element_type=F32)
            if masked:
                s = jnp.where(keep, s, MASK_VALUE)
            m_prev = m_ref[mp]
            m_new = jnp.maximum(m_prev, jnp.max(s, axis=-1, keepdims=True) + tile_bias)
            alpha = jnp.exp(m_prev - m_new)
            shift = m_new - tile_bias
            p = jnp.exp(s - jnp.concatenate([shift] * (tk // HEAD_COLS), axis=1))
            l_ref[mp] = alpha * l_ref[mp] + jnp.sum(p, axis=-1, keepdims=True)
            acc_ref[mp] = alpha * acc_ref[mp] + jnp.dot(p.astype(BF16), v,
                                                        preferred_element_type=F32)
            m_ref[mp] = m_new

    def full_tile(kj, carry):
        step(kj, masked=False)
        return carry

    lax.fori_loop(0, qi, full_tile, 0)
    step(qi, masked=True)

    lam = (jnp.exp(jnp.sum(lq1_ref[...] * lk1_ref[...], axis=-1, keepdims=True))
           - jnp.exp(jnp.sum(lq2_ref[...] * lk2_ref[...], axis=-1, keepdims=True))
           + lam_init)
    o = acc_ref[0] / l_ref[0] - lam * (acc_ref[1] / l_ref[1])
    ms = jnp.mean(o * o, axis=-1, keepdims=True)
    o = (o * lax.rsqrt(ms + NORM_EPS) * gain_ref[...]) * (1.0 - lam_init)
    o_ref[...] = o.astype(BF16)


def _diff_attn(qkv, slopes, lq1, lk1, lq2, lk2, gain, *, batch, seq, lam_init):
    nq = seq // ATTN_Q_ROWS
    small = lambda cols: pl.BlockSpec((1, cols), lambda b, h, i: (0, 0))
    return pl.pallas_call(
        functools.partial(_diff_attn_kernel, lam_init=lam_init),
        grid=(batch, ATTN_HEADS, nq),
        in_specs=[
            pl.BlockSpec(memory_space=pltpu.SMEM),
            small(HEAD_DIM), small(HEAD_DIM), small(HEAD_DIM), small(HEAD_DIM),
            small(HEAD_COLS),
            pl.BlockSpec((ATTN_Q_ROWS, HEAD_COLS), lambda b, h, i: (b * nq + i, h)),
            pl.BlockSpec((seq, HEAD_COLS), lambda b, h, i: (b, ATTN_HEADS + h)),
            pl.BlockSpec((seq, HEAD_COLS), lambda b, h, i: (b, 2 * ATTN_HEADS + h)),
        ],
        out_specs=pl.BlockSpec((ATTN_Q_ROWS, HEAD_COLS), lambda b, h, i: (b * nq + i, h)),
        out_shape=jax.ShapeDtypeStruct((batch * seq, ATTN_WIDTH), BF16),
        scratch_shapes=[pltpu.VMEM((seq, HEAD_COLS), BF16),
                        pltpu.VMEM((seq, HEAD_COLS), BF16),
                        pltpu.VMEM((2, ATTN_Q_ROWS, HEAD_COLS), F32),
                        pltpu.VMEM((2, ATTN_Q_ROWS, HEAD_COLS), F32),
                        pltpu.VMEM((2, ATTN_Q_ROWS, HEAD_COLS), F32)],
        compiler_params=pltpu.CompilerParams(
            dimension_semantics=("arbitrary", "arbitrary", "arbitrary"),
            vmem_limit_bytes=VMEM_LIMIT_BYTES),
        name="diff_attn",
    )(slopes, lq1, lk1, lq2, lk2, gain, qkv, qkv, qkv)


def _sigmoid(z):
    return 1.0 / (1.0 + jnp.exp(-z))


def _out_proj_kernel(x_ref, o_ref, za_ref, conv_ref, halo_ref, gate_ref, cw_ref, wa_ref,
                     wc_ref, bm_ref, wo_ref, gp_ref, out_ref, u_ref, *, tiles_per_seq):
    tm, cwid = OUT_ROWS, CONV_WIDTH
    first = pl.program_id(0) % tiles_per_seq == 0

    z_a = za_ref[...].astype(F32)
    ya_in = o_ref[...].astype(F32) * (z_a * _sigmoid(z_a))
    y_attn = jnp.dot(ya_in.astype(BF16), wa_ref[...], preferred_element_type=F32)

    cb = conv_ref[:, 0:cwid].astype(F32)
    u = conv_ref[:, cwid:2 * cwid].astype(F32) * conv_ref[:, 2 * cwid:3 * cwid].astype(F32)
    z_c = conv_ref[:, 3 * cwid:4 * cwid].astype(F32)
    halo = (halo_ref[:, cwid:2 * cwid].astype(F32) * halo_ref[:, 2 * cwid:3 * cwid].astype(F32))
    u_ref[0:HALO_ROWS, :] = jnp.where(first, 0.0, halo)
    u_ref[HALO_ROWS:HALO_ROWS + tm, :] = u
    conv = cw_ref[CONV_K - 1:CONV_K, :] * u
    for k in range(CONV_K - 1):
        shift = CONV_K - 1 - k
        conv = conv + cw_ref[k:k + 1, :] * u_ref[HALO_ROWS - shift:HALO_ROWS - shift + tm, :]
    yc_in = (cb * conv) * (z_c * _sigmoid(z_c))
    y_conv = jnp.dot(yc_in.astype(BF16), wc_ref[...], preferred_element_type=F32)

    g_a = _sigmoid(gate_ref[:, 0:D_MODEL].astype(F32) + bm_ref[:, 0:D_MODEL])
    g_c = _sigmoid(gate_ref[:, D_MODEL:2 * D_MODEL].astype(F32) + bm_ref[:, D_MODEL:2 * D_MODEL])
    y = g_a * y_attn + g_c * y_conv
    out = jnp.dot(y.astype(BF16), wo_ref[...], preferred_element_type=F32)
    ms = jnp.mean(out * out, axis=-1, keepdims=True)
    out_ref[...] = x_ref[...] + out * lax.rsqrt(ms + NORM_EPS) * gp_ref[...]


def _out_proj(x2d, o, za, conv, gate, conv_w, wa, wc, b_merge, wo, g_post, *, seq):
    n = x2d.shape[0]
    tiles_per_seq = seq // OUT_ROWS
    halo_blocks = OUT_ROWS // HALO_ROWS
    row_block = lambda cols: pl.BlockSpec((OUT_ROWS, cols), lambda i: (i, 0))
    whole = lambda shape: pl.BlockSpec(shape, lambda i: (0, 0))
    return pl.pallas_call(
        functools.partial(_out_proj_kernel, tiles_per_seq=tiles_per_seq),
        grid=(n // OUT_ROWS,),
        in_specs=[
            row_block(D_MODEL), row_block(ATTN_WIDTH), row_block(ATTN_WIDTH), row_block(CONV_COLS),
            pl.BlockSpec((HALO_ROWS, CONV_COLS),
                         lambda i: (jnp.maximum(i * halo_blocks - 1, 0), 0)),
            row_block(GATE_COLS),
            whole((CONV_K, CONV_WIDTH)), whole((ATTN_WIDTH, D_MODEL)),
            whole((CONV_WIDTH, D_MODEL)), whole((1, 2 * D_MODEL)), whole((D_MODEL, D_MODEL)),
            whole((1, D_MODEL)),
        ],
        out_specs=row_block(D_MODEL),
        out_shape=jax.ShapeDtypeStruct((n, D_MODEL), F32),
        scratch_shapes=[pltpu.VMEM((HALO_ROWS + OUT_ROWS, CONV_WIDTH), F32)],
        compiler_params=pltpu.CompilerParams(
            dimension_semantics=("arbitrary",), vmem_limit_bytes=VMEM_LIMIT_BYTES),
        name="out_proj",
    )(x2d, o, za, conv, conv, gate, conv_w, wa, wc, b_merge, wo, g_post)


def _layer(x, layer_idx, w_in, lq1, lk1, lq2, lk2, subln_gain, conv_w, w_attn_o, w_conv_o,
           b_merge, w_out, g_pre, g_post):
    batch, seq, d = x.shape
    x2d = x.reshape(batch * seq, d)
    row = lambda a: a.reshape(1, -1).astype(F32)
    slopes = jnp.asarray([2.0 ** (-8.0 * (i + 1) / ATTN_HEADS) for i in range(ATTN_HEADS)], F32)

    qkv, za, conv, gate = _in_proj(x2d, row(g_pre), w_in.astype(BF16))
    o = _diff_attn(qkv, slopes, row(lq1), row(lk1), row(lq2), row(lk2), row(subln_gain),
                   batch=batch, seq=seq, lam_init=_lambda_init(layer_idx))
    out = _out_proj(x2d, o, za, conv, gate, conv_w.astype(F32), w_attn_o.astype(BF16),
                    w_conv_o.astype(BF16), row(b_merge), w_out.astype(BF16), row(g_post), seq=seq)
    return out.reshape(batch, seq, d)


def kernel(x, w_in, lambda_q1, lambda_k1, lambda_q2, lambda_k2, subln_gain, conv_w, w_attn_o,
           w_conv_o, b_merge, w_out, g_pre, g_post):
    for l in range(w_in.shape[0]):
        x = _layer(x, l, w_in[l], lambda_q1[l], lambda_k1[l], lambda_q2[l], lambda_k2[l],
                   subln_gain[l], conv_w[l], w_attn_o[l], w_conv_o[l], b_merge[l], w_out[l],
                   g_pre[l], g_post[l])
    return x
```

```python
import functools
import math

import jax
import jax.numpy as jnp
from jax import lax
from jax.experimental import pallas as pl
from jax.experimental.pallas import tpu as pltpu

D_MODEL = 1024
ATTN_HEADS = 4
HEAD_DIM = 64
HEAD_COLS = 2 * HEAD_DIM
ATTN_WIDTH = ATTN_HEADS * HEAD_COLS
CONV_WIDTH = D_MODEL // 2
CONV_K = 3
NORM_EPS = 1e-6
QKV_COLS = 3 * ATTN_WIDTH
CONV_COLS = 4 * CONV_WIDTH
GATE_COLS = 2 * D_MODEL
IN_COLS = QKV_COLS + ATTN_WIDTH + CONV_COLS + GATE_COLS
MASK_VALUE = -1e30
LOG2_E = math.log2(math.e)
Q_SCALE_LOG2 = HEAD_DIM ** -0.5 * LOG2_E

PROJ_ROWS = 512
PROJ_COL_CHUNK = 512
ATTN_Q_ROWS = 512
ATTN_K_ROWS = 1024
TK_SHIFT = ATTN_K_ROWS.bit_length() - 1
ATTN_SUB_ROWS = 256
ATTN_LOOKAHEAD = 1
assert ATTN_K_ROWS == 1 << TK_SHIFT and ATTN_K_ROWS % ATTN_Q_ROWS == 0
OUT_ROWS = 512
HALO_ROWS = 16
VMEM_LIMIT_BYTES = 56 * 1024 * 1024

BF16 = jnp.bfloat16
F32 = jnp.float32


def _lambda_init(layer_idx):
    return 0.8 - 0.6 * math.exp(-0.3 * layer_idx)


def _in_proj_kernel(x_ref, g_ref, w_ref, qkv_ref, za_ref, conv_ref, gate_ref, h_ref):
    x = x_ref[...]
    ms = jnp.mean(x * x, axis=-1, keepdims=True)
    h_ref[...] = (x * lax.rsqrt(ms + NORM_EPS) * g_ref[...]).astype(BF16)
    col = 0
    for out_ref in (qkv_ref, za_ref, conv_ref, gate_ref):
        for c in range(0, out_ref.shape[1], PROJ_COL_CHUNK):
            acc = jnp.dot(h_ref[...], w_ref[:, col:col + PROJ_COL_CHUNK],
                          preferred_element_type=F32)
            if col < ATTN_WIDTH:
                acc = acc * Q_SCALE_LOG2
            out_ref[:, c:c + PROJ_COL_CHUNK] = acc.astype(BF16)
            col += PROJ_COL_CHUNK


def _in_proj(x2d, g_pre, w_in_bf16):
    n = x2d.shape[0]
    row_block = lambda cols: pl.BlockSpec((PROJ_ROWS, cols), lambda i: (i, 0))
    return pl.pallas_call(
        _in_proj_kernel,
        grid=(n // PROJ_ROWS,),
        in_specs=[
            row_block(D_MODEL),
            pl.BlockSpec((1, D_MODEL), lambda i: (0, 0)),
            pl.BlockSpec((D_MODEL, IN_COLS), lambda i: (0, 0), pipeline_mode=pl.Buffered(1)),
        ],
        out_specs=[row_block(QKV_COLS), row_block(ATTN_WIDTH), row_block(CONV_COLS),
                   row_block(GATE_COLS)],
        out_shape=[jax.ShapeDtypeStruct((n, c), BF16)
                   for c in (QKV_COLS, ATTN_WIDTH, CONV_COLS, GATE_COLS)],
        scratch_shapes=[pltpu.VMEM((PROJ_ROWS, D_MODEL), BF16)],
        compiler_params=pltpu.CompilerParams(
            dimension_semantics=("arbitrary",), vmem_limit_bytes=VMEM_LIMIT_BYTES),
        name="in_proj",
    )(x2d, g_pre, w_in_bf16)


def _diff_attn_kernel(slopes_ref, lq1_ref, lk1_ref, lq2_ref, lk2_ref, gain_ref,
                      q_ref, k_ref, v_ref, o_ref, ka_ref, kb_ref, va_ref, m_ref, acc_ref,
                      *, lam_init):
    tq, tk, ts = ATTN_Q_ROWS, ATTN_K_ROWS, ATTN_SUB_ROWS
    seq = k_ref.shape[0]
    head = pl.program_id(1)
    qi = pl.program_id(2)
    slope = slopes_ref[head] * LOG2_E

    @pl.when(qi == 0)
    def _build_keys_values():
        lane = lax.broadcasted_iota(jnp.int32, (tk, HEAD_COLS), 1)
        b = lax.broadcasted_iota(jnp.int32, (tk, HEAD_COLS), 0).astype(F32) * slope
        b1 = b.astype(BF16).astype(F32)
        b2 = (b - b1).astype(BF16).astype(F32)
        b3 = (b - b1 - b2).astype(BF16).astype(F32)

        def bias_lanes(first):
            return jnp.where(lane == first, b1, jnp.where(
                lane == first + 1, b2, jnp.where(lane == first + 2, b3, 0.0)))

        bias_a = bias_lanes(HEAD_DIM)
        bias_b = bias_lanes(0)
        for c in range(0, seq, tk):
            k = k_ref[c:c + tk, :].astype(F32)
            ka_ref[c:c + tk, :] = jnp.where(lane < HEAD_DIM, k, bias_a).astype(BF16)
            kb_ref[c:c + tk, :] = jnp.where(lane >= HEAD_DIM, k, bias_b).astype(BF16)
            va_ref[c:c + tk, 0:HEAD_COLS] = v_ref[c:c + tk, :]
            va_ref[c:c + tk, HEAD_COLS:2 * HEAD_COLS] = jnp.ones((tk, HEAD_COLS), BF16)

    q = q_ref[...].astype(F32)
    qlane = lax.broadcasted_iota(jnp.int32, q.shape, 1)
    q_maps = (
        jnp.where(qlane < HEAD_DIM, q, jnp.where(qlane < HEAD_DIM + 3, 1.0, 0.0)).astype(BF16),
        jnp.where(qlane >= HEAD_DIM, q, jnp.where(qlane < 3, 1.0, 0.0)).astype(BF16),
    )
    key_refs = (ka_ref, kb_ref)

    m_ref[...] = jnp.full(m_ref.shape, MASK_VALUE, F32)
    acc_ref[...] = jnp.zeros(acc_ref.shape, F32)

    tri_keep = (lax.broadcasted_iota(jnp.int32, (ts, ts), 1)
                <= lax.broadcasted_iota(jnp.int32, (ts, ts), 0))

    def step(key_start, diag_cols):
        span_bias = (key_start - qi * tq).astype(F32) * slope
        streams = [(r, mp) for r in range(0, tq, ts) for mp in range(2)]

        def width(r):
            return tk if diag_cols is None else diag_cols + r + ts

        def scores(r, mp):
            keys = key_refs[mp][pl.ds(key_start, width(r)), :]
            s = lax.dot_general(q_maps[mp][r:r + ts], keys, (((1,), (1,)), ((), ())),
                                preferred_element_type=F32)
            if diag_cols is not None:
                below = width(r) - ts
                blocks = [s[:, :below]] if below else []
                s = jnp.concatenate(blocks + [jnp.where(tri_keep, s[:, below:], MASK_VALUE)],
                                    axis=1)
            return s

        def update(r, mp, s):
            m_prev = m_ref[mp, r:r + ts]
            m_new = jnp.maximum(m_prev, jnp.max(s, axis=-1, keepdims=True) + span_bias)
            alpha = jnp.exp2(m_prev - m_new)
            shift = m_new - span_bias
            p = jnp.exp2(s - jnp.concatenate([shift] * (width(r) // HEAD_COLS), axis=1))
            pv = jnp.dot(p.astype(BF16), va_ref[pl.ds(key_start, width(r)), :],
                         preferred_element_type=F32)
            acc_ref[mp, r:r + ts] = (
                jnp.concatenate([alpha, alpha], axis=1) * acc_ref[mp, r:r + ts] + pv)
            m_ref[mp, r:r + ts] = m_new

        pending = [scores(*st) for st in streams[:ATTN_LOOKAHEAD]]
        for i, st in enumerate(streams):
            if i + ATTN_LOOKAHEAD < len(streams):
                pending.append(scores(*streams[i + ATTN_LOOKAHEAD]))
            update(*st, pending.pop(0))

    n_spans = lax.shift_right_logical(qi * tq, TK_SHIFT)

    def full_span(kj, carry):
        step(pl.multiple_of(kj * tk, tk), None)
        return carry

    lax.fori_loop(0, n_spans, full_span, 0)
    tail_start = pl.multiple_of(n_spans * tk, tk)
    for extra in range(0, tk, tq):
        @pl.when(qi * tq - tail_start == extra)
        def _diag_step(extra=extra):
            step(tail_start, extra)

    lam = (jnp.exp(jnp.sum(lq1_ref[...] * lk1_ref[...], axis=-1, keepdims=True))
           - jnp.exp(jnp.sum(lq2_ref[...] * lk2_ref[...], axis=-1, keepdims=True))
           + lam_init)
    o = (acc_ref[0, :, 0:HEAD_COLS] / acc_ref[0, :, HEAD_COLS:2 * HEAD_COLS]
         - lam * (acc_ref[1, :, 0:HEAD_COLS] / acc_ref[1, :, HEAD_COLS:2 * HEAD_COLS]))
    ms = jnp.mean(o * o, axis=-1, keepdims=True)
    o = (o * lax.rsqrt(ms + NORM_EPS) * gain_ref[...]) * (1.0 - lam_init)
    o_ref[...] = o.astype(BF16)


def _diff_attn(qkv, slopes, lq1, lk1, lq2, lk2, gain, *, batch, seq, lam_init):
    nq = seq // ATTN_Q_ROWS
    small = lambda cols: pl.BlockSpec((1, cols), lambda b, h, i: (0, 0))
    return pl.pallas_call(
        functools.partial(_diff_attn_kernel, lam_init=lam_init),
        grid=(batch, ATTN_HEADS, nq),
        in_specs=[
            pl.BlockSpec(memory_space=pltpu.SMEM),
            small(HEAD_DIM), small(HEAD_DIM), small(HEAD_DIM), small(HEAD_DIM),
            small(HEAD_COLS),
            pl.BlockSpec((ATTN_Q_ROWS, HEAD_COLS), lambda b, h, i: (b * nq + i, h)),
            pl.BlockSpec((seq, HEAD_COLS), lambda b, h, i: (b, ATTN_HEADS + h)),
            pl.BlockSpec((seq, HEAD_COLS), lambda b, h, i: (b, 2 * ATTN_HEADS + h)),
        ],
        out_specs=pl.BlockSpec((ATTN_Q_ROWS, HEAD_COLS), lambda b, h, i: (b * nq + i, h)),
        out_shape=jax.ShapeDtypeStruct((batch * seq, ATTN_WIDTH), BF16),
        scratch_shapes=[pltpu.VMEM((seq, HEAD_COLS), BF16),
                        pltpu.VMEM((seq, HEAD_COLS), BF16),
                        pltpu.VMEM((seq, 2 * HEAD_COLS), BF16),
                        pltpu.VMEM((2, ATTN_Q_ROWS, HEAD_COLS), F32),
                        pltpu.VMEM((2, ATTN_Q_ROWS, 2 * HEAD_COLS), F32)],
        compiler_params=pltpu.CompilerParams(
            dimension_semantics=("arbitrary", "arbitrary", "arbitrary"),
            vmem_limit_bytes=VMEM_LIMIT_BYTES),
        name="diff_attn",
    )(slopes, lq1, lk1, lq2, lk2, gain, qkv, qkv, qkv)


def _sigmoid(z):
    return 1.0 / (1.0 + jnp.exp(-z))


def _out_proj_kernel(x_ref, o_ref, za_ref, conv_ref, halo_ref, gate_ref, cw_ref, wa_ref,
                     wc_ref, bm_ref, wo_ref, gp_ref, out_ref, u_ref, *, tiles_per_seq):
    tm, cwid = OUT_ROWS, CONV_WIDTH
    first = pl.program_id(0) % tiles_per_seq == 0

    z_a = za_ref[...].astype(F32)
    ya_in = o_ref[...].astype(F32) * (z_a * _sigmoid(z_a))
    y_attn = jnp.dot(ya_in.astype(BF16), wa_ref[...], preferred_element_type=F32)

    cb = conv_ref[:, 0:cwid].astype(F32)
    u = conv_ref[:, cwid:2 * cwid].astype(F32) * conv_ref[:, 2 * cwid:3 * cwid].astype(F32)
    z_c = conv_ref[:, 3 * cwid:4 * cwid].astype(F32)
    halo = (halo_ref[:, cwid:2 * cwid].astype(F32) * halo_ref[:, 2 * cwid:3 * cwid].astype(F32))
    u_ref[0:HALO_ROWS, :] = jnp.where(first, 0.0, halo)
    u_ref[HALO_ROWS:HALO_ROWS + tm, :] = u
    conv = cw_ref[CONV_K - 1:CONV_K, :] * u
    for k in range(CONV_K - 1):
        shift = CONV_K - 1 - k
        conv = conv + cw_ref[k:k + 1, :] * u_ref[HALO_ROWS - shift:HALO_ROWS - shift + tm, :]
    yc_in = (cb * conv) * (z_c * _sigmoid(z_c))
    y_conv = jnp.dot(yc_in.astype(BF16), wc_ref[...], preferred_element_type=F32)

    g_a = _sigmoid(gate_ref[:, 0:D_MODEL].astype(F32) + bm_ref[:, 0:D_MODEL])
    g_c = _sigmoid(gate_ref[:, D_MODEL:2 * D_MODEL].astype(F32) + bm_ref[:, D_MODEL:2 * D_MODEL])
    y = g_a * y_attn + g_c * y_conv
    out = jnp.dot(y.astype(BF16), wo_ref[...], preferred_element_type=F32)
    ms = jnp.mean(out * out, axis=-1, keepdims=True)
    out_ref[...] = x_ref[...] + out * lax.rsqrt(ms + NORM_EPS) * gp_ref[...]


def _out_proj(x2d, o, za, conv, gate, conv_w, wa, wc, b_merge, wo, g_post, *, seq):
    n = x2d.shape[0]
    tiles_per_seq = seq // OUT_ROWS
    halo_blocks = OUT_ROWS // HALO_ROWS
    row_block = lambda cols: pl.BlockSpec((OUT_ROWS, cols), lambda i: (i, 0))
    whole = lambda shape: pl.BlockSpec(shape, lambda i: (0, 0))
    return pl.pallas_call(
        functools.partial(_out_proj_kernel, tiles_per_seq=tiles_per_seq),
        grid=(n // OUT_ROWS,),
        in_specs=[
            row_block(D_MODEL), row_block(ATTN_WIDTH), row_block(ATTN_WIDTH), row_block(CONV_COLS),
            pl.BlockSpec((HALO_ROWS, CONV_COLS),
                         lambda i: (jnp.maximum(i * halo_blocks - 1, 0), 0)),
            row_block(GATE_COLS),
            whole((CONV_K, CONV_WIDTH)), whole((ATTN_WIDTH, D_MODEL)),
            whole((CONV_WIDTH, D_MODEL)), whole((1, 2 * D_MODEL)), whole((D_MODEL, D_MODEL)),
            whole((1, D_MODEL)),
        ],
        out_specs=row_block(D_MODEL),
        out_shape=jax.ShapeDtypeStruct((n, D_MODEL), F32),
        scratch_shapes=[pltpu.VMEM((HALO_ROWS + OUT_ROWS, CONV_WIDTH), F32)],
        compiler_params=pltpu.CompilerParams(
            dimension_semantics=("arbitrary",), vmem_limit_bytes=VMEM_LIMIT_BYTES),
        name="out_proj",
    )(x2d, o, za, conv, conv, gate, conv_w, wa, wc, b_merge, wo, g_post)


def _layer(x, layer_idx, w_in, lq1, lk1, lq2, lk2, subln_gain, conv_w, w_attn_o, w_conv_o,
           b_merge, w_out, g_pre, g_post):
    batch, seq, d = x.shape
    x2d = x.reshape(batch * seq, d)
    row = lambda a: a.reshape(1, -1).astype(F32)
    slopes = jnp.asarray([2.0 ** (-8.0 * (i + 1) / ATTN_HEADS) for i in range(ATTN_HEADS)], F32)

    qkv, za, conv, gate = _in_proj(x2d, row(g_pre), w_in.astype(BF16))
    o = _diff_attn(qkv, slopes, row(lq1), row(lk1), row(lq2), row(lk2), row(subln_gain),
                   batch=batch, seq=seq, lam_init=_lambda_init(layer_idx))
    out = _out_proj(x2d, o, za, conv, gate, conv_w.astype(F32), w_attn_o.astype(BF16),
                    w_conv_o.astype(BF16), row(b_merge), w_out.astype(BF16), row(g_post), seq=seq)
    return out.reshape(batch, seq, d)


def kernel(x, w_in, lambda_q1, lambda_k1, lambda_q2, lambda_k2, subln_gain, conv_w, w_attn_o,
           w_conv_o, b_merge, w_out, g_pre, g_post):
    for l in range(w_in.shape[0]):
        x = _layer(x, l, w_in[l], lambda_q1[l], lambda_k1[l], lambda_q2[l], lambda_k2[l],
                   subln_gain[l], conv_w[l], w_attn_o[l], w_conv_o[l], b_merge[l], w_out[l],
                   g_pre[l], g_post[l])
    return x
```

```python
import functools
import math

import jax
import jax.numpy as jnp
from jax import lax
from jax.experimental import pallas as pl
from jax.experimental.pallas import tpu as pltpu

D_MODEL = 1024
ATTN_HEADS = 4
HEAD_DIM = 64
HEAD_COLS = 2 * HEAD_DIM
ATTN_WIDTH = ATTN_HEADS * HEAD_COLS
CONV_WIDTH = D_MODEL // 2
CONV_K = 3
NORM_EPS = 1e-6
QKV_COLS = 3 * ATTN_WIDTH
CONV_COLS = 4 * CONV_WIDTH
GATE_COLS = 2 * D_MODEL
IN_COLS = QKV_COLS + ATTN_WIDTH + CONV_COLS + GATE_COLS
MASK_VALUE = -1e30
LOG2_E = math.log2(math.e)
Q_SCALE_LOG2 = HEAD_DIM ** -0.5 * LOG2_E

PROJ_ROWS = 512
PROJ_COL_CHUNK = 512
ATTN_Q_ROWS = 1024
ATTN_K_ROWS = 1024
TK_SHIFT = ATTN_K_ROWS.bit_length() - 1
ATTN_SUB_ROWS = 256
ATTN_LOOKAHEAD = 1
assert ATTN_K_ROWS == 1 << TK_SHIFT and ATTN_K_ROWS % ATTN_Q_ROWS == 0
OUT_ROWS = 512
HALO_ROWS = 16
VMEM_LIMIT_BYTES = 56 * 1024 * 1024

BF16 = jnp.bfloat16
F32 = jnp.float32


def _lambda_init(layer_idx):
    return 0.8 - 0.6 * math.exp(-0.3 * layer_idx)


def _in_proj_kernel(x_ref, g_ref, w_ref, qkv_ref, za_ref, conv_ref, gate_ref, h_ref):
    x = x_ref[...]
    ms = jnp.mean(x * x, axis=-1, keepdims=True)
    h_ref[...] = (x * lax.rsqrt(ms + NORM_EPS) * g_ref[...]).astype(BF16)
    col = 0
    for out_ref in (qkv_ref, za_ref, conv_ref, gate_ref):
        for c in range(0, out_ref.shape[1], PROJ_COL_CHUNK):
            acc = jnp.dot(h_ref[...], w_ref[:, col:col + PROJ_COL_CHUNK],
                          preferred_element_type=F32)
            if col < ATTN_WIDTH:
                acc = acc * Q_SCALE_LOG2
            out_ref[:, c:c + PROJ_COL_CHUNK] = acc.astype(BF16)
            col += PROJ_COL_CHUNK


def _in_proj(x2d, g_pre, w_in_bf16):
    n = x2d.shape[0]
    row_block = lambda cols: pl.BlockSpec((PROJ_ROWS, cols), lambda i: (i, 0))
    return pl.pallas_call(
        _in_proj_kernel,
        grid=(n // PROJ_ROWS,),
        in_specs=[
            row_block(D_MODEL),
            pl.BlockSpec((1, D_MODEL), lambda i: (0, 0)),
            pl.BlockSpec((D_MODEL, IN_COLS), lambda i: (0, 0), pipeline_mode=pl.Buffered(1)),
        ],
        out_specs=[row_block(QKV_COLS), row_block(ATTN_WIDTH), row_block(CONV_COLS),
                   row_block(GATE_COLS)],
        out_shape=[jax.ShapeDtypeStruct((n, c), BF16)
                   for c in (QKV_COLS, ATTN_WIDTH, CONV_COLS, GATE_COLS)],
        scratch_shapes=[pltpu.VMEM((PROJ_ROWS, D_MODEL), BF16)],
        compiler_params=pltpu.CompilerParams(
            dimension_semantics=("arbitrary",), vmem_limit_bytes=VMEM_LIMIT_BYTES),
        name="in_proj",
    )(x2d, g_pre, w_in_bf16)


def _diff_attn_kernel(slopes_ref, lq1_ref, lk1_ref, lq2_ref, lk2_ref, gain_ref,
                      q_ref, k_ref, v_ref, o_ref, ka_ref, kb_ref, va_ref, m_ref, acc_ref,
                      *, lam_init):
    tq, tk, ts = ATTN_Q_ROWS, ATTN_K_ROWS, ATTN_SUB_ROWS
    seq = k_ref.shape[0]
    head = pl.program_id(1)
    qi = pl.program_id(2)
    slope = slopes_ref[head] * LOG2_E

    @pl.when(qi == 0)
    def _build_keys_values():
        lane = lax.broadcasted_iota(jnp.int32, (tk, HEAD_COLS), 1)
        b = lax.broadcasted_iota(jnp.int32, (tk, HEAD_COLS), 0).astype(F32) * slope
        b1 = b.astype(BF16).astype(F32)
        b2 = (b - b1).astype(BF16).astype(F32)
        b3 = (b - b1 - b2).astype(BF16).astype(F32)

        def bias_lanes(first):
            return jnp.where(lane == first, b1, jnp.where(
                lane == first + 1, b2, jnp.where(lane == first + 2, b3, 0.0)))

        bias_a = bias_lanes(HEAD_DIM)
        bias_b = bias_lanes(0)
        for c in range(0, seq, tk):
            k = k_ref[c:c + tk, :].astype(F32)
            ka_ref[c:c + tk, :] = jnp.where(lane < HEAD_DIM, k, bias_a).astype(BF16)
            kb_ref[c:c + tk, :] = jnp.where(lane >= HEAD_DIM, k, bias_b).astype(BF16)
            va_ref[c:c + tk, 0:HEAD_COLS] = v_ref[c:c + tk, :]
            va_ref[c:c + tk, HEAD_COLS:2 * HEAD_COLS] = jnp.ones((tk, HEAD_COLS), BF16)

    q = q_ref[...].astype(F32)
    qlane = lax.broadcasted_iota(jnp.int32, q.shape, 1)
    q_maps = (
        jnp.where(qlane < HEAD_DIM, q, jnp.where(qlane < HEAD_DIM + 3, 1.0, 0.0)).astype(BF16),
        jnp.where(qlane >= HEAD_DIM, q, jnp.where(qlane < 3, 1.0, 0.0)).astype(BF16),
    )
    key_refs = (ka_ref, kb_ref)

    m_ref[...] = jnp.full(m_ref.shape, MASK_VALUE, F32)
    acc_ref[...] = jnp.zeros(acc_ref.shape, F32)

    tri_keep = (lax.broadcasted_iota(jnp.int32, (ts, ts), 1)
                <= lax.broadcasted_iota(jnp.int32, (ts, ts), 0))

    def step(key_start, diag_cols):
        span_bias = (key_start - qi * tq).astype(F32) * slope
        streams = [(r, mp) for r in range(0, tq, ts) for mp in range(2)]

        def width(r):
            return tk if diag_cols is None else diag_cols + r + ts

        def scores(r, mp):
            keys = key_refs[mp][pl.ds(key_start, width(r)), :]
            s = lax.dot_general(q_maps[mp][r:r + ts], keys, (((1,), (1,)), ((), ())),
                                preferred_element_type=F32)
            if diag_cols is not None:
                below = width(r) - ts
                blocks = [s[:, :below]] if below else []
                s = jnp.concatenate(blocks + [jnp.where(tri_keep, s[:, below:], MASK_VALUE)],
                                    axis=1)
            return s

        def update(r, mp, s):
            m_prev = m_ref[mp, r:r + ts]
            m_new = jnp.maximum(m_prev, jnp.max(s, axis=-1, keepdims=True) + span_bias)
            alpha = jnp.exp2(m_prev - m_new)
            shift = m_new - span_bias
            p = jnp.exp2(s - jnp.concatenate([shift] * (width(r) // HEAD_COLS), axis=1))
            pv = jnp.dot(p.astype(BF16), va_ref[pl.ds(key_start, width(r)), :],
                         preferred_element_type=F32)
            acc_ref[mp, r:r + ts] = (
                jnp.concatenate([alpha, alpha], axis=1) * acc_ref[mp, r:r + ts] + pv)
            m_ref[mp, r:r + ts] = m_new

        pending = [scores(*st) for st in streams[:ATTN_LOOKAHEAD]]
        for i, st in enumerate(streams):
            if i + ATTN_LOOKAHEAD < len(streams):
                pending.append(scores(*streams[i + ATTN_LOOKAHEAD]))
            update(*st, pending.pop(0))

    n_spans = lax.shift_right_logical(qi * tq, TK_SHIFT)

    def full_span(kj, carry):
        step(pl.multiple_of(kj * tk, tk), None)
        return carry

    lax.fori_loop(0, n_spans, full_span, 0)
    tail_start = pl.multiple_of(n_spans * tk, tk)
    for extra in range(0, tk, tq):
        @pl.when(qi * tq - tail_start == extra)
        def _diag_step(extra=extra):
            step(tail_start, extra)

    lam = (jnp.exp(jnp.sum(lq1_ref[...] * lk1_ref[...], axis=-1, keepdims=True))
           - jnp.exp(jnp.sum(lq2_ref[...] * lk2_ref[...], axis=-1, keepdims=True))
           + lam_init)
    o = (acc_ref[0, :, 0:HEAD_COLS] / acc_ref[0, :, HEAD_COLS:2 * HEAD_COLS]
         - lam * (acc_ref[1, :, 0:HEAD_COLS] / acc_ref[1, :, HEAD_COLS:2 * HEAD_COLS]))
    ms = jnp.mean(o * o, axis=-1, keepdims=True)
    o = (o * lax.rsqrt(ms + NORM_EPS) * gain_ref[...]) * (1.0 - lam_init)
    o_ref[...] = o.astype(BF16)


def _diff_attn(qkv, slopes, lq1, lk1, lq2, lk2, gain, *, batch, seq, lam_init):
    nq = seq // ATTN_Q_ROWS
    small = lambda cols: pl.BlockSpec((1, cols), lambda b, h, i: (0, 0))
    return pl.pallas_call(
        functools.partial(_diff_attn_kernel, lam_init=lam_init),
        grid=(batch, ATTN_HEADS, nq),
        in_specs=[
            pl.BlockSpec(memory_space=pltpu.SMEM),
            small(HEAD_DIM), small(HEAD_DIM), small(HEAD_DIM), small(HEAD_DIM),
            small(HEAD_COLS),
            pl.BlockSpec((ATTN_Q_ROWS, HEAD_COLS), lambda b, h, i: (b * nq + i, h)),
            pl.BlockSpec((seq, HEAD_COLS), lambda b, h, i: (b, ATTN_HEADS + h)),
            pl.BlockSpec((seq, HEAD_COLS), lambda b, h, i: (b, 2 * ATTN_HEADS + h)),
        ],
        out_specs=pl.BlockSpec((ATTN_Q_ROWS, HEAD_COLS), lambda b, h, i: (b * nq + i, h)),
        out_shape=jax.ShapeDtypeStruct((batch * seq, ATTN_WIDTH), BF16),
        scratch_shapes=[pltpu.VMEM((seq, HEAD_COLS), BF16),
                        pltpu.VMEM((seq, HEAD_COLS), BF16),
                        pltpu.VMEM((seq, 2 * HEAD_COLS), BF16),
                        pltpu.VMEM((2, ATTN_Q_ROWS, HEAD_COLS), F32),
                        pltpu.VMEM((2, ATTN_Q_ROWS, 2 * HEAD_COLS), F32)],
        compiler_params=pltpu.CompilerParams(
            dimension_semantics=("arbitrary", "arbitrary", "arbitrary"),
            vmem_limit_bytes=VMEM_LIMIT_BYTES),
        name="diff_attn",
    )(slopes, lq1, lk1, lq2, lk2, gain, qkv, qkv, qkv)


def _sigmoid(z):
    return 1.0 / (1.0 + jnp.exp(-z))


def _out_proj_kernel(x_ref, o_ref, za_ref, conv_ref, halo_ref, gate_ref, cw_ref, wa_ref,
                     wc_ref, bm_ref, wo_ref, gp_ref, out_ref, u_ref, *, tiles_per_seq):
    tm, cwid = OUT_ROWS, CONV_WIDTH
    first = pl.program_id(0) % tiles_per_seq == 0

    z_a = za_ref[...].astype(F32)
    ya_in = o_ref[...].astype(F32) * (z_a * _sigmoid(z_a))
    y_attn = jnp.dot(ya_in.astype(BF16), wa_ref[...], preferred_element_type=F32)

    cb = conv_ref[:, 0:cwid].astype(F32)
    u = conv_ref[:, cwid:2 * cwid].astype(F32) * conv_ref[:, 2 * cwid:3 * cwid].astype(F32)
    z_c = conv_ref[:, 3 * cwid:4 * cwid].astype(F32)
    halo = (halo_ref[:, cwid:2 * cwid].astype(F32) * halo_ref[:, 2 * cwid:3 * cwid].astype(F32))
    u_ref[0:HALO_ROWS, :] = jnp.where(first, 0.0, halo)
    u_ref[HALO_ROWS:HALO_ROWS + tm, :] = u
    conv = cw_ref[CONV_K - 1:CONV_K, :] * u
    for k in range(CONV_K - 1):
        shift = CONV_K - 1 - k
        conv = conv + cw_ref[k:k + 1, :] * u_ref[HALO_ROWS - shift:HALO_ROWS - shift + tm, :]
    yc_in = (cb * conv) * (z_c * _sigmoid(z_c))
    y_conv = jnp.dot(yc_in.astype(BF16), wc_ref[...], preferred_element_type=F32)

    g_a = _sigmoid(gate_ref[:, 0:D_MODEL].astype(F32) + bm_ref[:, 0:D_MODEL])
    g_c = _sigmoid(gate_ref[:, D_MODEL:2 * D_MODEL].astype(F32) + bm_ref[:, D_MODEL:2 * D_MODEL])
    y = g_a * y_attn + g_c * y_conv
    out = jnp.dot(y.astype(BF16), wo_ref[...], preferred_element_type=F32)
    ms = jnp.mean(out * out, axis=-1, keepdims=True)
    out_ref[...] = x_ref[...] + out * lax.rsqrt(ms + NORM_EPS) * gp_ref[...]


def _out_proj(x2d, o, za, conv, gate, conv_w, wa, wc, b_merge, wo, g_post, *, seq):
    n = x2d.shape[0]
    tiles_per_seq = seq // OUT_ROWS
    halo_blocks = OUT_ROWS // HALO_ROWS
    row_block = lambda cols: pl.BlockSpec((OUT_ROWS, cols), lambda i: (i, 0))
    whole = lambda shape: pl.BlockSpec(shape, lambda i: (0, 0))
    return pl.pallas_call(
        functools.partial(_out_proj_kernel, tiles_per_seq=tiles_per_seq),
        grid=(n // OUT_ROWS,),
        in_specs=[
            row_block(D_MODEL), row_block(ATTN_WIDTH), row_block(ATTN_WIDTH), row_block(CONV_COLS),
            pl.BlockSpec((HALO_ROWS, CONV_COLS),
                         lambda i: (jnp.maximum(i * halo_blocks - 1, 0), 0)),
            row_block(GATE_COLS),
            whole((CONV_K, CONV_WIDTH)), whole((ATTN_WIDTH, D_MODEL)),
            whole((CONV_WIDTH, D_MODEL)), whole((1, 2 * D_MODEL)), whole((D_MODEL, D_MODEL)),
            whole((1, D_MODEL)),
        ],
        out_specs=row_block(D_MODEL),
        out_shape=jax.ShapeDtypeStruct((n, D_MODEL), F32),
        scratch_shapes=[pltpu.VMEM((HALO_ROWS + OUT_ROWS, CONV_WIDTH), F32)],
        compiler_params=pltpu.CompilerParams(
            dimension_semantics=("arbitrary",), vmem_limit_bytes=VMEM_LIMIT_BYTES),
        name="out_proj",
    )(x2d, o, za, conv, conv, gate, conv_w, wa, wc, b_merge, wo, g_post)


def _layer(x, layer_idx, w_in, lq1, lk1, lq2, lk2, subln_gain, conv_w, w_attn_o, w_conv_o,
           b_merge, w_out, g_pre, g_post):
    batch, seq, d = x.shape
    x2d = x.reshape(batch * seq, d)
    row = lambda a: a.reshape(1, -1).astype(F32)
    slopes = jnp.asarray([2.0 ** (-8.0 * (i + 1) / ATTN_HEADS) for i in range(ATTN_HEADS)], F32)

    qkv, za, conv, gate = _in_proj(x2d, row(g_pre), w_in.astype(BF16))
    o = _diff_attn(qkv, slopes, row(lq1), row(lk1), row(lq2), row(lk2), row(subln_gain),
                   batch=batch, seq=seq, lam_init=_lambda_init(layer_idx))
    out = _out_proj(x2d, o, za, conv, gate, conv_w.astype(F32), w_attn_o.astype(BF16),
                    w_conv_o.astype(BF16), row(b_merge), w_out.astype(BF16), row(g_post), seq=seq)
    return out.reshape(batch, seq, d)


def kernel(x, w_in, lambda_q1, lambda_k1, lambda_q2, lambda_k2, subln_gain, conv_w, w_attn_o,
           w_conv_o, b_merge, w_out, g_pre, g_post):
    for l in range(w_in.shape[0]):
        x = _layer(x, l, w_in[l], lambda_q1[l], lambda_k1[l], lambda_q2[l], lambda_k2[l],
                   subln_gain[l], conv_w[l], w_attn_o[l], w_conv_o[l], b_merge[l], w_out[l],
                   g_pre[l], g_post[l])
    return x
```

```python
import functools
import math

import jax
import jax.numpy as jnp
from jax import lax
from jax.experimental import pallas as pl
from jax.experimental.pallas import tpu as pltpu

D_MODEL = 1024
ATTN_HEADS = 4
HEAD_DIM = 64
HEAD_COLS = 2 * HEAD_DIM
ATTN_WIDTH = ATTN_HEADS * HEAD_COLS
CONV_WIDTH = D_MODEL // 2
CONV_K = 3
NORM_EPS = 1e-6
QKV_COLS = 3 * ATTN_WIDTH
CONV_COLS = 4 * CONV_WIDTH
GATE_COLS = 2 * D_MODEL
IN_COLS = QKV_COLS + ATTN_WIDTH + CONV_COLS + GATE_COLS
MASK_VALUE = -1e30
LOG2_E = math.log2(math.e)
Q_SCALE_LOG2 = HEAD_DIM ** -0.5 * LOG2_E

PROJ_ROWS = 512
PROJ_COL_CHUNK = 512
ATTN_Q_ROWS = 1024
ATTN_K_ROWS = 1024
TK_SHIFT = ATTN_K_ROWS.bit_length() - 1
ATTN_SUB_ROWS = 256
ATTN_LOOKAHEAD = 1
assert ATTN_K_ROWS == 1 << TK_SHIFT and ATTN_K_ROWS % ATTN_Q_ROWS == 0
OUT_ROWS = 512
HALO_ROWS = 16
VMEM_LIMIT_BYTES = 56 * 1024 * 1024

BF16 = jnp.bfloat16
F32 = jnp.float32


def _lambda_init(layer_idx):
    return 0.8 - 0.6 * math.exp(-0.3 * layer_idx)


def _in_proj_kernel(x_ref, g_ref, w_ref, qkv_ref, za_ref, conv_ref, gate_ref, h_ref):
    x = x_ref[...]
    ms = jnp.mean(x * x, axis=-1, keepdims=True)
    h_ref[...] = (x * lax.rsqrt(ms + NORM_EPS) * g_ref[...]).astype(BF16)
    col = 0
    for out_ref in (qkv_ref, za_ref, conv_ref, gate_ref):
        for c in range(0, out_ref.shape[1], PROJ_COL_CHUNK):
            acc = jnp.dot(h_ref[...], w_ref[:, col:col + PROJ_COL_CHUNK],
                          preferred_element_type=F32)
            if col < ATTN_WIDTH:
                acc = acc * Q_SCALE_LOG2
            out_ref[:, c:c + PROJ_COL_CHUNK] = acc.astype(BF16)
            col += PROJ_COL_CHUNK


def _in_proj(x2d, g_pre, w_in_bf16):
    n = x2d.shape[0]
    row_block = lambda cols: pl.BlockSpec((PROJ_ROWS, cols), lambda i: (i, 0))
    return pl.pallas_call(
        _in_proj_kernel,
        grid=(n // PROJ_ROWS,),
        in_specs=[
            row_block(D_MODEL),
            pl.BlockSpec((1, D_MODEL), lambda i: (0, 0)),
            pl.BlockSpec((D_MODEL, IN_COLS), lambda i: (0, 0), pipeline_mode=pl.Buffered(1)),
        ],
        out_specs=[row_block(QKV_COLS), row_block(ATTN_WIDTH), row_block(CONV_COLS),
                   row_block(GATE_COLS)],
        out_shape=[jax.ShapeDtypeStruct((n, c), BF16)
                   for c in (QKV_COLS, ATTN_WIDTH, CONV_COLS, GATE_COLS)],
        scratch_shapes=[pltpu.VMEM((PROJ_ROWS, D_MODEL), BF16)],
        compiler_params=pltpu.CompilerParams(
            dimension_semantics=("arbitrary",), vmem_limit_bytes=VMEM_LIMIT_BYTES),
        name="in_proj",
    )(x2d, g_pre, w_in_bf16)


def _diff_attn_kernel(slopes_ref, lq1_ref, lk1_ref, lq2_ref, lk2_ref, gain_ref,
                      q_ref, k_ref, v_ref, o_ref, ka_ref, kb_ref, va_ref, m_ref, acc_ref,
                      *, lam_init):
    tq, tk, ts = ATTN_Q_ROWS, ATTN_K_ROWS, ATTN_SUB_ROWS
    seq = k_ref.shape[0]
    head = pl.program_id(1)
    qi = pl.program_id(2)
    slope = slopes_ref[head] * LOG2_E

    @pl.when(qi == 0)
    def _build_keys_values():
        lane = lax.broadcasted_iota(jnp.int32, (tk, HEAD_COLS), 1)
        b = lax.broadcasted_iota(jnp.int32, (tk, HEAD_COLS), 0).astype(F32) * slope
        b1 = b.astype(BF16).astype(F32)
        b2 = (b - b1).astype(BF16).astype(F32)
        b3 = (b - b1 - b2).astype(BF16).astype(F32)

        def bias_lanes(first):
            return jnp.where(lane == first, b1, jnp.where(
                lane == first + 1, b2, jnp.where(lane == first + 2, b3, 0.0)))

        bias_a = bias_lanes(HEAD_DIM)
        bias_b = bias_lanes(0)
        for c in range(0, seq, tk):
            k = k_ref[c:c + tk, :].astype(F32)
            ka_ref[c:c + tk, :] = jnp.where(lane < HEAD_DIM, k, bias_a).astype(BF16)
            kb_ref[c:c + tk, :] = jnp.where(lane >= HEAD_DIM, k, bias_b).astype(BF16)
            va_ref[c:c + tk, 0:HEAD_COLS] = v_ref[c:c + tk, :]
            va_ref[c:c + tk, HEAD_COLS:2 * HEAD_COLS] = jnp.ones((tk, HEAD_COLS), BF16)

    q = q_ref[...].astype(F32)
    qlane = lax.broadcasted_iota(jnp.int32, q.shape, 1)
    q_maps = (
        jnp.where(qlane < HEAD_DIM, q, jnp.where(qlane < HEAD_DIM + 3, 1.0, 0.0)).astype(BF16),
        jnp.where(qlane >= HEAD_DIM, q, jnp.where(qlane < 3, 1.0, 0.0)).astype(BF16),
    )
    key_refs = (ka_ref, kb_ref)

    tri_keep = (lax.broadcasted_iota(jnp.int32, (ts, ts), 1)
                <= lax.broadcasted_iota(jnp.int32, (ts, ts), 0))

    def step(key_start, diag_cols):
        span_bias = (key_start - qi * tq).astype(F32) * slope
        first = diag_cols is not None
        streams = [(r, mp) for r in (range(tq - ts, -ts, -ts) if first else range(0, tq, ts))
                   for mp in range(2)]

        def width(r):
            return tk if diag_cols is None else diag_cols + r + ts

        def scores(r, mp):
            keys = key_refs[mp][pl.ds(key_start, width(r)), :]
            s = lax.dot_general(q_maps[mp][r:r + ts], keys, (((1,), (1,)), ((), ())),
                                preferred_element_type=F32)
            if diag_cols is not None:
                below = width(r) - ts
                blocks = [s[:, :below]] if below else []
                s = jnp.concatenate(blocks + [jnp.where(tri_keep, s[:, below:], MASK_VALUE)],
                                    axis=1)
            return s

        def update(r, mp, s):
            m_cur = jnp.max(s, axis=-1, keepdims=True) + span_bias
            if first:
                m_new = jnp.broadcast_to(m_cur, (ts, HEAD_COLS))
            else:
                m_prev = m_ref[mp, r:r + ts]
                m_new = jnp.maximum(m_prev, m_cur)
            shift = m_new - span_bias
            p = jnp.exp2(s - jnp.concatenate([shift] * (width(r) // HEAD_COLS), axis=1))
            pv = jnp.dot(p.astype(BF16), va_ref[pl.ds(key_start, width(r)), :],
                         preferred_element_type=F32)
            if not first:
                alpha = jnp.exp2(m_prev - m_new)
                pv = jnp.concatenate([alpha, alpha], axis=1) * acc_ref[mp, r:r + ts] + pv
            acc_ref[mp, r:r + ts] = pv
            m_ref[mp, r:r + ts] = m_new

        pending = [scores(*st) for st in streams[:ATTN_LOOKAHEAD]]
        for i, st in enumerate(streams):
            if i + ATTN_LOOKAHEAD < len(streams):
                pending.append(scores(*streams[i + ATTN_LOOKAHEAD]))
            update(*st, pending.pop(0))

    n_spans = lax.shift_right_logical(qi * tq, TK_SHIFT)
    tail_start = pl.multiple_of(n_spans * tk, tk)
    for extra in range(0, tk, tq):
        @pl.when(qi * tq - tail_start == extra)
        def _diag_step(extra=extra):
            step(tail_start, extra)

    def full_span(kj, carry):
        step(pl.multiple_of(kj * tk, tk), None)
        return carry

    lax.fori_loop(0, n_spans, full_span, 0)

    lam = (jnp.exp(jnp.sum(lq1_ref[...] * lk1_ref[...], axis=-1, keepdims=True))
           - jnp.exp(jnp.sum(lq2_ref[...] * lk2_ref[...], axis=-1, keepdims=True))
           + lam_init)
    o = (acc_ref[0, :, 0:HEAD_COLS] / acc_ref[0, :, HEAD_COLS:2 * HEAD_COLS]
         - lam * (acc_ref[1, :, 0:HEAD_COLS] / acc_ref[1, :, HEAD_COLS:2 * HEAD_COLS]))
    ms = jnp.mean(o * o, axis=-1, keepdims=True)
    o = (o * lax.rsqrt(ms + NORM_EPS) * gain_ref[...]) * (1.0 - lam_init)
    o_ref[...] = o.astype(BF16)


def _diff_attn(qkv, slopes, lq1, lk1, lq2, lk2, gain, *, batch, seq, lam_init):
    nq = seq // ATTN_Q_ROWS
    small = lambda cols: pl.BlockSpec((1, cols), lambda b, h, i: (0, 0))
    return pl.pallas_call(
        functools.partial(_diff_attn_kernel, lam_init=lam_init),
        grid=(batch, ATTN_HEADS, nq),
        in_specs=[
            pl.BlockSpec(memory_space=pltpu.SMEM),
            small(HEAD_DIM), small(HEAD_DIM), small(HEAD_DIM), small(HEAD_DIM),
            small(HEAD_COLS),
            pl.BlockSpec((ATTN_Q_ROWS, HEAD_COLS), lambda b, h, i: (b * nq + i, h)),
            pl.BlockSpec((seq, HEAD_COLS), lambda b, h, i: (b, ATTN_HEADS + h)),
            pl.BlockSpec((seq, HEAD_COLS), lambda b, h, i: (b, 2 * ATTN_HEADS + h)),
        ],
        out_specs=pl.BlockSpec((ATTN_Q_ROWS, HEAD_COLS), lambda b, h, i: (b * nq + i, h)),
        out_shape=jax.ShapeDtypeStruct((batch * seq, ATTN_WIDTH), BF16),
        scratch_shapes=[pltpu.VMEM((seq, HEAD_COLS), BF16),
                        pltpu.VMEM((seq, HEAD_COLS), BF16),
                        pltpu.VMEM((seq, 2 * HEAD_COLS), BF16),
                        pltpu.VMEM((2, ATTN_Q_ROWS, HEAD_COLS), F32),
                        pltpu.VMEM((2, ATTN_Q_ROWS, 2 * HEAD_COLS), F32)],
        compiler_params=pltpu.CompilerParams(
            dimension_semantics=("arbitrary", "arbitrary", "arbitrary"),
            vmem_limit_bytes=VMEM_LIMIT_BYTES),
        name="diff_attn",
    )(slopes, lq1, lk1, lq2, lk2, gain, qkv, qkv, qkv)


def _sigmoid(z):
    return 1.0 / (1.0 + jnp.exp(-z))


def _out_proj_kernel(x_ref, o_ref, za_ref, conv_ref, halo_ref, gate_ref, cw_ref, wa_ref,
                     wc_ref, bm_ref, wo_ref, gp_ref, out_ref, u_ref, *, tiles_per_seq):
    tm, cwid = OUT_ROWS, CONV_WIDTH
    first = pl.program_id(0) % tiles_per_seq == 0

    z_a = za_ref[...].astype(F32)
    ya_in = o_ref[...].astype(F32) * (z_a * _sigmoid(z_a))
    y_attn = jnp.dot(ya_in.astype(BF16), wa_ref[...], preferred_element_type=F32)

    cb = conv_ref[:, 0:cwid].astype(F32)
    u = conv_ref[:, cwid:2 * cwid].astype(F32) * conv_ref[:, 2 * cwid:3 * cwid].astype(F32)
    z_c = conv_ref[:, 3 * cwid:4 * cwid].astype(F32)
    halo = (halo_ref[:, cwid:2 * cwid].astype(F32) * halo_ref[:, 2 * cwid:3 * cwid].astype(F32))
    u_ref[0:HALO_ROWS, :] = jnp.where(first, 0.0, halo)
    u_ref[HALO_ROWS:HALO_ROWS + tm, :] = u
    conv = cw_ref[CONV_K - 1:CONV_K, :] * u
    for k in range(CONV_K - 1):
        shift = CONV_K - 1 - k
        conv = conv + cw_ref[k:k + 1, :] * u_ref[HALO_ROWS - shift:HALO_ROWS - shift + tm, :]
    yc_in = (cb * conv) * (z_c * _sigmoid(z_c))
    y_conv = jnp.dot(yc_in.astype(BF16), wc_ref[...], preferred_element_type=F32)

    g_a = _sigmoid(gate_ref[:, 0:D_MODEL].astype(F32) + bm_ref[:, 0:D_MODEL])
    g_c = _sigmoid(gate_ref[:, D_MODEL:2 * D_MODEL].astype(F32) + bm_ref[:, D_MODEL:2 * D_MODEL])
    y = g_a * y_attn + g_c * y_conv
    out = jnp.dot(y.astype(BF16), wo_ref[...], preferred_element_type=F32)
    ms = jnp.mean(out * out, axis=-1, keepdims=True)
    out_ref[...] = x_ref[...] + out * lax.rsqrt(ms + NORM_EPS) * gp_ref[...]


def _out_proj(x2d, o, za, conv, gate, conv_w, wa, wc, b_merge, wo, g_post, *, seq):
    n = x2d.shape[0]
    tiles_per_seq = seq // OUT_ROWS
    halo_blocks = OUT_ROWS // HALO_ROWS
    row_block = lambda cols: pl.BlockSpec((OUT_ROWS, cols), lambda i: (i, 0))
    whole = lambda shape: pl.BlockSpec(shape, lambda i: (0, 0))
    return pl.pallas_call(
        functools.partial(_out_proj_kernel, tiles_per_seq=tiles_per_seq),
        grid=(n // OUT_ROWS,),
        in_specs=[
            row_block(D_MODEL), row_block(ATTN_WIDTH), row_block(ATTN_WIDTH), row_block(CONV_COLS),
            pl.BlockSpec((HALO_ROWS, CONV_COLS),
                         lambda i: (jnp.maximum(i * halo_blocks - 1, 0), 0)),
            row_block(GATE_COLS),
            whole((CONV_K, CONV_WIDTH)), whole((ATTN_WIDTH, D_MODEL)),
            whole((CONV_WIDTH, D_MODEL)), whole((1, 2 * D_MODEL)), whole((D_MODEL, D_MODEL)),
            whole((1, D_MODEL)),
        ],
        out_specs=row_block(D_MODEL),
        out_shape=jax.ShapeDtypeStruct((n, D_MODEL), F32),
        scratch_shapes=[pltpu.VMEM((HALO_ROWS + OUT_ROWS, CONV_WIDTH), F32)],
        compiler_params=pltpu.CompilerParams(
            dimension_semantics=("arbitrary",), vmem_limit_bytes=VMEM_LIMIT_BYTES),
        name="out_proj",
    )(x2d, o, za, conv, conv, gate, conv_w, wa, wc, b_merge, wo, g_post)


def _layer(x, layer_idx, w_in, lq1, lk1, lq2, lk2, subln_gain, conv_w, w_attn_o, w_conv_o,
           b_merge, w_out, g_pre, g_post):
    batch, seq, d = x.shape
    x2d = x.reshape(batch * seq, d)
    row = lambda a: a.reshape(1, -1).astype(F32)
    slopes = jnp.asarray([2.0 ** (-8.0 * (i + 1) / ATTN_HEADS) for i in range(ATTN_HEADS)], F32)

    qkv, za, conv, gate = _in_proj(x2d, row(g_pre), w_in.astype(BF16))
    o = _diff_attn(qkv, slopes, row(lq1), row(lk1), row(lq2), row(lk2), row(subln_gain),
                   batch=batch, seq=seq, lam_init=_lambda_init(layer_idx))
    out = _out_proj(x2d, o, za, conv, gate, conv_w.astype(F32), w_attn_o.astype(BF16),
                    w_conv_o.astype(BF16), row(b_merge), w_out.astype(BF16), row(g_post), seq=seq)
    return out.reshape(batch, seq, d)


def kernel(x, w_in, lambda_q1, lambda_k1, lambda_q2, lambda_k2, subln_gain, conv_w, w_attn_o,
           w_conv_o, b_merge, w_out, g_pre, g_post):
    for l in range(w_in.shape[0]):
        x = _layer(x, l, w_in[l], lambda_q1[l], lambda_k1[l], lambda_q2[l], lambda_k2[l],
                   subln_gain[l], conv_w[l], w_attn_o[l], w_conv_o[l], b_merge[l], w_out[l],
                   g_pre[l], g_post[l])
    return x
```

```python
import functools
import math

import jax
import jax.numpy as jnp
from jax import lax
from jax.experimental import pallas as pl
from jax.experimental.pallas import tpu as pltpu

D_MODEL = 1024
ATTN_HEADS = 4
HEAD_DIM = 64
HEAD_COLS = 2 * HEAD_DIM
ATTN_WIDTH = ATTN_HEADS * HEAD_COLS
CONV_WIDTH = D_MODEL // 2
CONV_K = 3
NORM_EPS = 1e-6
QKV_COLS = 3 * ATTN_WIDTH
CONV_COLS = 4 * CONV_WIDTH
GATE_COLS = 2 * D_MODEL
IN_COLS = QKV_COLS + ATTN_WIDTH + CONV_COLS + GATE_COLS
MASK_VALUE = -1e30
LOG2_E = math.log2(math.e)
Q_SCALE_LOG2 = HEAD_DIM ** -0.5 * LOG2_E

PROJ_ROWS = 512
PROJ_COL_CHUNK = 512
ATTN_Q_ROWS = 1024
ATTN_K_ROWS = 1024
TK_SHIFT = ATTN_K_ROWS.bit_length() - 1
ATTN_SUB_ROWS = 256
ATTN_LOOKAHEAD = 2
assert ATTN_K_ROWS == 1 << TK_SHIFT and ATTN_K_ROWS % ATTN_Q_ROWS == 0
OUT_ROWS = 512
HALO_ROWS = 16
VMEM_LIMIT_BYTES = 56 * 1024 * 1024

BF16 = jnp.bfloat16
F32 = jnp.float32


def _lambda_init(layer_idx):
    return 0.8 - 0.6 * math.exp(-0.3 * layer_idx)


def _rms_norm_bf16(x, gain):
    ms = jnp.mean(x * x, axis=-1, keepdims=True)
    return (x * lax.rsqrt(ms + NORM_EPS) * gain).astype(BF16)


def _qkv_proj_kernel(x_ref, g_ref, w_ref, qkv_ref, h_ref):
    h_ref[...] = _rms_norm_bf16(x_ref[...], g_ref[...])
    for c in range(0, QKV_COLS, PROJ_COL_CHUNK):
        acc = jnp.dot(h_ref[...], w_ref[:, c:c + PROJ_COL_CHUNK], preferred_element_type=F32)
        if c < ATTN_WIDTH:
            acc = acc * Q_SCALE_LOG2
        qkv_ref[:, c:c + PROJ_COL_CHUNK] = acc.astype(BF16)


def _qkv_proj(x2d, g_pre, w_in_bf16):
    n = x2d.shape[0]
    return pl.pallas_call(
        _qkv_proj_kernel,
        grid=(n // PROJ_ROWS,),
        in_specs=[
            pl.BlockSpec((PROJ_ROWS, D_MODEL), lambda i: (i, 0)),
            pl.BlockSpec((1, D_MODEL), lambda i: (0, 0)),
            pl.BlockSpec((D_MODEL, QKV_COLS), lambda i: (0, 0), pipeline_mode=pl.Buffered(1)),
        ],
        out_specs=pl.BlockSpec((PROJ_ROWS, QKV_COLS), lambda i: (i, 0)),
        out_shape=jax.ShapeDtypeStruct((n, QKV_COLS), BF16),
        scratch_shapes=[pltpu.VMEM((PROJ_ROWS, D_MODEL), BF16)],
        compiler_params=pltpu.CompilerParams(
            dimension_semantics=("arbitrary",), vmem_limit_bytes=VMEM_LIMIT_BYTES),
        name="qkv_proj",
    )(x2d, g_pre, w_in_bf16)


def _diff_attn_kernel(slopes_ref, lq1_ref, lk1_ref, lq2_ref, lk2_ref, gain_ref,
                      q_ref, k_ref, v_ref, o_ref, ka_ref, kb_ref, va_ref, m_ref, acc_ref,
                      *, lam_init):
    tq, tk, ts = ATTN_Q_ROWS, ATTN_K_ROWS, ATTN_SUB_ROWS
    seq = k_ref.shape[0]
    head = pl.program_id(1)
    qi = pl.program_id(2)
    slope = slopes_ref[head] * LOG2_E

    @pl.when(qi == 0)
    def _build_keys_values():
        lane = lax.broadcasted_iota(jnp.int32, (tk, HEAD_COLS), 1)
        b = lax.broadcasted_iota(jnp.int32, (tk, HEAD_COLS), 0).astype(F32) * slope
        b1 = b.astype(BF16).astype(F32)
        b2 = (b - b1).astype(BF16).astype(F32)
        b3 = (b - b1 - b2).astype(BF16).astype(F32)

        def bias_lanes(first):
            return jnp.where(lane == first, b1, jnp.where(
                lane == first + 1, b2, jnp.where(lane == first + 2, b3, 0.0)))

        bias_a = bias_lanes(HEAD_DIM)
        bias_b = bias_lanes(0)
        for c in range(0, seq, tk):
            k = k_ref[c:c + tk, :].astype(F32)
            ka_ref[c:c + tk, :] = jnp.where(lane < HEAD_DIM, k, bias_a).astype(BF16)
            kb_ref[c:c + tk, :] = jnp.where(lane >= HEAD_DIM, k, bias_b).astype(BF16)
            va_ref[c:c + tk, 0:HEAD_COLS] = v_ref[c:c + tk, :]
            va_ref[c:c + tk, HEAD_COLS:2 * HEAD_COLS] = jnp.ones((tk, HEAD_COLS), BF16)

    q = q_ref[...].astype(F32)
    qlane = lax.broadcasted_iota(jnp.int32, q.shape, 1)
    q_maps = (
        jnp.where(qlane < HEAD_DIM, q, jnp.where(qlane < HEAD_DIM + 3, 1.0, 0.0)).astype(BF16),
        jnp.where(qlane >= HEAD_DIM, q, jnp.where(qlane < 3, 1.0, 0.0)).astype(BF16),
    )
    key_refs = (ka_ref, kb_ref)

    tri_keep = (lax.broadcasted_iota(jnp.int32, (ts, ts), 1)
                <= lax.broadcasted_iota(jnp.int32, (ts, ts), 0))

    def step(key_start, diag_cols):
        span_bias = (key_start - qi * tq).astype(F32) * slope
        first = diag_cols is not None
        streams = [(r, mp) for r in (range(tq - ts, -ts, -ts) if first else range(0, tq, ts))
                   for mp in range(2)]

        def width(r):
            return tk if diag_cols is None else diag_cols + r + ts

        def scores(r, mp):
            keys = key_refs[mp][pl.ds(key_start, width(r)), :]
            s = lax.dot_general(q_maps[mp][r:r + ts], keys, (((1,), (1,)), ((), ())),
                                preferred_element_type=F32)
            if diag_cols is not None:
                below = width(r) - ts
                blocks = [s[:, :below]] if below else []
                s = jnp.concatenate(blocks + [jnp.where(tri_keep, s[:, below:], MASK_VALUE)],
                                    axis=1)
            return s

        def update(r, mp, s):
            m_cur = jnp.max(s, axis=-1, keepdims=True) + span_bias
            if first:
                m_new = jnp.broadcast_to(m_cur, (ts, HEAD_COLS))
            else:
                m_prev = m_ref[mp, r:r + ts]
                m_new = jnp.maximum(m_prev, m_cur)
            shift = m_new - span_bias
            p = jnp.exp2(s - jnp.concatenate([shift] * (width(r) // HEAD_COLS), axis=1))
            pv = jnp.dot(p.astype(BF16), va_ref[pl.ds(key_start, width(r)), :],
                         preferred_element_type=F32)
            if not first:
                alpha = jnp.exp2(m_prev - m_new)
                pv = jnp.concatenate([alpha, alpha], axis=1) * acc_ref[mp, r:r + ts] + pv
            acc_ref[mp, r:r + ts] = pv
            m_ref[mp, r:r + ts] = m_new

        pending = [scores(*st) for st in streams[:ATTN_LOOKAHEAD]]
        for i, st in enumerate(streams):
            if i + ATTN_LOOKAHEAD < len(streams):
                pending.append(scores(*streams[i + ATTN_LOOKAHEAD]))
            update(*st, pending.pop(0))

    n_spans = lax.shift_right_logical(qi * tq, TK_SHIFT)
    tail_start = pl.multiple_of(n_spans * tk, tk)
    for extra in range(0, tk, tq):
        @pl.when(qi * tq - tail_start == extra)
        def _diag_step(extra=extra):
            step(tail_start, extra)

    def full_span(kj, carry):
        step(pl.multiple_of(kj * tk, tk), None)
        return carry

    lax.fori_loop(0, n_spans, full_span, 0)

    lam = (jnp.exp(jnp.sum(lq1_ref[...] * lk1_ref[...], axis=-1, keepdims=True))
           - jnp.exp(jnp.sum(lq2_ref[...] * lk2_ref[...], axis=-1, keepdims=True))
           + lam_init)
    o = (acc_ref[0, :, 0:HEAD_COLS] / acc_ref[0, :, HEAD_COLS:2 * HEAD_COLS]
         - lam * (acc_ref[1, :, 0:HEAD_COLS] / acc_ref[1, :, HEAD_COLS:2 * HEAD_COLS]))
    ms = jnp.mean(o * o, axis=-1, keepdims=True)
    o = (o * lax.rsqrt(ms + NORM_EPS) * gain_ref[...]) * (1.0 - lam_init)
    o_ref[...] = o.astype(BF16)


def _diff_attn(qkv, slopes, lq1, lk1, lq2, lk2, gain, *, batch, seq, lam_init):
    nq = seq // ATTN_Q_ROWS
    small = lambda cols: pl.BlockSpec((1, cols), lambda b, h, i: (0, 0))
    return pl.pallas_call(
        functools.partial(_diff_attn_kernel, lam_init=lam_init),
        grid=(batch, ATTN_HEADS, nq),
        in_specs=[
            pl.BlockSpec(memory_space=pltpu.SMEM),
            small(HEAD_DIM), small(HEAD_DIM), small(HEAD_DIM), small(HEAD_DIM),
            small(HEAD_COLS),
            pl.BlockSpec((ATTN_Q_ROWS, HEAD_COLS), lambda b, h, i: (b * nq + i, h)),
            pl.BlockSpec((seq, HEAD_COLS), lambda b, h, i: (b, ATTN_HEADS + h)),
            pl.BlockSpec((seq, HEAD_COLS), lambda b, h, i: (b, 2 * ATTN_HEADS + h)),
        ],
        out_specs=pl.BlockSpec((ATTN_Q_ROWS, HEAD_COLS), lambda b, h, i: (b * nq + i, h)),
        out_shape=jax.ShapeDtypeStruct((batch * seq, ATTN_WIDTH), BF16),
        scratch_shapes=[pltpu.VMEM((seq, HEAD_COLS), BF16),
                        pltpu.VMEM((seq, HEAD_COLS), BF16),
                        pltpu.VMEM((seq, 2 * HEAD_COLS), BF16),
                        pltpu.VMEM((2, ATTN_Q_ROWS, HEAD_COLS), F32),
                        pltpu.VMEM((2, ATTN_Q_ROWS, 2 * HEAD_COLS), F32)],
        compiler_params=pltpu.CompilerParams(
            dimension_semantics=("arbitrary", "arbitrary", "arbitrary"),
            vmem_limit_bytes=VMEM_LIMIT_BYTES),
        name="diff_attn",
    )(slopes, lq1, lk1, lq2, lk2, gain, qkv, qkv, qkv)


def _sigmoid(z):
    return 1.0 / (1.0 + jnp.exp(-z))


def _out_proj_kernel(x_ref, o_ref, gpre_ref, wmid_ref, whi_ref, cw_ref, wa_ref, wc_ref, bm_ref,
                     wo_ref, gp_ref, out_ref, h_ref, u_ref, *, tiles_per_seq):
    tm, cwid = OUT_ROWS, CONV_WIDTH
    x = x_ref[...]
    h_ref[...] = _rms_norm_bf16(x, gpre_ref[...])

    def proj(w_ref, col, cols):
        return jnp.dot(h_ref[...], w_ref[:, col:col + cols], preferred_element_type=F32)

    z_a = proj(wmid_ref, 0, ATTN_WIDTH)
    ya_in = o_ref[...].astype(F32) * (z_a * _sigmoid(z_a))
    y_attn = jnp.dot(ya_in.astype(BF16), wa_ref[...], preferred_element_type=F32)

    @pl.when(pl.program_id(0) % tiles_per_seq == 0)
    def _sequence_start():
        u_ref[0:HALO_ROWS, :] = jnp.zeros((HALO_ROWS, cwid), F32)

    u = proj(wmid_ref, ATTN_WIDTH + cwid, cwid) * proj(whi_ref, 0, cwid)
    u_ref[HALO_ROWS:HALO_ROWS + tm, :] = u
    conv = cw_ref[CONV_K - 1:CONV_K, :] * u
    for k in range(CONV_K - 1):
        shift = CONV_K - 1 - k
        conv = conv + cw_ref[k:k + 1, :] * u_ref[HALO_ROWS - shift:HALO_ROWS - shift + tm, :]
    u_ref[0:HALO_ROWS, :] = u_ref[tm:tm + HALO_ROWS, :]
    z_c = proj(whi_ref, cwid, cwid)
    yc_in = (proj(wmid_ref, ATTN_WIDTH, cwid) * conv) * (z_c * _sigmoid(z_c))
    y_conv = jnp.dot(yc_in.astype(BF16), wc_ref[...], preferred_element_type=F32)

    g_a = _sigmoid(proj(whi_ref, 2 * cwid, D_MODEL) + bm_ref[:, 0:D_MODEL])
    g_c = _sigmoid(proj(whi_ref, 2 * cwid + D_MODEL, D_MODEL) + bm_ref[:, D_MODEL:2 * D_MODEL])
    y = g_a * y_attn + g_c * y_conv
    out = jnp.dot(y.astype(BF16), wo_ref[...], preferred_element_type=F32)
    ms = jnp.mean(out * out, axis=-1, keepdims=True)
    out_ref[...] = x + out * lax.rsqrt(ms + NORM_EPS) * gp_ref[...]


def _out_proj(x2d, o, g_pre, w_in_bf16, conv_w, wa, wc, b_merge, wo, g_post, *, seq):
    n = x2d.shape[0]
    mid_cols = ATTN_WIDTH + 2 * CONV_WIDTH
    hi_cols = 2 * CONV_WIDTH + GATE_COLS
    assert QKV_COLS == mid_cols and QKV_COLS + mid_cols == hi_cols == IN_COLS - hi_cols
    row_block = lambda cols: pl.BlockSpec((OUT_ROWS, cols), lambda i: (i, 0))
    const = lambda shape, col_block=0: pl.BlockSpec(shape, lambda i: (0, col_block),
                                                   pipeline_mode=pl.Buffered(1))
    return pl.pallas_call(
        functools.partial(_out_proj_kernel, tiles_per_seq=seq // OUT_ROWS),
        grid=(n // OUT_ROWS,),
        in_specs=[
            row_block(D_MODEL), row_block(ATTN_WIDTH), const((1, D_MODEL)),
            const((D_MODEL, mid_cols), 1), const((D_MODEL, hi_cols), 1),
            const((CONV_K, CONV_WIDTH)), const((ATTN_WIDTH, D_MODEL)),
            const((CONV_WIDTH, D_MODEL)), const((1, 2 * D_MODEL)), const((D_MODEL, D_MODEL)),
            const((1, D_MODEL)),
        ],
        out_specs=row_block(D_MODEL),
        out_shape=jax.ShapeDtypeStruct((n, D_MODEL), F32),
        scratch_shapes=[pltpu.VMEM((OUT_ROWS, D_MODEL), BF16),
                        pltpu.VMEM((HALO_ROWS + OUT_ROWS, CONV_WIDTH), F32)],
        compiler_params=pltpu.CompilerParams(
            dimension_semantics=("arbitrary",), vmem_limit_bytes=VMEM_LIMIT_BYTES),
        name="out_proj",
    )(x2d, o, g_pre, w_in_bf16, w_in_bf16, conv_w, wa, wc, b_merge, wo, g_post)


def _layer(x, layer_idx, w_in, lq1, lk1, lq2, lk2, subln_gain, conv_w, w_attn_o, w_conv_o,
           b_merge, w_out, g_pre, g_post):
    batch, seq, d = x.shape
    x2d = x.reshape(batch * seq, d)
    row = lambda a: a.reshape(1, -1).astype(F32)
    slopes = jnp.asarray([2.0 ** (-8.0 * (i + 1) / ATTN_HEADS) for i in range(ATTN_HEADS)], F32)

    w_in_bf16 = w_in.astype(BF16)
    qkv = _qkv_proj(x2d, row(g_pre), w_in_bf16)
    o = _diff_attn(qkv, slopes, row(lq1), row(lk1), row(lq2), row(lk2), row(subln_gain),
                   batch=batch, seq=seq, lam_init=_lambda_init(layer_idx))
    out = _out_proj(x2d, o, row(g_pre), w_in_bf16, conv_w.astype(F32), w_attn_o.astype(BF16),
                    w_conv_o.astype(BF16), row(b_merge), w_out.astype(BF16), row(g_post), seq=seq)
    return out.reshape(batch, seq, d)


def kernel(x, w_in, lambda_q1, lambda_k1, lambda_q2, lambda_k2, subln_gain, conv_w, w_attn_o,
           w_conv_o, b_merge, w_out, g_pre, g_post):
    for l in range(w_in.shape[0]):
        x = _layer(x, l, w_in[l], lambda_q1[l], lambda_k1[l], lambda_q2[l], lambda_k2[l],
                   subln_gain[l], conv_w[l], w_attn_o[l], w_conv_o[l], b_merge[l], w_out[l],
                   g_pre[l], g_post[l])
    return x
```

```python
import functools
import math

import jax
import jax.numpy as jnp
from jax import lax
from jax.experimental import pallas as pl
from jax.experimental.pallas import tpu as pltpu

D_MODEL = 1024
ATTN_HEADS = 4
HEAD_DIM = 64
HEAD_COLS = 2 * HEAD_DIM
ATTN_WIDTH = ATTN_HEADS * HEAD_COLS
CONV_WIDTH = D_MODEL // 2
CONV_K = 3
NORM_EPS = 1e-6
QKV_COLS = 3 * ATTN_WIDTH
CONV_COLS = 4 * CONV_WIDTH
GATE_COLS = 2 * D_MODEL
IN_COLS = QKV_COLS + ATTN_WIDTH + CONV_COLS + GATE_COLS
MASK_VALUE = -1e30
LOG2_E = math.log2(math.e)
Q_SCALE_LOG2 = HEAD_DIM ** -0.5 * LOG2_E

PROJ_ROWS = 1024
PROJ_SUB_ROWS = 512
PROJ_COL_CHUNK = 512
ATTN_Q_ROWS = 1024
ATTN_K_ROWS = 1024
TK_SHIFT = ATTN_K_ROWS.bit_length() - 1
ATTN_SUB_ROWS = 256
ATTN_LOOKAHEAD = 2
assert ATTN_K_ROWS == 1 << TK_SHIFT and ATTN_K_ROWS % ATTN_Q_ROWS == 0
OUT_ROWS = 1024
OUT_SUB_ROWS = 512
HALO_ROWS = 16
VMEM_LIMIT_BYTES = 56 * 1024 * 1024

BF16 = jnp.bfloat16
F32 = jnp.float32


def _lambda_init(layer_idx):
    return 0.8 - 0.6 * math.exp(-0.3 * layer_idx)


def _rms_norm_bf16(x, gain):
    ms = jnp.mean(x * x, axis=-1, keepdims=True)
    return (x * lax.rsqrt(ms + NORM_EPS) * gain).astype(BF16)


def _qkv_proj_kernel(x_ref, g_ref, w_ref, qkv_ref, h_ref):
    for r in range(0, PROJ_ROWS, PROJ_SUB_ROWS):
        rows = slice(r, r + PROJ_SUB_ROWS)
        h_ref[rows, :] = _rms_norm_bf16(x_ref[rows, :], g_ref[...])
        for c in range(0, QKV_COLS, PROJ_COL_CHUNK):
            acc = jnp.dot(h_ref[rows, :], w_ref[:, c:c + PROJ_COL_CHUNK],
                          preferred_element_type=F32)
            if c < ATTN_WIDTH:
                acc = acc * Q_SCALE_LOG2
            qkv_ref[rows, c:c + PROJ_COL_CHUNK] = acc.astype(BF16)


def _qkv_proj(x2d, g_pre, w_in_bf16):
    n = x2d.shape[0]
    return pl.pallas_call(
        _qkv_proj_kernel,
        grid=(n // PROJ_ROWS,),
        in_specs=[
            pl.BlockSpec((PROJ_ROWS, D_MODEL), lambda i: (i, 0)),
            pl.BlockSpec((1, D_MODEL), lambda i: (0, 0)),
            pl.BlockSpec((D_MODEL, QKV_COLS), lambda i: (0, 0), pipeline_mode=pl.Buffered(1)),
        ],
        out_specs=pl.BlockSpec((PROJ_ROWS, QKV_COLS), lambda i: (i, 0)),
        out_shape=jax.ShapeDtypeStruct((n, QKV_COLS), BF16),
        scratch_shapes=[pltpu.VMEM((PROJ_ROWS, D_MODEL), BF16)],
        compiler_params=pltpu.CompilerParams(
            dimension_semantics=("arbitrary",), vmem_limit_bytes=VMEM_LIMIT_BYTES),
        name="qkv_proj",
    )(x2d, g_pre, w_in_bf16)


def _diff_attn_kernel(slopes_ref, lq1_ref, lk1_ref, lq2_ref, lk2_ref, gain_ref,
                      q_ref, k_ref, v_ref, o_ref, ka_ref, kb_ref, va_ref, m_ref, acc_ref,
                      *, lam_init):
    tq, tk, ts = ATTN_Q_ROWS, ATTN_K_ROWS, ATTN_SUB_ROWS
    seq = k_ref.shape[0]
    head = pl.program_id(1)
    qi = pl.program_id(2)
    slope = slopes_ref[head] * LOG2_E

    @pl.when(qi == 0)
    def _build_keys_values():
        lane = lax.broadcasted_iota(jnp.int32, (tk, HEAD_COLS), 1)
        b = lax.broadcasted_iota(jnp.int32, (tk, HEAD_COLS), 0).astype(F32) * slope
        b1 = b.astype(BF16).astype(F32)
        b2 = (b - b1).astype(BF16).astype(F32)
        b3 = (b - b1 - b2).astype(BF16).astype(F32)

        def bias_lanes(first):
            return jnp.where(lane == first, b1, jnp.where(
                lane == first + 1, b2, jnp.where(lane == first + 2, b3, 0.0)))

        bias_a = bias_lanes(HEAD_DIM)
        bias_b = bias_lanes(0)
        for c in range(0, seq, tk):
            k = k_ref[c:c + tk, :].astype(F32)
            ka_ref[c:c + tk, :] = jnp.where(lane < HEAD_DIM, k, bias_a).astype(BF16)
            kb_ref[c:c + tk, :] = jnp.where(lane >= HEAD_DIM, k, bias_b).astype(BF16)
            va_ref[c:c + tk, 0:HEAD_COLS] = v_ref[c:c + tk, :]
            va_ref[c:c + tk, HEAD_COLS:2 * HEAD_COLS] = jnp.ones((tk, HEAD_COLS), BF16)

    q = q_ref[...].astype(F32)
    qlane = lax.broadcasted_iota(jnp.int32, q.shape, 1)
    q_maps = (
        jnp.where(qlane < HEAD_DIM, q, jnp.where(qlane < HEAD_DIM + 3, 1.0, 0.0)).astype(BF16),
        jnp.where(qlane >= HEAD_DIM, q, jnp.where(qlane < 3, 1.0, 0.0)).astype(BF16),
    )
    key_refs = (ka_ref, kb_ref)

    tri_keep = (lax.broadcasted_iota(jnp.int32, (ts, ts), 1)
                <= lax.broadcasted_iota(jnp.int32, (ts, ts), 0))

    def step(key_start, diag_cols):
        span_bias = (key_start - qi * tq).astype(F32) * slope
        first = diag_cols is not None
        streams = [(r, mp) for r in (range(tq - ts, -ts, -ts) if first else range(0, tq, ts))
                   for mp in range(2)]

        def width(r):
            return tk if diag_cols is None else diag_cols + r + ts

        def scores(r, mp):
            keys = key_refs[mp][pl.ds(key_start, width(r)), :]
            s = lax.dot_general(q_maps[mp][r:r + ts], keys, (((1,), (1,)), ((), ())),
                                preferred_element_type=F32)
            if diag_cols is not None:
                below = width(r) - ts
                blocks = [s[:, :below]] if below else []
                s = jnp.concatenate(blocks + [jnp.where(tri_keep, s[:, below:], MASK_VALUE)],
                                    axis=1)
            return s

        def update(r, mp, s):
            m_cur = jnp.max(s, axis=-1, keepdims=True) + span_bias
            if first:
                m_new = jnp.broadcast_to(m_cur, (ts, HEAD_COLS))
            else:
                m_prev = m_ref[mp, r:r + ts]
                m_new = jnp.maximum(m_prev, m_cur)
            shift = m_new - span_bias
            p = jnp.exp2(s - jnp.concatenate([shift] * (width(r) // HEAD_COLS), axis=1))
            pv = jnp.dot(p.astype(BF16), va_ref[pl.ds(key_start, width(r)), :],
                         preferred_element_type=F32)
            if not first:
                alpha = jnp.exp2(m_prev - m_new)
                pv = jnp.concatenate([alpha, alpha], axis=1) * acc_ref[mp, r:r + ts] + pv
            acc_ref[mp, r:r + ts] = pv
            m_ref[mp, r:r + ts] = m_new

        pending = [scores(*st) for st in streams[:ATTN_LOOKAHEAD]]
        for i, st in enumerate(streams):
            if i + ATTN_LOOKAHEAD < len(streams):
                pending.append(scores(*streams[i + ATTN_LOOKAHEAD]))
            update(*st, pending.pop(0))

    n_spans = lax.shift_right_logical(qi * tq, TK_SHIFT)
    tail_start = pl.multiple_of(n_spans * tk, tk)
    for extra in range(0, tk, tq):
        @pl.when(qi * tq - tail_start == extra)
        def _diag_step(extra=extra):
            step(tail_start, extra)

    def full_span(kj, carry):
        step(pl.multiple_of(kj * tk, tk), None)
        return carry

    lax.fori_loop(0, n_spans, full_span, 0)

    lam = (jnp.exp(jnp.sum(lq1_ref[...] * lk1_ref[...], axis=-1, keepdims=True))
           - jnp.exp(jnp.sum(lq2_ref[...] * lk2_ref[...], axis=-1, keepdims=True))
           + lam_init)
    o = (acc_ref[0, :, 0:HEAD_COLS] / acc_ref[0, :, HEAD_COLS:2 * HEAD_COLS]
         - lam * (acc_ref[1, :, 0:HEAD_COLS] / acc_ref[1, :, HEAD_COLS:2 * HEAD_COLS]))
    ms = jnp.mean(o * o, axis=-1, keepdims=True)
    o = (o * lax.rsqrt(ms + NORM_EPS) * gain_ref[...]) * (1.0 - lam_init)
    o_ref[...] = o.astype(BF16)


def _diff_attn(qkv, slopes, lq1, lk1, lq2, lk2, gain, *, batch, seq, lam_init):
    nq = seq // ATTN_Q_ROWS
    small = lambda cols: pl.BlockSpec((1, cols), lambda b, h, i: (0, 0))
    return pl.pallas_call(
        functools.partial(_diff_attn_kernel, lam_init=lam_init),
        grid=(batch, ATTN_HEADS, nq),
        in_specs=[
            pl.BlockSpec(memory_space=pltpu.SMEM),
            small(HEAD_DIM), small(HEAD_DIM), small(HEAD_DIM), small(HEAD_DIM),
            small(HEAD_COLS),
            pl.BlockSpec((ATTN_Q_ROWS, HEAD_COLS), lambda b, h, i: (b * nq + i, h)),
            pl.BlockSpec((seq, HEAD_COLS), lambda b, h, i: (b, ATTN_HEADS + h)),
            pl.BlockSpec((seq, HEAD_COLS), lambda b, h, i: (b, 2 * ATTN_HEADS + h)),
        ],
        out_specs=pl.BlockSpec((ATTN_Q_ROWS, HEAD_COLS), lambda b, h, i: (b * nq + i, h)),
        out_shape=jax.ShapeDtypeStruct((batch * seq, ATTN_WIDTH), BF16),
        scratch_shapes=[pltpu.VMEM((seq, HEAD_COLS), BF16),
                        pltpu.VMEM((seq, HEAD_COLS), BF16),
                        pltpu.VMEM((seq, 2 * HEAD_COLS), BF16),
                        pltpu.VMEM((2, ATTN_Q_ROWS, HEAD_COLS), F32),
                        pltpu.VMEM((2, ATTN_Q_ROWS, 2 * HEAD_COLS), F32)],
        compiler_params=pltpu.CompilerParams(
            dimension_semantics=("arbitrary", "arbitrary", "arbitrary"),
            vmem_limit_bytes=VMEM_LIMIT_BYTES),
        name="diff_attn",
    )(slopes, lq1, lk1, lq2, lk2, gain, qkv, qkv, qkv)


def _sigmoid(z):
    return 1.0 / (1.0 + jnp.exp(-z))


def _out_proj_kernel(x_ref, o_ref, gpre_ref, wmid_ref, whi_ref, cw_ref, wa_ref, wc_ref, bm_ref,
                     wo_ref, gp_ref, out_ref, h_ref, u_ref, *, tiles_per_seq):
    tm, ts, cwid = OUT_ROWS, OUT_SUB_ROWS, CONV_WIDTH

    @pl.when(pl.program_id(0) % tiles_per_seq == 0)
    def _sequence_start():
        u_ref[0:HALO_ROWS, :] = jnp.zeros((HALO_ROWS, cwid), F32)

    for r in range(0, tm, ts):
        x = x_ref[r:r + ts, :]
        h_ref[r:r + ts, :] = _rms_norm_bf16(x, gpre_ref[...])

        def proj(w_ref, col, cols, r=r):
            return jnp.dot(h_ref[r:r + ts, :], w_ref[:, col:col + cols],
                           preferred_element_type=F32)

        z_a = proj(wmid_ref, 0, ATTN_WIDTH)
        ya_in = o_ref[r:r + ts, :].astype(F32) * (z_a * _sigmoid(z_a))
        y_attn = jnp.dot(ya_in.astype(BF16), wa_ref[...], preferred_element_type=F32)

        u = proj(wmid_ref, ATTN_WIDTH + cwid, cwid) * proj(whi_ref, 0, cwid)
        u_ref[HALO_ROWS + r:HALO_ROWS + r + ts, :] = u
        conv = cw_ref[CONV_K - 1:CONV_K, :] * u
        for k in range(CONV_K - 1):
            first_row = HALO_ROWS + r - (CONV_K - 1 - k)
            conv = conv + cw_ref[k:k + 1, :] * u_ref[first_row:first_row + ts, :]
        z_c = proj(whi_ref, cwid, cwid)
        yc_in = (proj(wmid_ref, ATTN_WIDTH, cwid) * conv) * (z_c * _sigmoid(z_c))
        y_conv = jnp.dot(yc_in.astype(BF16), wc_ref[...], preferred_element_type=F32)

        g_a = _sigmoid(proj(whi_ref, 2 * cwid, D_MODEL) + bm_ref[:, 0:D_MODEL])
        g_c = _sigmoid(proj(whi_ref, 2 * cwid + D_MODEL, D_MODEL)
                       + bm_ref[:, D_MODEL:2 * D_MODEL])
        y = g_a * y_attn + g_c * y_conv
        out = jnp.dot(y.astype(BF16), wo_ref[...], preferred_element_type=F32)
        ms = jnp.mean(out * out, axis=-1, keepdims=True)
        out_ref[r:r + ts, :] = x + out * lax.rsqrt(ms + NORM_EPS) * gp_ref[...]

    u_ref[0:HALO_ROWS, :] = u_ref[tm:tm + HALO_ROWS, :]


def _out_proj(x2d, o, g_pre, w_in_bf16, conv_w, wa, wc, b_merge, wo, g_post, *, seq):
    n = x2d.shape[0]
    mid_cols = ATTN_WIDTH + 2 * CONV_WIDTH
    hi_cols = 2 * CONV_WIDTH + GATE_COLS
    assert QKV_COLS == mid_cols and QKV_COLS + mid_cols == hi_cols == IN_COLS - hi_cols
    row_block = lambda cols: pl.BlockSpec((OUT_ROWS, cols), lambda i: (i, 0))
    const = lambda shape, col_block=0: pl.BlockSpec(shape, lambda i: (0, col_block),
                                                   pipeline_mode=pl.Buffered(1))
    return pl.pallas_call(
        functools.partial(_out_proj_kernel, tiles_per_seq=seq // OUT_ROWS),
        grid=(n // OUT_ROWS,),
        in_specs=[
            row_block(D_MODEL), row_block(ATTN_WIDTH), const((1, D_MODEL)),
            const((D_MODEL, mid_cols), 1), const((D_MODEL, hi_cols), 1),
            const((CONV_K, CONV_WIDTH)), const((ATTN_WIDTH, D_MODEL)),
            const((CONV_WIDTH, D_MODEL)), const((1, 2 * D_MODEL)), const((D_MODEL, D_MODEL)),
            const((1, D_MODEL)),
        ],
        out_specs=row_block(D_MODEL),
        out_shape=jax.ShapeDtypeStruct((n, D_MODEL), F32),
        scratch_shapes=[pltpu.VMEM((OUT_ROWS, D_MODEL), BF16),
                        pltpu.VMEM((HALO_ROWS + OUT_ROWS, CONV_WIDTH), F32)],
        compiler_params=pltpu.CompilerParams(
            dimension_semantics=("arbitrary",), vmem_limit_bytes=VMEM_LIMIT_BYTES),
        name="out_proj",
    )(x2d, o, g_pre, w_in_bf16, w_in_bf16, conv_w, wa, wc, b_merge, wo, g_post)


def _layer(x, layer_idx, w_in, lq1, lk1, lq2, lk2, subln_gain, conv_w, w_attn_o, w_conv_o,
           b_merge, w_out, g_pre, g_post):
    batch, seq, d = x.shape
    x2d = x.reshape(batch * seq, d)
    row = lambda a: a.reshape(1, -1).astype(F32)
    slopes = jnp.asarray([2.0 ** (-8.0 * (i + 1) / ATTN_HEADS) for i in range(ATTN_HEADS)], F32)

    w_in_bf16 = w_in.astype(BF16)
    qkv = _qkv_proj(x2d, row(g_pre), w_in_bf16)
    o = _diff_attn(qkv, slopes, row(lq1), row(lk1), row(lq2), row(lk2), row(subln_gain),
                   batch=batch, seq=seq, lam_init=_lambda_init(layer_idx))
    out = _out_proj(x2d, o, row(g_pre), w_in_bf16, conv_w.astype(F32), w_attn_o.astype(BF16),
                    w_conv_o.astype(BF16), row(b_merge), w_out.astype(BF16), row(g_post), seq=seq)
    return out.reshape(batch, seq, d)


def kernel(x, w_in, lambda_q1, lambda_k1, lambda_q2, lambda_k2, subln_gain, conv_w, w_attn_o,
           w_conv_o, b_merge, w_out, g_pre, g_post):
    for l in range(w_in.shape[0]):
        x = _layer(x, l, w_in[l], lambda_q1[l], lambda_k1[l], lambda_q2[l], lambda_k2[l],
                   subln_gain[l], conv_w[l], w_attn_o[l], w_conv_o[l], b_merge[l], w_out[l],
                   g_pre[l], g_post[l])
    return x
```

```python
import functools
import math

import jax
import jax.numpy as jnp
from jax import lax
from jax.experimental import pallas as pl
from jax.experimental.pallas import tpu as pltpu

D_MODEL = 1024
ATTN_HEADS = 4
HEAD_DIM = 64
HEAD_COLS = 2 * HEAD_DIM
ATTN_WIDTH = ATTN_HEADS * HEAD_COLS
CONV_WIDTH = D_MODEL // 2
CONV_K = 3
NORM_EPS = 1e-6
QKV_COLS = 3 * ATTN_WIDTH
CONV_COLS = 4 * CONV_WIDTH
GATE_COLS = 2 * D_MODEL
IN_COLS = QKV_COLS + ATTN_WIDTH + CONV_COLS + GATE_COLS
MASK_VALUE = -1e30
LOG2_E = math.log2(math.e)
Q_SCALE_LOG2 = HEAD_DIM ** -0.5 * LOG2_E
ALIBI_SKIP_BITS = 160.0
NORM_SLACK = 1.01

PROJ_ROWS = 1024
PROJ_SUB_ROWS = 512
ATTN_Q_ROWS = 1024
ATTN_K_ROWS = 1024
TK_SHIFT = ATTN_K_ROWS.bit_length() - 1
ATTN_SUB_ROWS = 256
ATTN_LOOKAHEAD = 2
assert ATTN_K_ROWS == 1 << TK_SHIFT and ATTN_K_ROWS % ATTN_Q_ROWS == 0
assert PROJ_ROWS == ATTN_K_ROWS
OUT_ROWS = 1024
OUT_SUB_ROWS = 512
HALO_ROWS = 16
VMEM_LIMIT_BYTES = 56 * 1024 * 1024

BF16 = jnp.bfloat16
F32 = jnp.float32


def _lambda_init(layer_idx):
    return 0.8 - 0.6 * math.exp(-0.3 * layer_idx)


def _rms_norm_bf16(x, gain):
    ms = jnp.mean(x * x, axis=-1, keepdims=True)
    return (x * lax.rsqrt(ms + NORM_EPS) * gain).astype(BF16)


def _map_norm_bound(sq_col_max):
    lane = lax.broadcasted_iota(jnp.int32, sq_col_max.shape, 1)
    n0 = jnp.sum(jnp.where(lane < HEAD_DIM, sq_col_max, 0.0), axis=-1, keepdims=True)
    n1 = jnp.sum(jnp.where(lane >= HEAD_DIM, sq_col_max, 0.0), axis=-1, keepdims=True)
    return jnp.sqrt(jnp.maximum(n0, n1))[0, 0] * NORM_SLACK


def _qkv_proj_kernel(slopes_ref, x_ref, g_ref, w_ref, q_ref, ka_ref, kb_ref, v_ref, knorm_ref,
                     h_ref, biasa_ref, biasb_ref):
    @pl.when(pl.program_id(0) == 0)
    def _build_bias():
        lane = lax.broadcasted_iota(jnp.int32, (PROJ_ROWS, HEAD_COLS), 1)
        row = lax.broadcasted_iota(jnp.int32, (PROJ_ROWS, HEAD_COLS), 0).astype(F32)
        for hd in range(ATTN_HEADS):
            b = row * (slopes_ref[hd] * LOG2_E)
            b1 = b.astype(BF16).astype(F32)
            b2 = (b - b1).astype(BF16).astype(F32)
            b3 = (b - b1 - b2).astype(BF16).astype(F32)
            for bias_ref, first in ((biasa_ref, HEAD_DIM), (biasb_ref, 0)):
                bias_ref[hd] = jnp.where(lane == first, b1, jnp.where(
                    lane == first + 1, b2, jnp.where(lane == first + 2, b3, 0.0)))

    lane = lax.broadcasted_iota(jnp.int32, (PROJ_SUB_ROWS, HEAD_COLS), 1)
    sq_col_max = [jnp.zeros((1, HEAD_COLS), F32)] * ATTN_HEADS
    for r in range(0, PROJ_ROWS, PROJ_SUB_ROWS):
        rows = slice(r, r + PROJ_SUB_ROWS)
        h_ref[rows, :] = _rms_norm_bf16(x_ref[rows, :], g_ref[...])

        def proj(col, rows=rows):
            return jnp.dot(h_ref[rows, :], w_ref[:, col:col + ATTN_WIDTH],
                           preferred_element_type=F32)

        q_ref[rows, :] = (proj(0) * Q_SCALE_LOG2).astype(BF16)
        k = proj(ATTN_WIDTH)
        for hd in range(ATTN_HEADS):
            cols = slice(hd * HEAD_COLS, (hd + 1) * HEAD_COLS)
            kh = k[:, cols]
            ka_ref[rows, cols] = jnp.where(lane < HEAD_DIM, kh, biasa_ref[hd, rows, :]).astype(BF16)
            kb_ref[rows, cols] = jnp.where(lane >= HEAD_DIM, kh, biasb_ref[hd, rows, :]).astype(BF16)
            sq_col_max[hd] = jnp.maximum(sq_col_max[hd], jnp.max(kh * kh, axis=0, keepdims=True))
        v_ref[rows, :] = proj(2 * ATTN_WIDTH).astype(BF16)
    for hd in range(ATTN_HEADS):
        knorm_ref[pl.program_id(0), hd] = _map_norm_bound(sq_col_max[hd])


def _qkv_proj(slopes, x2d, g_pre, w_in_bf16):
    n = x2d.shape[0]
    row_block = lambda cols: pl.BlockSpec((PROJ_ROWS, cols), lambda i: (i, 0))
    act = jax.ShapeDtypeStruct((n, ATTN_WIDTH), BF16)
    return pl.pallas_call(
        _qkv_proj_kernel,
        grid=(n // PROJ_ROWS,),
        in_specs=[
            pl.BlockSpec(memory_space=pltpu.SMEM),
            row_block(D_MODEL),
            pl.BlockSpec((1, D_MODEL), lambda i: (0, 0)),
            pl.BlockSpec((D_MODEL, QKV_COLS), lambda i: (0, 0), pipeline_mode=pl.Buffered(1)),
        ],
        out_specs=[row_block(ATTN_WIDTH), row_block(ATTN_WIDTH), row_block(ATTN_WIDTH),
                   row_block(ATTN_WIDTH), pl.BlockSpec(memory_space=pltpu.SMEM)],
        out_shape=[act, act, act, act,
                   jax.ShapeDtypeStruct((n // PROJ_ROWS, ATTN_HEADS), F32)],
        scratch_shapes=[pltpu.VMEM((PROJ_ROWS, D_MODEL), BF16),
                        pltpu.VMEM((ATTN_HEADS, PROJ_ROWS, HEAD_COLS), F32),
                        pltpu.VMEM((ATTN_HEADS, PROJ_ROWS, HEAD_COLS), F32)],
        compiler_params=pltpu.CompilerParams(
            dimension_semantics=("arbitrary",), vmem_limit_bytes=VMEM_LIMIT_BYTES),
        name="qkv_proj",
    )(slopes, x2d, g_pre, w_in_bf16)


def _diff_attn_kernel(slopes_ref, knorm_ref, lq1_ref, lk1_ref, lq2_ref, lk2_ref, gain_ref,
                      q_ref, ka_ref, kb_ref, v_ref, o_ref, m_ref, acc_ref, *, lam_init):
    tq, tk, ts = ATTN_Q_ROWS, ATTN_K_ROWS, ATTN_SUB_ROWS
    spans_per_seq = ka_ref.shape[0] // tk
    batch = pl.program_id(0)
    head = pl.program_id(1)
    qi = pl.program_id(2)
    slope = slopes_ref[head] * LOG2_E

    q = q_ref[...].astype(F32)
    qlane = lax.broadcasted_iota(jnp.int32, q.shape, 1)
    q_norm = _map_norm_bound(jnp.max(q * q, axis=0, keepdims=True))
    q_maps = (
        jnp.where(qlane < HEAD_DIM, q, jnp.where(qlane < HEAD_DIM + 3, 1.0, 0.0)).astype(BF16),
        jnp.where(qlane >= HEAD_DIM, q, jnp.where(qlane < 3, 1.0, 0.0)).astype(BF16),
    )
    key_refs = (ka_ref, kb_ref)

    tri_keep = (lax.broadcasted_iota(jnp.int32, (ts, ts), 1)
                <= lax.broadcasted_iota(jnp.int32, (ts, ts), 0))

    def step(key_start, diag_cols):
        span_bias = (key_start - qi * tq).astype(F32) * slope
        first = diag_cols is not None
        streams = [(r, mp) for r in (range(tq - ts, -ts, -ts) if first else range(0, tq, ts))
                   for mp in range(2)]

        def width(r):
            return tk if diag_cols is None else diag_cols + r + ts

        def scores(r, mp):
            keys = key_refs[mp][pl.ds(key_start, width(r)), :]
            s = lax.dot_general(q_maps[mp][r:r + ts], keys, (((1,), (1,)), ((), ())),
                                preferred_element_type=F32)
            if diag_cols is not None:
                below = width(r) - ts
                blocks = [s[:, :below]] if below else []
                s = jnp.concatenate(blocks + [jnp.where(tri_keep, s[:, below:], MASK_VALUE)],
                                    axis=1)
            return s

        def update(r, mp, s):
            m_cur = jnp.max(s, axis=-1, keepdims=True) + span_bias
            if first:
                m_new = jnp.broadcast_to(m_cur, (ts, HEAD_COLS))
            else:
                m_prev = m_ref[mp, r:r + ts]
                m_new = jnp.maximum(m_prev, m_cur)
            shift = m_new - span_bias
            p = jnp.exp2(s - jnp.concatenate([shift] * (width(r) // HEAD_COLS), axis=1))
            v_ones = jnp.concatenate([v_ref[pl.ds(key_start, width(r)), :],
                                      jnp.ones((width(r), HEAD_COLS), BF16)], axis=1)
            pv = jnp.dot(p.astype(BF16), v_ones, preferred_element_type=F32)
            if not first:
                alpha = jnp.exp2(m_prev - m_new)
                pv = jnp.concatenate([alpha, alpha], axis=1) * acc_ref[mp, r:r + ts] + pv
            acc_ref[mp, r:r + ts] = pv
            m_ref[mp, r:r + ts] = m_new

        pending = [scores(*st) for st in streams[:ATTN_LOOKAHEAD]]
        for i, st in enumerate(streams):
            if i + ATTN_LOOKAHEAD < len(streams):
                pending.append(scores(*streams[i + ATTN_LOOKAHEAD]))
            update(*st, pending.pop(0))

    n_spans = lax.shift_right_logical(qi * tq, TK_SHIFT)
    tail_start = pl.multiple_of(n_spans * tk, tk)
    for extra in range(0, tk, tq):
        @pl.when(qi * tq - tail_start == extra)
        def _diag_step(extra=extra):
            step(tail_start, extra)

    m_low = jnp.minimum(m_ref[0], m_ref[1])
    m_low = jnp.min(jnp.min(m_low, axis=0, keepdims=True), axis=1, keepdims=True)[0, 0]

    def full_span(kj, carry):
        key_start = pl.multiple_of(kj * tk, tk)
        bound = (q_norm * knorm_ref[batch * spans_per_seq + kj, head]
                 + (key_start + (tk - 1) - qi * tq).astype(F32) * slope)

        @pl.when(bound > m_low - ALIBI_SKIP_BITS)
        def _fold_span():
            step(key_start, None)

        return carry

    lax.fori_loop(0, n_spans, full_span, 0)

    lam = (jnp.exp(jnp.sum(lq1_ref[...] * lk1_ref[...], axis=-1, keepdims=True))
           - jnp.exp(jnp.sum(lq2_ref[...] * lk2_ref[...], axis=-1, keepdims=True))
           + lam_init)
    o = (acc_ref[0, :, 0:HEAD_COLS] / acc_ref[0, :, HEAD_COLS:2 * HEAD_COLS]
         - lam * (acc_ref[1, :, 0:HEAD_COLS] / acc_ref[1, :, HEAD_COLS:2 * HEAD_COLS]))
    ms = jnp.mean(o * o, axis=-1, keepdims=True)
    o = (o * lax.rsqrt(ms + NORM_EPS) * gain_ref[...]) * (1.0 - lam_init)
    o_ref[...] = o.astype(BF16)


def _diff_attn(q, ka, kb, v, slopes, knorm, lq1, lk1, lq2, lk2, gain, *, batch, seq, lam_init):
    nq = seq // ATTN_Q_ROWS
    small = lambda cols: pl.BlockSpec((1, cols), lambda b, h, i: (0, 0))
    seq_block = pl.BlockSpec((seq, HEAD_COLS), lambda b, h, i: (b, h))
    q_block = pl.BlockSpec((ATTN_Q_ROWS, HEAD_COLS), lambda b, h, i: (b * nq + i, h))
    return pl.pallas_call(
        functools.partial(_diff_attn_kernel, lam_init=lam_init),
        grid=(batch, ATTN_HEADS, nq),
        in_specs=[
            pl.BlockSpec(memory_space=pltpu.SMEM), pl.BlockSpec(memory_space=pltpu.SMEM),
            small(HEAD_DIM), small(HEAD_DIM), small(HEAD_DIM), small(HEAD_DIM),
            small(HEAD_COLS),
            q_block, seq_block, seq_block, seq_block,
        ],
        out_specs=q_block,
        out_shape=jax.ShapeDtypeStruct((batch * seq, ATTN_WIDTH), BF16),
        scratch_shapes=[pltpu.VMEM((2, ATTN_Q_ROWS, HEAD_COLS), F32),
                        pltpu.VMEM((2, ATTN_Q_ROWS, 2 * HEAD_COLS), F32)],
        compiler_params=pltpu.CompilerParams(
            dimension_semantics=("arbitrary", "arbitrary", "arbitrary"),
            vmem_limit_bytes=VMEM_LIMIT_BYTES),
        name="diff_attn",
    )(slopes, knorm, lq1, lk1, lq2, lk2, gain, q, ka, kb, v)


def _sigmoid(z):
    return 1.0 / (1.0 + jnp.exp(-z))


def _out_proj_kernel(x_ref, o_ref, gpre_ref, wmid_ref, whi_ref, cw_ref, wa_ref, wc_ref, bm_ref,
                     wo_ref, gp_ref, out_ref, h_ref, u_ref, *, tiles_per_seq):
    tm, ts, cwid = OUT_ROWS, OUT_SUB_ROWS, CONV_WIDTH

    @pl.when(pl.program_id(0) % tiles_per_seq == 0)
    def _sequence_start():
        u_ref[0:HALO_ROWS, :] = jnp.zeros((HALO_ROWS, cwid), F32)

    for r in range(0, tm, ts):
        x = x_ref[r:r + ts, :]
        h_ref[r:r + ts, :] = _rms_norm_bf16(x, gpre_ref[...])

        def proj(w_ref, col, cols, r=r):
            return jnp.dot(h_ref[r:r + ts, :], w_ref[:, col:col + cols],
                           preferred_element_type=F32)

        z_a = proj(wmid_ref, 0, ATTN_WIDTH)
        ya_in = o_ref[r:r + ts, :].astype(F32) * (z_a * _sigmoid(z_a))
        y_attn = jnp.dot(ya_in.astype(BF16), wa_ref[...], preferred_element_type=F32)

        u = proj(wmid_ref, ATTN_WIDTH + cwid, cwid) * proj(whi_ref, 0, cwid)
        u_ref[HALO_ROWS + r:HALO_ROWS + r + ts, :] = u
        conv = cw_ref[CONV_K - 1:CONV_K, :] * u
        for k in range(CONV_K - 1):
            first_row = HALO_ROWS + r - (CONV_K - 1 - k)
            conv = conv + cw_ref[k:k + 1, :] * u_ref[first_row:first_row + ts, :]
        z_c = proj(whi_ref, cwid, cwid)
        yc_in = (proj(wmid_ref, ATTN_WIDTH, cwid) * conv) * (z_c * _sigmoid(z_c))
        y_conv = jnp.dot(yc_in.astype(BF16), wc_ref[...], preferred_element_type=F32)

        g_a = _sigmoid(proj(whi_ref, 2 * cwid, D_MODEL) + bm_ref[:, 0:D_MODEL])
        g_c = _sigmoid(proj(whi_ref, 2 * cwid + D_MODEL, D_MODEL)
                       + bm_ref[:, D_MODEL:2 * D_MODEL])
        y = g_a * y_attn + g_c * y_conv
        out = jnp.dot(y.astype(BF16), wo_ref[...], preferred_element_type=F32)
        ms = jnp.mean(out * out, axis=-1, keepdims=True)
        out_ref[r:r + ts, :] = x + out * lax.rsqrt(ms + NORM_EPS) * gp_ref[...]

    u_ref[0:HALO_ROWS, :] = u_ref[tm:tm + HALO_ROWS, :]


def _out_proj(x2d, o, g_pre, w_in_bf16, conv_w, wa, wc, b_merge, wo, g_post, *, seq):
    n = x2d.shape[0]
    mid_cols = ATTN_WIDTH + 2 * CONV_WIDTH
    hi_cols = 2 * CONV_WIDTH + GATE_COLS
    assert QKV_COLS == mid_cols and QKV_COLS + mid_cols == hi_cols == IN_COLS - hi_cols
    row_block = lambda cols: pl.BlockSpec((OUT_ROWS, cols), lambda i: (i, 0))
    const = lambda shape, col_block=0: pl.BlockSpec(shape, lambda i: (0, col_block),
                                                   pipeline_mode=pl.Buffered(1))
    return pl.pallas_call(
        functools.partial(_out_proj_kernel, tiles_per_seq=seq // OUT_ROWS),
        grid=(n // OUT_ROWS,),
        in_specs=[
            row_block(D_MODEL), row_block(ATTN_WIDTH), const((1, D_MODEL)),
            const((D_MODEL, mid_cols), 1), const((D_MODEL, hi_cols), 1),
            const((CONV_K, CONV_WIDTH)), const((ATTN_WIDTH, D_MODEL)),
            const((CONV_WIDTH, D_MODEL)), const((1, 2 * D_MODEL)), const((D_MODEL, D_MODEL)),
            const((1, D_MODEL)),
        ],
        out_specs=row_block(D_MODEL),
        out_shape=jax.ShapeDtypeStruct((n, D_MODEL), F32),
        scratch_shapes=[pltpu.VMEM((OUT_ROWS, D_MODEL), BF16),
                        pltpu.VMEM((HALO_ROWS + OUT_ROWS, CONV_WIDTH), F32)],
        compiler_params=pltpu.CompilerParams(
            dimension_semantics=("arbitrary",), vmem_limit_bytes=VMEM_LIMIT_BYTES),
        name="out_proj",
    )(x2d, o, g_pre, w_in_bf16, w_in_bf16, conv_w, wa, wc, b_merge, wo, g_post)


def _layer(x, layer_idx, w_in, lq1, lk1, lq2, lk2, subln_gain, conv_w, w_attn_o, w_conv_o,
           b_merge, w_out, g_pre, g_post):
    batch, seq, d = x.shape
    x2d = x.reshape(batch * seq, d)
    row = lambda a: a.reshape(1, -1).astype(F32)
    slopes = jnp.asarray([2.0 ** (-8.0 * (i + 1) / ATTN_HEADS) for i in range(ATTN_HEADS)], F32)

    w_in_bf16 = w_in.astype(BF16)
    q, ka, kb, v, knorm = _qkv_proj(slopes, x2d, row(g_pre), w_in_bf16)
    o = _diff_attn(q, ka, kb, v, slopes, knorm, row(lq1), row(lk1), row(lq2), row(lk2),
                   row(subln_gain), batch=batch, seq=seq, lam_init=_lambda_init(layer_idx))
    out = _out_proj(x2d, o, row(g_pre), w_in_bf16, conv_w.astype(F32), w_attn_o.astype(BF16),
                    w_conv_o.astype(BF16), row(b_merge), w_out.astype(BF16), row(g_post), seq=seq)
    return out.reshape(batch, seq, d)


def kernel(x, w_in, lambda_q1, lambda_k1, lambda_q2, lambda_k2, subln_gain, conv_w, w_attn_o,
           w_conv_o, b_merge, w_out, g_pre, g_post):
    for l in range(w_in.shape[0]):
        x = _layer(x, l, w_in[l], lambda_q1[l], lambda_k1[l], lambda_q2[l], lambda_k2[l],
                   subln_gain[l], conv_w[l], w_attn_o[l], w_conv_o[l], b_merge[l], w_out[l],
                   g_pre[l], g_post[l])
    return x
```

```python
import functools
import math

import jax
import jax.numpy as jnp
from jax import lax
from jax.experimental import pallas as pl
from jax.experimental.pallas import tpu as pltpu

D_MODEL = 1024
ATTN_HEADS = 4
HEAD_DIM = 64
HEAD_COLS = 2 * HEAD_DIM
ATTN_WIDTH = ATTN_HEADS * HEAD_COLS
CONV_WIDTH = D_MODEL // 2
CONV_K = 3
NORM_EPS = 1e-6
QKV_COLS = 3 * ATTN_WIDTH
CONV_COLS = 4 * CONV_WIDTH
GATE_COLS = 2 * D_MODEL
IN_COLS = QKV_COLS + ATTN_WIDTH + CONV_COLS + GATE_COLS
MASK_VALUE = -1e30
LOG2_E = math.log2(math.e)
Q_SCALE_LOG2 = HEAD_DIM ** -0.5 * LOG2_E
ALIBI_SKIP_BITS = 160.0
NORM_SLACK = 1.01

PROJ_ROWS = 1024
PROJ_SUB_ROWS = 512
ATTN_Q_ROWS = 1024
ATTN_K_ROWS = 1024
TK_SHIFT = ATTN_K_ROWS.bit_length() - 1
ATTN_SUB_ROWS = 256
ATTN_LOOKAHEAD = 2
assert ATTN_K_ROWS == 1 << TK_SHIFT and ATTN_K_ROWS % ATTN_Q_ROWS == 0
assert PROJ_ROWS == ATTN_K_ROWS
OUT_ROWS = 1024
OUT_SUB_ROWS = 512
HALO_ROWS = 16
VMEM_LIMIT_BYTES = 56 * 1024 * 1024

BF16 = jnp.bfloat16
F32 = jnp.float32


def _lambda_init(layer_idx):
    return 0.8 - 0.6 * math.exp(-0.3 * layer_idx)


def _rms_norm_bf16(x, gain):
    ms = jnp.mean(x * x, axis=-1, keepdims=True)
    return (x * lax.rsqrt(ms + NORM_EPS) * gain).astype(BF16)


def _map_norm_bound(sq_col_max):
    lane = lax.broadcasted_iota(jnp.int32, sq_col_max.shape, 1)
    n0 = jnp.sum(jnp.where(lane < HEAD_DIM, sq_col_max, 0.0), axis=-1, keepdims=True)
    n1 = jnp.sum(jnp.where(lane >= HEAD_DIM, sq_col_max, 0.0), axis=-1, keepdims=True)
    return jnp.sqrt(jnp.maximum(n0, n1))[0, 0] * NORM_SLACK


def _qkv_proj_kernel(slopes_ref, x_ref, g_ref, wqkv_ref, win_ref, wa_ref, wc_ref, wo_ref,
                     q_ref, ka_ref, kb_ref, v_ref, knorm_ref, win16_ref, wa16_ref, wc16_ref,
                     wo16_ref, h_ref, biasa_ref, biasb_ref, w_ref):
    for w32_ref, w16_ref in ((win_ref, win16_ref), (wa_ref, wa16_ref), (wc_ref, wc16_ref),
                             (wo_ref, wo16_ref)):
        w16_ref[...] = w32_ref[...].astype(BF16)

    @pl.when(pl.program_id(0) == 0)
    def _first_step():
        w_ref[...] = wqkv_ref[...].astype(BF16)
        lane = lax.broadcasted_iota(jnp.int32, (PROJ_ROWS, HEAD_COLS), 1)
        row = lax.broadcasted_iota(jnp.int32, (PROJ_ROWS, HEAD_COLS), 0).astype(F32)
        for hd in range(ATTN_HEADS):
            b = row * (slopes_ref[hd] * LOG2_E)
            b1 = b.astype(BF16).astype(F32)
            b2 = (b - b1).astype(BF16).astype(F32)
            b3 = (b - b1 - b2).astype(BF16).astype(F32)
            for bias_ref, first in ((biasa_ref, HEAD_DIM), (biasb_ref, 0)):
                bias_ref[hd] = jnp.where(lane == first, b1, jnp.where(
                    lane == first + 1, b2, jnp.where(lane == first + 2, b3, 0.0)))

    lane = lax.broadcasted_iota(jnp.int32, (PROJ_SUB_ROWS, HEAD_COLS), 1)
    sq_col_max = [jnp.zeros((1, HEAD_COLS), F32)] * ATTN_HEADS
    for r in range(0, PROJ_ROWS, PROJ_SUB_ROWS):
        rows = slice(r, r + PROJ_SUB_ROWS)
        h_ref[rows, :] = _rms_norm_bf16(x_ref[rows, :], g_ref[...])

        def proj(col, rows=rows):
            return jnp.dot(h_ref[rows, :], w_ref[:, col:col + ATTN_WIDTH],
                           preferred_element_type=F32)

        q_ref[rows, :] = (proj(0) * Q_SCALE_LOG2).astype(BF16)
        k = proj(ATTN_WIDTH)
        for hd in range(ATTN_HEADS):
            cols = slice(hd * HEAD_COLS, (hd + 1) * HEAD_COLS)
            kh = k[:, cols]
            ka_ref[rows, cols] = jnp.where(lane < HEAD_DIM, kh, biasa_ref[hd, rows, :]).astype(BF16)
            kb_ref[rows, cols] = jnp.where(lane >= HEAD_DIM, kh, biasb_ref[hd, rows, :]).astype(BF16)
            sq_col_max[hd] = jnp.maximum(sq_col_max[hd], jnp.max(kh * kh, axis=0, keepdims=True))
        v_ref[rows, :] = proj(2 * ATTN_WIDTH).astype(BF16)
    for hd in range(ATTN_HEADS):
        knorm_ref[pl.program_id(0), hd] = _map_norm_bound(sq_col_max[hd])


def _qkv_proj(slopes, x2d, g_pre, w_in, w_attn_o, w_conv_o, w_out):
    n = x2d.shape[0]
    steps = n // PROJ_ROWS
    row_block = lambda cols: pl.BlockSpec((PROJ_ROWS, cols), lambda i: (i, 0))
    act = jax.ShapeDtypeStruct((n, ATTN_WIDTH), BF16)
    weights = (w_in, w_attn_o, w_conv_o, w_out)
    assert all(w.shape[0] % (steps * HALO_ROWS) == 0 for w in weights)
    slab = lambda w: pl.BlockSpec((w.shape[0] // steps, w.shape[1]), lambda i: (i, 0))
    return pl.pallas_call(
        _qkv_proj_kernel,
        grid=(steps,),
        in_specs=[
            pl.BlockSpec(memory_space=pltpu.SMEM),
            row_block(D_MODEL),
            pl.BlockSpec((1, D_MODEL), lambda i: (0, 0)),
            pl.BlockSpec((D_MODEL, QKV_COLS), lambda i: (0, 0), pipeline_mode=pl.Buffered(1)),
        ] + [slab(w) for w in weights],
        out_specs=[row_block(ATTN_WIDTH), row_block(ATTN_WIDTH), row_block(ATTN_WIDTH),
                   row_block(ATTN_WIDTH), pl.BlockSpec(memory_space=pltpu.SMEM)]
        + [slab(w) for w in weights],
        out_shape=[act, act, act, act, jax.ShapeDtypeStruct((steps, ATTN_HEADS), F32)]
        + [jax.ShapeDtypeStruct(w.shape, BF16) for w in weights],
        scratch_shapes=[pltpu.VMEM((PROJ_ROWS, D_MODEL), BF16),
                        pltpu.VMEM((ATTN_HEADS, PROJ_ROWS, HEAD_COLS), F32),
                        pltpu.VMEM((ATTN_HEADS, PROJ_ROWS, HEAD_COLS), F32),
                        pltpu.VMEM((D_MODEL, QKV_COLS), BF16)],
        compiler_params=pltpu.CompilerParams(
            dimension_semantics=("arbitrary",), vmem_limit_bytes=VMEM_LIMIT_BYTES),
        name="qkv_proj",
    )(slopes, x2d, g_pre, w_in, w_in, w_attn_o, w_conv_o, w_out)


def _diff_attn_kernel(slopes_ref, knorm_ref, lq1_ref, lk1_ref, lq2_ref, lk2_ref, gain_ref,
                      q_ref, ka_ref, kb_ref, v_ref, o_ref, m_ref, acc_ref, *, lam_init):
    tq, tk, ts = ATTN_Q_ROWS, ATTN_K_ROWS, ATTN_SUB_ROWS
    spans_per_seq = ka_ref.shape[0] // tk
    batch = pl.program_id(0)
    head = pl.program_id(1)
    qi = pl.program_id(2)
    slope = slopes_ref[head] * LOG2_E

    q = q_ref[...].astype(F32)
    qlane = lax.broadcasted_iota(jnp.int32, q.shape, 1)
    q_norm = _map_norm_bound(jnp.max(q * q, axis=0, keepdims=True))
    q_maps = (
        jnp.where(qlane < HEAD_DIM, q, jnp.where(qlane < HEAD_DIM + 3, 1.0, 0.0)).astype(BF16),
        jnp.where(qlane >= HEAD_DIM, q, jnp.where(qlane < 3, 1.0, 0.0)).astype(BF16),
    )
    key_refs = (ka_ref, kb_ref)

    tri_keep = (lax.broadcasted_iota(jnp.int32, (ts, ts), 1)
                <= lax.broadcasted_iota(jnp.int32, (ts, ts), 0))

    def step(key_start, diag_cols):
        span_bias = (key_start - qi * tq).astype(F32) * slope
        first = diag_cols is not None
        streams = [(r, mp) for r in (range(tq - ts, -ts, -ts) if first else range(0, tq, ts))
                   for mp in range(2)]

        def width(r):
            return tk if diag_cols is None else diag_cols + r + ts

        def scores(r, mp):
            keys = key_refs[mp][pl.ds(key_start, width(r)), :]
            s = lax.dot_general(q_maps[mp][r:r + ts], keys, (((1,), (1,)), ((), ())),
                                preferred_element_type=F32)
            if diag_cols is not None:
                below = width(r) - ts
                blocks = [s[:, :below]] if below else []
                s = jnp.concatenate(blocks + [jnp.where(tri_keep, s[:, below:], MASK_VALUE)],
                                    axis=1)
            return s

        def update(r, mp, s):
            m_cur = jnp.max(s, axis=-1, keepdims=True) + span_bias
            if first:
                m_new = jnp.broadcast_to(m_cur, (ts, HEAD_COLS))
            else:
                m_prev = m_ref[mp, r:r + ts]
                m_new = jnp.maximum(m_prev, m_cur)
            shift = m_new - span_bias
            p = jnp.exp2(s - jnp.concatenate([shift] * (width(r) // HEAD_COLS), axis=1))
            v_ones = jnp.concatenate([v_ref[pl.ds(key_start, width(r)), :],
                                      jnp.ones((width(r), HEAD_COLS), BF16)], axis=1)
            pv = jnp.dot(p.astype(BF16), v_ones, preferred_element_type=F32)
            if not first:
                alpha = jnp.exp2(m_prev - m_new)
                pv = jnp.concatenate([alpha, alpha], axis=1) * acc_ref[mp, r:r + ts] + pv
            acc_ref[mp, r:r + ts] = pv
            m_ref[mp, r:r + ts] = m_new

        pending = [scores(*st) for st in streams[:ATTN_LOOKAHEAD]]
        for i, st in enumerate(streams):
            if i + ATTN_LOOKAHEAD < len(streams):
                pending.append(scores(*streams[i + ATTN_LOOKAHEAD]))
            update(*st, pending.pop(0))

    n_spans = lax.shift_right_logical(qi * tq, TK_SHIFT)
    tail_start = pl.multiple_of(n_spans * tk, tk)
    for extra in range(0, tk, tq):
        @pl.when(qi * tq - tail_start == extra)
        def _diag_step(extra=extra):
            step(tail_start, extra)

    m_low = jnp.minimum(m_ref[0], m_ref[1])
    m_low = jnp.min(jnp.min(m_low, axis=0, keepdims=True), axis=1, keepdims=True)[0, 0]

    def full_span(kj, carry):
        key_start = pl.multiple_of(kj * tk, tk)
        bound = (q_norm * knorm_ref[batch * spans_per_seq + kj, head]
                 + (key_start + (tk - 1) - qi * tq).astype(F32) * slope)

        @pl.when(bound > m_low - ALIBI_SKIP_BITS)
        def _fold_span():
            step(key_start, None)

        return carry

    lax.fori_loop(0, n_spans, full_span, 0)

    lam = (jnp.exp(jnp.sum(lq1_ref[...] * lk1_ref[...], axis=-1, keepdims=True))
           - jnp.exp(jnp.sum(lq2_ref[...] * lk2_ref[...], axis=-1, keepdims=True))
           + lam_init)
    o = (acc_ref[0, :, 0:HEAD_COLS] / acc_ref[0, :, HEAD_COLS:2 * HEAD_COLS]
         - lam * (acc_ref[1, :, 0:HEAD_COLS] / acc_ref[1, :, HEAD_COLS:2 * HEAD_COLS]))
    ms = jnp.mean(o * o, axis=-1, keepdims=True)
    o = (o * lax.rsqrt(ms + NORM_EPS) * gain_ref[...]) * (1.0 - lam_init)
    o_ref[...] = o.astype(BF16)


def _diff_attn(q, ka, kb, v, slopes, knorm, lq1, lk1, lq2, lk2, gain, *, batch, seq, lam_init):
    nq = seq // ATTN_Q_ROWS
    small = lambda cols: pl.BlockSpec((1, cols), lambda b, h, i: (0, 0))
    seq_block = pl.BlockSpec((seq, HEAD_COLS), lambda b, h, i: (b, h))
    q_block = pl.BlockSpec((ATTN_Q_ROWS, HEAD_COLS), lambda b, h, i: (b * nq + i, h))
    return pl.pallas_call(
        functools.partial(_diff_attn_kernel, lam_init=lam_init),
        grid=(batch, ATTN_HEADS, nq),
        in_specs=[
            pl.BlockSpec(memory_space=pltpu.SMEM), pl.BlockSpec(memory_space=pltpu.SMEM),
            small(HEAD_DIM), small(HEAD_DIM), small(HEAD_DIM), small(HEAD_DIM),
            small(HEAD_COLS),
            q_block, seq_block, seq_block, seq_block,
        ],
        out_specs=q_block,
        out_shape=jax.ShapeDtypeStruct((batch * seq, ATTN_WIDTH), BF16),
        scratch_shapes=[pltpu.VMEM((2, ATTN_Q_ROWS, HEAD_COLS), F32),
                        pltpu.VMEM((2, ATTN_Q_ROWS, 2 * HEAD_COLS), F32)],
        compiler_params=pltpu.CompilerParams(
            dimension_semantics=("arbitrary", "arbitrary", "arbitrary"),
            vmem_limit_bytes=VMEM_LIMIT_BYTES),
        name="diff_attn",
    )(slopes, knorm, lq1, lk1, lq2, lk2, gain, q, ka, kb, v)


def _sigmoid(z):
    return 1.0 / (1.0 + jnp.exp(-z))


def _out_proj_kernel(x_ref, o_ref, gpre_ref, wmid_ref, whi_ref, cw_ref, wa_ref, wc_ref, bm_ref,
                     wo_ref, gp_ref, out_ref, h_ref, u_ref, *, tiles_per_seq):
    tm, ts, cwid = OUT_ROWS, OUT_SUB_ROWS, CONV_WIDTH

    @pl.when(pl.program_id(0) % tiles_per_seq == 0)
    def _sequence_start():
        u_ref[0:HALO_ROWS, :] = jnp.zeros((HALO_ROWS, cwid), F32)

    for r in range(0, tm, ts):
        x = x_ref[r:r + ts, :]
        h_ref[r:r + ts, :] = _rms_norm_bf16(x, gpre_ref[...])

        def proj(w_ref, col, cols, r=r):
            return jnp.dot(h_ref[r:r + ts, :], w_ref[:, col:col + cols],
                           preferred_element_type=F32)

        z_a = proj(wmid_ref, 0, ATTN_WIDTH)
        ya_in = o_ref[r:r + ts, :].astype(F32) * (z_a * _sigmoid(z_a))
        y_attn = jnp.dot(ya_in.astype(BF16), wa_ref[...], preferred_element_type=F32)

        u = proj(wmid_ref, ATTN_WIDTH + cwid, cwid) * proj(whi_ref, 0, cwid)
        u_ref[HALO_ROWS + r:HALO_ROWS + r + ts, :] = u
        conv = cw_ref[CONV_K - 1:CONV_K, :] * u
        for k in range(CONV_K - 1):
            first_row = HALO_ROWS + r - (CONV_K - 1 - k)
            conv = conv + cw_ref[k:k + 1, :] * u_ref[first_row:first_row + ts, :]
        z_c = proj(whi_ref, cwid, cwid)
        yc_in = (proj(wmid_ref, ATTN_WIDTH, cwid) * conv) * (z_c * _sigmoid(z_c))
        y_conv = jnp.dot(yc_in.astype(BF16), wc_ref[...], preferred_element_type=F32)

        g_a = _sigmoid(proj(whi_ref, 2 * cwid, D_MODEL) + bm_ref[:, 0:D_MODEL])
        g_c = _sigmoid(proj(whi_ref, 2 * cwid + D_MODEL, D_MODEL)
                       + bm_ref[:, D_MODEL:2 * D_MODEL])
        y = g_a * y_attn + g_c * y_conv
        out = jnp.dot(y.astype(BF16), wo_ref[...], preferred_element_type=F32)
        ms = jnp.mean(out * out, axis=-1, keepdims=True)
        out_ref[r:r + ts, :] = x + out * lax.rsqrt(ms + NORM_EPS) * gp_ref[...]

    u_ref[0:HALO_ROWS, :] = u_ref[tm:tm + HALO_ROWS, :]


def _out_proj(x2d, o, g_pre, w_in_bf16, conv_w, wa, wc, b_merge, wo, g_post, *, seq):
    n = x2d.shape[0]
    mid_cols = ATTN_WIDTH + 2 * CONV_WIDTH
    hi_cols = 2 * CONV_WIDTH + GATE_COLS
    assert QKV_COLS == mid_cols and QKV_COLS + mid_cols == hi_cols == IN_COLS - hi_cols
    row_block = lambda cols: pl.BlockSpec((OUT_ROWS, cols), lambda i: (i, 0))
    const = lambda shape, col_block=0: pl.BlockSpec(shape, lambda i: (0, col_block),
                                                   pipeline_mode=pl.Buffered(1))
    return pl.pallas_call(
        functools.partial(_out_proj_kernel, tiles_per_seq=seq // OUT_ROWS),
        grid=(n // OUT_ROWS,),
        in_specs=[
            row_block(D_MODEL), row_block(ATTN_WIDTH), const((1, D_MODEL)),
            const((D_MODEL, mid_cols), 1), const((D_MODEL, hi_cols), 1),
            const((CONV_K, CONV_WIDTH)), const((ATTN_WIDTH, D_MODEL)),
            const((CONV_WIDTH, D_MODEL)), const((1, 2 * D_MODEL)), const((D_MODEL, D_MODEL)),
            const((1, D_MODEL)),
        ],
        out_specs=row_block(D_MODEL),
        out_shape=jax.ShapeDtypeStruct((n, D_MODEL), F32),
        scratch_shapes=[pltpu.VMEM((OUT_ROWS, D_MODEL), BF16),
                        pltpu.VMEM((HALO_ROWS + OUT_ROWS, CONV_WIDTH), F32)],
        compiler_params=pltpu.CompilerParams(
            dimension_semantics=("arbitrary",), vmem_limit_bytes=VMEM_LIMIT_BYTES),
        name="out_proj",
    )(x2d, o, g_pre, w_in_bf16, w_in_bf16, conv_w, wa, wc, b_merge, wo, g_post)


def _layer(x, layer_idx, w_in, lq1, lk1, lq2, lk2, subln_gain, conv_w, w_attn_o, w_conv_o,
           b_merge, w_out, g_pre, g_post):
    batch, seq, d = x.shape
    x2d = x.reshape(batch * seq, d)
    row = lambda a: a.reshape(1, -1).astype(F32)
    slopes = jnp.asarray([2.0 ** (-8.0 * (i + 1) / ATTN_HEADS) for i in range(ATTN_HEADS)], F32)

    q, ka, kb, v, knorm, w_in_bf16, wa_bf16, wc_bf16, wo_bf16 = _qkv_proj(
        slopes, x2d, row(g_pre), w_in, w_attn_o, w_conv_o, w_out)
    o = _diff_attn(q, ka, kb, v, slopes, knorm, row(lq1), row(lk1), row(lq2), row(lk2),
                   row(subln_gain), batch=batch, seq=seq, lam_init=_lambda_init(layer_idx))
    out = _out_proj(x2d, o, row(g_pre), w_in_bf16, conv_w.astype(F32), wa_bf16, wc_bf16,
                    row(b_merge), wo_bf16, row(g_post), seq=seq)
    return out.reshape(batch, seq, d)


def kernel(x, w_in, lambda_q1, lambda_k1, lambda_q2, lambda_k2, subln_gain, conv_w, w_attn_o,
           w_conv_o, b_merge, w_out, g_pre, g_post):
    for l in range(w_in.shape[0]):
        x = _layer(x, l, w_in[l], lambda_q1[l], lambda_k1[l], lambda_q2[l], lambda_k2[l],
                   subln_gain[l], conv_w[l], w_attn_o[l], w_conv_o[l], b_merge[l], w_out[l],
                   g_pre[l], g_post[l])
    return x
```

```python
import functools
import math

import jax
import jax.numpy as jnp
from jax import lax
from jax.experimental import pallas as pl
from jax.experimental.pallas import tpu as pltpu

D_MODEL = 1024
ATTN_HEADS = 4
HEAD_DIM = 64
HEAD_COLS = 2 * HEAD_DIM
ATTN_WIDTH = ATTN_HEADS * HEAD_COLS
CONV_WIDTH = D_MODEL // 2
CONV_K = 3
NORM_EPS = 1e-6
QKV_COLS = 3 * ATTN_WIDTH
CONV_COLS = 4 * CONV_WIDTH
GATE_COLS = 2 * D_MODEL
IN_COLS = QKV_COLS + ATTN_WIDTH + CONV_COLS + GATE_COLS
MASK_VALUE = -1e30
LOG2_E = math.log2(math.e)
Q_SCALE_LOG2 = HEAD_DIM ** -0.5 * LOG2_E
ALIBI_SKIP_BITS = 160.0
NORM_SLACK = 1.01

PROJ_ROWS = 1024
PROJ_SUB_ROWS = 512
ATTN_Q_ROWS = 1024
ATTN_K_ROWS = 1024
TK_SHIFT = ATTN_K_ROWS.bit_length() - 1
ATTN_SUB_ROWS = 256
ATTN_LOOKAHEAD = 2
assert ATTN_K_ROWS == 1 << TK_SHIFT and ATTN_K_ROWS % ATTN_Q_ROWS == 0
assert PROJ_ROWS == ATTN_K_ROWS
assert ATTN_Q_ROWS % PROJ_SUB_ROWS == 0 and PROJ_SUB_ROWS % ATTN_SUB_ROWS == 0
OUT_ROWS = 1024
OUT_SUB_ROWS = 512
HALO_ROWS = 16
VMEM_LIMIT_BYTES = 56 * 1024 * 1024

BF16 = jnp.bfloat16
F32 = jnp.float32


def _lambda_init(layer_idx):
    return 0.8 - 0.6 * math.exp(-0.3 * layer_idx)


def _rms_norm_bf16(x, gain):
    ms = jnp.mean(x * x, axis=-1, keepdims=True)
    return (x * lax.rsqrt(ms + NORM_EPS) * gain).astype(BF16)


def _max_map_norm(x, lane):
    sq = x * x
    n0 = jnp.sum(jnp.where(lane < HEAD_DIM, sq, 0.0), axis=-1, keepdims=True)
    n1 = jnp.sum(jnp.where(lane >= HEAD_DIM, sq, 0.0), axis=-1, keepdims=True)
    return jnp.sqrt(jnp.max(jnp.maximum(n0, n1), axis=0, keepdims=True))[0, 0] * NORM_SLACK


def _qkv_proj_kernel(slopes_ref, x_ref, g_ref, wqkv_ref, win_ref, wa_ref, wc_ref, wo_ref,
                     q_ref, ka_ref, kb_ref, v_ref, qnorm_ref, knorm_ref, win16_ref, wa16_ref,
                     wc16_ref, wo16_ref, h_ref, biasa_ref, biasb_ref, w_ref):
    for w32_ref, w16_ref in ((win_ref, win16_ref), (wa_ref, wa16_ref), (wc_ref, wc16_ref),
                             (wo_ref, wo16_ref)):
        w16_ref[...] = w32_ref[...].astype(BF16)

    @pl.when(pl.program_id(0) == 0)
    def _first_step():
        w_ref[...] = wqkv_ref[...].astype(BF16)
        lane = lax.broadcasted_iota(jnp.int32, (PROJ_ROWS, HEAD_COLS), 1)
        row = lax.broadcasted_iota(jnp.int32, (PROJ_ROWS, HEAD_COLS), 0).astype(F32)
        for hd in range(ATTN_HEADS):
            b = row * (slopes_ref[hd] * LOG2_E)
            b1 = b.astype(BF16).astype(F32)
            b2 = (b - b1).astype(BF16).astype(F32)
            b3 = (b - b1 - b2).astype(BF16).astype(F32)
            for bias_ref, first in ((biasa_ref, HEAD_DIM), (biasb_ref, 0)):
                bias_ref[hd] = jnp.where(lane == first, b1, jnp.where(
                    lane == first + 1, b2, jnp.where(lane == first + 2, b3, 0.0)))

    lane = lax.broadcasted_iota(jnp.int32, (PROJ_SUB_ROWS, HEAD_COLS), 1)
    for r in range(0, PROJ_ROWS, PROJ_SUB_ROWS):
        rows = slice(r, r + PROJ_SUB_ROWS)
        norm_row = pl.program_id(0) * (PROJ_ROWS // PROJ_SUB_ROWS) + r // PROJ_SUB_ROWS
        h_ref[rows, :] = _rms_norm_bf16(x_ref[rows, :], g_ref[...])

        def proj(col, rows=rows):
            return jnp.dot(h_ref[rows, :], w_ref[:, col:col + ATTN_WIDTH],
                           preferred_element_type=F32)

        q = proj(0) * Q_SCALE_LOG2
        q_ref[rows, :] = q.astype(BF16)
        k = proj(ATTN_WIDTH)
        for hd in range(ATTN_HEADS):
            cols = slice(hd * HEAD_COLS, (hd + 1) * HEAD_COLS)
            kh = k[:, cols]
            ka_ref[rows, cols] = jnp.where(lane < HEAD_DIM, kh, biasa_ref[hd, rows, :]).astype(BF16)
            kb_ref[rows, cols] = jnp.where(lane >= HEAD_DIM, kh, biasb_ref[hd, rows, :]).astype(BF16)
            qnorm_ref[norm_row, hd] = _max_map_norm(q[:, cols], lane)
            knorm_ref[norm_row, hd] = _max_map_norm(kh, lane)
        v_ref[rows, :] = proj(2 * ATTN_WIDTH).astype(BF16)


def _qkv_proj(slopes, x2d, g_pre, w_in, w_attn_o, w_conv_o, w_out):
    n = x2d.shape[0]
    steps = n // PROJ_ROWS
    row_block = lambda cols: pl.BlockSpec((PROJ_ROWS, cols), lambda i: (i, 0))
    act = jax.ShapeDtypeStruct((n, ATTN_WIDTH), BF16)
    norms = jax.ShapeDtypeStruct((n // PROJ_SUB_ROWS, ATTN_HEADS), F32)
    weights = (w_in, w_attn_o, w_conv_o, w_out)
    assert all(w.shape[0] % (steps * HALO_ROWS) == 0 for w in weights)
    slab = lambda w: pl.BlockSpec((w.shape[0] // steps, w.shape[1]), lambda i: (i, 0))
    return pl.pallas_call(
        _qkv_proj_kernel,
        grid=(steps,),
        in_specs=[
            pl.BlockSpec(memory_space=pltpu.SMEM),
            row_block(D_MODEL),
            pl.BlockSpec((1, D_MODEL), lambda i: (0, 0)),
            pl.BlockSpec((D_MODEL, QKV_COLS), lambda i: (0, 0), pipeline_mode=pl.Buffered(1)),
        ] + [slab(w) for w in weights],
        out_specs=[row_block(ATTN_WIDTH), row_block(ATTN_WIDTH), row_block(ATTN_WIDTH),
                   row_block(ATTN_WIDTH), pl.BlockSpec(memory_space=pltpu.SMEM),
                   pl.BlockSpec(memory_space=pltpu.SMEM)]
        + [slab(w) for w in weights],
        out_shape=[act, act, act, act, norms, norms]
        + [jax.ShapeDtypeStruct(w.shape, BF16) for w in weights],
        scratch_shapes=[pltpu.VMEM((PROJ_ROWS, D_MODEL), BF16),
                        pltpu.VMEM((ATTN_HEADS, PROJ_ROWS, HEAD_COLS), F32),
                        pltpu.VMEM((ATTN_HEADS, PROJ_ROWS, HEAD_COLS), F32),
                        pltpu.VMEM((D_MODEL, QKV_COLS), BF16)],
        compiler_params=pltpu.CompilerParams(
            dimension_semantics=("arbitrary",), vmem_limit_bytes=VMEM_LIMIT_BYTES),
        name="qkv_proj",
    )(slopes, x2d, g_pre, w_in, w_in, w_attn_o, w_conv_o, w_out)


def _diff_attn_kernel(slopes_ref, qnorm_ref, knorm_ref, lq1_ref, lk1_ref, lq2_ref, lk2_ref,
                      gain_ref, q_ref, ka_ref, kb_ref, v_ref, o_ref, m_ref, acc_ref,
                      *, lam_init):
    tq, tk, ts = ATTN_Q_ROWS, ATTN_K_ROWS, ATTN_SUB_ROWS
    th = PROJ_SUB_ROWS
    batch = pl.program_id(0)
    head = pl.program_id(1)
    qi = pl.program_id(2)
    slope = slopes_ref[head] * LOG2_E
    seq_norm_row = batch * (ka_ref.shape[0] // th)

    q = q_ref[...].astype(F32)
    qlane = lax.broadcasted_iota(jnp.int32, q.shape, 1)
    q_maps = (
        jnp.where(qlane < HEAD_DIM, q, jnp.where(qlane < HEAD_DIM + 3, 1.0, 0.0)).astype(BF16),
        jnp.where(qlane >= HEAD_DIM, q, jnp.where(qlane < 3, 1.0, 0.0)).astype(BF16),
    )
    key_refs = (ka_ref, kb_ref)

    tri_keep = (lax.broadcasted_iota(jnp.int32, (ts, ts), 1)
                <= lax.broadcasted_iota(jnp.int32, (ts, ts), 0))

    def step(key_start, diag_cols, row_stop=tq):
        span_bias = (key_start - qi * tq).astype(F32) * slope
        first = diag_cols is not None
        streams = [(r, mp)
                   for r in (range(tq - ts, -ts, -ts) if first else range(0, row_stop, ts))
                   for mp in range(2)]

        def width(r):
            return tk if diag_cols is None else diag_cols + r + ts

        def scores(r, mp):
            keys = key_refs[mp][pl.ds(key_start, width(r)), :]
            s = lax.dot_general(q_maps[mp][r:r + ts], keys, (((1,), (1,)), ((), ())),
                                preferred_element_type=F32)
            if diag_cols is not None:
                below = width(r) - ts
                blocks = [s[:, :below]] if below else []
                s = jnp.concatenate(blocks + [jnp.where(tri_keep, s[:, below:], MASK_VALUE)],
                                    axis=1)
            return s

        def update(r, mp, s):
            m_cur = jnp.max(s, axis=-1, keepdims=True) + span_bias
            if first:
                m_new = jnp.broadcast_to(m_cur, (ts, HEAD_COLS))
            else:
                m_prev = m_ref[mp, r:r + ts]
                m_new = jnp.maximum(m_prev, m_cur)
            shift = m_new - span_bias
            p = jnp.exp2(s - jnp.concatenate([shift] * (width(r) // HEAD_COLS), axis=1))
            v_ones = jnp.concatenate([v_ref[pl.ds(key_start, width(r)), :],
                                      jnp.ones((width(r), HEAD_COLS), BF16)], axis=1)
            pv = jnp.dot(p.astype(BF16), v_ones, preferred_element_type=F32)
            if not first:
                alpha = jnp.exp2(m_prev - m_new)
                pv = jnp.concatenate([alpha, alpha], axis=1) * acc_ref[mp, r:r + ts] + pv
            acc_ref[mp, r:r + ts] = pv
            m_ref[mp, r:r + ts] = m_new

        pending = [scores(*st) for st in streams[:ATTN_LOOKAHEAD]]
        for i, st in enumerate(streams):
            if i + ATTN_LOOKAHEAD < len(streams):
                pending.append(scores(*streams[i + ATTN_LOOKAHEAD]))
            update(*st, pending.pop(0))

    n_spans = lax.shift_right_logical(qi * tq, TK_SHIFT)
    tail_start = pl.multiple_of(n_spans * tk, tk)
    for extra in range(0, tk, tq):
        @pl.when(qi * tq - tail_start == extra)
        def _diag_step(extra=extra):
            step(tail_start, extra)

    def row_block_bound(hf):
        m_low = jnp.minimum(m_ref[0, hf * th:(hf + 1) * th], m_ref[1, hf * th:(hf + 1) * th])
        m_low = jnp.min(jnp.min(m_low, axis=0, keepdims=True), axis=1, keepdims=True)[0, 0]
        q_norm = qnorm_ref[seq_norm_row + qi * (tq // th) + hf, head]
        return q_norm, m_low - ALIBI_SKIP_BITS

    row_blocks = [row_block_bound(hf) for hf in range(tq // th)]

    def full_span(kj, carry):
        key_start = pl.multiple_of(kj * tk, tk)
        k_norm = knorm_ref[seq_norm_row + kj * (tk // th), head]
        for part in range(1, tk // th):
            k_norm = jnp.maximum(k_norm, knorm_ref[seq_norm_row + kj * (tk // th) + part, head])
        alibi_max = (key_start + (tk - 1) - qi * tq).astype(F32) * slope
        live = [q_norm * k_norm + alibi_max > floor for q_norm, floor in row_blocks]
        rows_needed = jnp.int32(0)
        for hf, is_live in enumerate(live):
            rows_needed = jnp.where(is_live, (hf + 1) * th, rows_needed)
        for hf in range(tq // th):
            @pl.when(rows_needed == (hf + 1) * th)
            def _fold_span(hf=hf):
                step(key_start, None, row_stop=(hf + 1) * th)

        return carry

    lax.fori_loop(0, n_spans, full_span, 0)

    lam = (jnp.exp(jnp.sum(lq1_ref[...] * lk1_ref[...], axis=-1, keepdims=True))
           - jnp.exp(jnp.sum(lq2_ref[...] * lk2_ref[...], axis=-1, keepdims=True))
           + lam_init)
    o = (acc_ref[0, :, 0:HEAD_COLS] / acc_ref[0, :, HEAD_COLS:2 * HEAD_COLS]
         - lam * (acc_ref[1, :, 0:HEAD_COLS] / acc_ref[1, :, HEAD_COLS:2 * HEAD_COLS]))
    ms = jnp.mean(o * o, axis=-1, keepdims=True)
    o = (o * lax.rsqrt(ms + NORM_EPS) * gain_ref[...]) * (1.0 - lam_init)
    o_ref[...] = o.astype(BF16)


def _diff_attn(q, ka, kb, v, slopes, qnorm, knorm, lq1, lk1, lq2, lk2, gain,
               *, batch, seq, lam_init):
    nq = seq // ATTN_Q_ROWS
    small = lambda cols: pl.BlockSpec((1, cols), lambda b, h, i: (0, 0))
    seq_block = pl.BlockSpec((seq, HEAD_COLS), lambda b, h, i: (b, h))
    q_block = pl.BlockSpec((ATTN_Q_ROWS, HEAD_COLS), lambda b, h, i: (b * nq + i, h))
    return pl.pallas_call(
        functools.partial(_diff_attn_kernel, lam_init=lam_init),
        grid=(batch, ATTN_HEADS, nq),
        in_specs=[
            pl.BlockSpec(memory_space=pltpu.SMEM), pl.BlockSpec(memory_space=pltpu.SMEM),
            pl.BlockSpec(memory_space=pltpu.SMEM),
            small(HEAD_DIM), small(HEAD_DIM), small(HEAD_DIM), small(HEAD_DIM),
            small(HEAD_COLS),
            q_block, seq_block, seq_block, seq_block,
        ],
        out_specs=q_block,
        out_shape=jax.ShapeDtypeStruct((batch * seq, ATTN_WIDTH), BF16),
        scratch_shapes=[pltpu.VMEM((2, ATTN_Q_ROWS, HEAD_COLS), F32),
                        pltpu.VMEM((2, ATTN_Q_ROWS, 2 * HEAD_COLS), F32)],
        compiler_params=pltpu.CompilerParams(
            dimension_semantics=("arbitrary", "arbitrary", "arbitrary"),
            vmem_limit_bytes=VMEM_LIMIT_BYTES),
        name="diff_attn",
    )(slopes, qnorm, knorm, lq1, lk1, lq2, lk2, gain, q, ka, kb, v)


def _sigmoid(z):
    return 1.0 / (1.0 + jnp.exp(-z))


def _out_proj_kernel(x_ref, o_ref, gpre_ref, wmid_ref, whi_ref, cw_ref, wa_ref, wc_ref, bm_ref,
                     wo_ref, gp_ref, out_ref, h_ref, u_ref, *, tiles_per_seq):
    tm, ts, cwid = OUT_ROWS, OUT_SUB_ROWS, CONV_WIDTH

    @pl.when(pl.program_id(0) % tiles_per_seq == 0)
    def _sequence_start():
        u_ref[0:HALO_ROWS, :] = jnp.zeros((HALO_ROWS, cwid), F32)

    for r in range(0, tm, ts):
        x = x_ref[r:r + ts, :]
        h_ref[r:r + ts, :] = _rms_norm_bf16(x, gpre_ref[...])

        def proj(w_ref, col, cols, r=r):
            return jnp.dot(h_ref[r:r + ts, :], w_ref[:, col:col + cols],
                           preferred_element_type=F32)

        z_a = proj(wmid_ref, 0, ATTN_WIDTH)
        ya_in = o_ref[r:r + ts, :].astype(F32) * (z_a * _sigmoid(z_a))
        y_attn = jnp.dot(ya_in.astype(BF16), wa_ref[...], preferred_element_type=F32)

        u = proj(wmid_ref, ATTN_WIDTH + cwid, cwid) * proj(whi_ref, 0, cwid)
        u_ref[HALO_ROWS + r:HALO_ROWS + r + ts, :] = u
        conv = cw_ref[CONV_K - 1:CONV_K, :] * u
        for k in range(CONV_K - 1):
            first_row = HALO_ROWS + r - (CONV_K - 1 - k)
            conv = conv + cw_ref[k:k + 1, :] * u_ref[first_row:first_row + ts, :]
        z_c = proj(whi_ref, cwid, cwid)
        yc_in = (proj(wmid_ref, ATTN_WIDTH, cwid) * conv) * (z_c * _sigmoid(z_c))
        y_conv = jnp.dot(yc_in.astype(BF16), wc_ref[...], preferred_element_type=F32)

        g_a = _sigmoid(proj(whi_ref, 2 * cwid, D_MODEL) + bm_ref[:, 0:D_MODEL])
        g_c = _sigmoid(proj(whi_ref, 2 * cwid + D_MODEL, D_MODEL)
                       + bm_ref[:, D_MODEL:2 * D_MODEL])
        y = g_a * y_attn + g_c * y_conv
        out = jnp.dot(y.astype(BF16), wo_ref[...], preferred_element_type=F32)
        ms = jnp.mean(out * out, axis=-1, keepdims=True)
        out_ref[r:r + ts, :] = x + out * lax.rsqrt(ms + NORM_EPS) * gp_ref[...]

    u_ref[0:HALO_ROWS, :] = u_ref[tm:tm + HALO_ROWS, :]


def _out_proj(x2d, o, g_pre, w_in_bf16, conv_w, wa, wc, b_merge, wo, g_post, *, seq):
    n = x2d.shape[0]
    mid_cols = ATTN_WIDTH + 2 * CONV_WIDTH
    hi_cols = 2 * CONV_WIDTH + GATE_COLS
    assert QKV_COLS == mid_cols and QKV_COLS + mid_cols == hi_cols == IN_COLS - hi_cols
    row_block = lambda cols: pl.BlockSpec((OUT_ROWS, cols), lambda i: (i, 0))
    const = lambda shape, col_block=0: pl.BlockSpec(shape, lambda i: (0, col_block),
                                                   pipeline_mode=pl.Buffered(1))
    return pl.pallas_call(
        functools.partial(_out_proj_kernel, tiles_per_seq=seq // OUT_ROWS),
        grid=(n // OUT_ROWS,),
        in_specs=[
            row_block(D_MODEL), row_block(ATTN_WIDTH), const((1, D_MODEL)),
            const((D_MODEL, mid_cols), 1), const((D_MODEL, hi_cols), 1),
            const((CONV_K, CONV_WIDTH)), const((ATTN_WIDTH, D_MODEL)),
            const((CONV_WIDTH, D_MODEL)), const((1, 2 * D_MODEL)), const((D_MODEL, D_MODEL)),
            const((1, D_MODEL)),
        ],
        out_specs=row_block(D_MODEL),
        out_shape=jax.ShapeDtypeStruct((n, D_MODEL), F32),
        scratch_shapes=[pltpu.VMEM((OUT_ROWS, D_MODEL), BF16),
                        pltpu.VMEM((HALO_ROWS + OUT_ROWS, CONV_WIDTH), F32)],
        compiler_params=pltpu.CompilerParams(
            dimension_semantics=("arbitrary",), vmem_limit_bytes=VMEM_LIMIT_BYTES),
        name="out_proj",
    )(x2d, o, g_pre, w_in_bf16, w_in_bf16, conv_w, wa, wc, b_merge, wo, g_post)


def _layer(x, layer_idx, w_in, lq1, lk1, lq2, lk2, subln_gain, conv_w, w_attn_o, w_conv_o,
           b_merge, w_out, g_pre, g_post):
    batch, seq, d = x.shape
    x2d = x.reshape(batch * seq, d)
    row = lambda a: a.reshape(1, -1).astype(F32)
    slopes = jnp.asarray([2.0 ** (-8.0 * (i + 1) / ATTN_HEADS) for i in range(ATTN_HEADS)], F32)

    q, ka, kb, v, qnorm, knorm, w_in_bf16, wa_bf16, wc_bf16, wo_bf16 = _qkv_proj(
        slopes, x2d, row(g_pre), w_in, w_attn_o, w_conv_o, w_out)
    o = _diff_attn(q, ka, kb, v, slopes, qnorm, knorm, row(lq1), row(lk1), row(lq2), row(lk2),
                   row(subln_gain), batch=batch, seq=seq, lam_init=_lambda_init(layer_idx))
    out = _out_proj(x2d, o, row(g_pre), w_in_bf16, conv_w.astype(F32), wa_bf16, wc_bf16,
                    row(b_merge), wo_bf16, row(g_post), seq=seq)
    return out.reshape(batch, seq, d)


def kernel(x, w_in, lambda_q1, lambda_k1, lambda_q2, lambda_k2, subln_gain, conv_w, w_attn_o,
           w_conv_o, b_merge, w_out, g_pre, g_post):
    for l in range(w_in.shape[0]):
        x = _layer(x, l, w_in[l], lambda_q1[l], lambda_k1[l], lambda_q2[l], lambda_k2[l],
                   subln_gain[l], conv_w[l], w_attn_o[l], w_conv_o[l], b_merge[l], w_out[l],
                   g_pre[l], g_post[l])
    return x
```

```python
import functools
import math

import jax
import jax.numpy as jnp
from jax import lax
from jax.experimental import pallas as pl
from jax.experimental.pallas import tpu as pltpu

D_MODEL = 1024
ATTN_HEADS = 4
HEAD_DIM = 64
HEAD_COLS = 2 * HEAD_DIM
ATTN_WIDTH = ATTN_HEADS * HEAD_COLS
CONV_WIDTH = D_MODEL // 2
CONV_K = 3
NORM_EPS = 1e-6
QKV_COLS = 3 * ATTN_WIDTH
CONV_COLS = 4 * CONV_WIDTH
GATE_COLS = 2 * D_MODEL
IN_COLS = QKV_COLS + ATTN_WIDTH + CONV_COLS + GATE_COLS
MASK_VALUE = -1e30
LOG2_E = math.log2(math.e)
Q_SCALE_LOG2 = HEAD_DIM ** -0.5 * LOG2_E
ALIBI_SKIP_BITS = 160.0
NORM_SLACK = 1.01

PROJ_ROWS = 1024
PROJ_SUB_ROWS = 512
ATTN_Q_ROWS = 1024
ATTN_K_ROWS = 1024
TK_SHIFT = ATTN_K_ROWS.bit_length() - 1
ATTN_SUB_ROWS = 256
ATTN_LOOKAHEAD = 3
assert ATTN_K_ROWS == 1 << TK_SHIFT and ATTN_K_ROWS % ATTN_Q_ROWS == 0
assert PROJ_ROWS == ATTN_K_ROWS
assert ATTN_Q_ROWS % PROJ_SUB_ROWS == 0 and PROJ_SUB_ROWS % ATTN_SUB_ROWS == 0
OUT_ROWS = 1024
OUT_SUB_ROWS = 512
HALO_ROWS = 16
VMEM_LIMIT_BYTES = 56 * 1024 * 1024

BF16 = jnp.bfloat16
F32 = jnp.float32


def _lambda_init(layer_idx):
    return 0.8 - 0.6 * math.exp(-0.3 * layer_idx)


def _rms_norm_bf16(x, gain):
    ms = jnp.mean(x * x, axis=-1, keepdims=True)
    return (x * lax.rsqrt(ms + NORM_EPS) * gain).astype(BF16)


def _max_map_norm(x, lane):
    sq = x * x
    n0 = jnp.sum(jnp.where(lane < HEAD_DIM, sq, 0.0), axis=-1, keepdims=True)
    n1 = jnp.sum(jnp.where(lane >= HEAD_DIM, sq, 0.0), axis=-1, keepdims=True)
    return jnp.sqrt(jnp.max(jnp.maximum(n0, n1), axis=0, keepdims=True))[0, 0] * NORM_SLACK


def _qkv_proj_kernel(slopes_ref, x_ref, g_ref, wqkv_ref, win_ref, wa_ref, wc_ref, wo_ref,
                     q_ref, ka_ref, kb_ref, v_ref, qnorm_ref, knorm_ref, win16_ref, wa16_ref,
                     wc16_ref, wo16_ref, h_ref, biasa_ref, biasb_ref, w_ref):
    for w32_ref, w16_ref in ((win_ref, win16_ref), (wa_ref, wa16_ref), (wc_ref, wc16_ref),
                             (wo_ref, wo16_ref)):
        w16_ref[...] = w32_ref[...].astype(BF16)

    @pl.when(pl.program_id(0) == 0)
    def _first_step():
        w_ref[...] = wqkv_ref[...].astype(BF16)
        lane = lax.broadcasted_iota(jnp.int32, (PROJ_ROWS, HEAD_COLS), 1)
        row = lax.broadcasted_iota(jnp.int32, (PROJ_ROWS, HEAD_COLS), 0).astype(F32)
        for hd in range(ATTN_HEADS):
            b = row * (slopes_ref[hd] * LOG2_E)
            b1 = b.astype(BF16).astype(F32)
            b2 = (b - b1).astype(BF16).astype(F32)
            b3 = (b - b1 - b2).astype(BF16).astype(F32)
            for bias_ref, first in ((biasa_ref, HEAD_DIM), (biasb_ref, 0)):
                bias_ref[hd] = jnp.where(lane == first, b1, jnp.where(
                    lane == first + 1, b2, jnp.where(lane == first + 2, b3, 0.0)))

    lane = lax.broadcasted_iota(jnp.int32, (PROJ_SUB_ROWS, HEAD_COLS), 1)
    for r in range(0, PROJ_ROWS, PROJ_SUB_ROWS):
        rows = slice(r, r + PROJ_SUB_ROWS)
        norm_row = pl.program_id(0) * (PROJ_ROWS // PROJ_SUB_ROWS) + r // PROJ_SUB_ROWS
        h_ref[rows, :] = _rms_norm_bf16(x_ref[rows, :], g_ref[...])

        def proj(col, rows=rows):
            return jnp.dot(h_ref[rows, :], w_ref[:, col:col + ATTN_WIDTH],
                           preferred_element_type=F32)

        q = proj(0) * Q_SCALE_LOG2
        q_ref[rows, :] = q.astype(BF16)
        k = proj(ATTN_WIDTH)
        for hd in range(ATTN_HEADS):
            cols = slice(hd * HEAD_COLS, (hd + 1) * HEAD_COLS)
            kh = k[:, cols]
            ka_ref[rows, cols] = jnp.where(lane < HEAD_DIM, kh, biasa_ref[hd, rows, :]).astype(BF16)
            kb_ref[rows, cols] = jnp.where(lane >= HEAD_DIM, kh, biasb_ref[hd, rows, :]).astype(BF16)
            qnorm_ref[norm_row, hd] = _max_map_norm(q[:, cols], lane)
            knorm_ref[norm_row, hd] = _max_map_norm(kh, lane)
        v_ref[rows, :] = proj(2 * ATTN_WIDTH).astype(BF16)


def _qkv_proj(slopes, x2d, g_pre, w_in, w_attn_o, w_conv_o, w_out):
    n = x2d.shape[0]
    steps = n // PROJ_ROWS
    row_block = lambda cols: pl.BlockSpec((PROJ_ROWS, cols), lambda i: (i, 0))
    act = jax.ShapeDtypeStruct((n, ATTN_WIDTH), BF16)
    norms = jax.ShapeDtypeStruct((n // PROJ_SUB_ROWS, ATTN_HEADS), F32)
    weights = (w_in, w_attn_o, w_conv_o, w_out)
    assert all(w.shape[0] % (steps * HALO_ROWS) == 0 for w in weights)
    slab = lambda w: pl.BlockSpec((w.shape[0] // steps, w.shape[1]), lambda i: (i, 0))
    return pl.pallas_call(
        _qkv_proj_kernel,
        grid=(steps,),
        in_specs=[
            pl.BlockSpec(memory_space=pltpu.SMEM),
            row_block(D_MODEL),
            pl.BlockSpec((1, D_MODEL), lambda i: (0, 0)),
            pl.BlockSpec((D_MODEL, QKV_COLS), lambda i: (0, 0), pipeline_mode=pl.Buffered(1)),
        ] + [slab(w) for w in weights],
        out_specs=[row_block(ATTN_WIDTH), row_block(ATTN_WIDTH), row_block(ATTN_WIDTH),
                   row_block(ATTN_WIDTH), pl.BlockSpec(memory_space=pltpu.SMEM),
                   pl.BlockSpec(memory_space=pltpu.SMEM)]
        + [slab(w) for w in weights],
        out_shape=[act, act, act, act, norms, norms]
        + [jax.ShapeDtypeStruct(w.shape, BF16) for w in weights],
        scratch_shapes=[pltpu.VMEM((PROJ_ROWS, D_MODEL), BF16),
                        pltpu.VMEM((ATTN_HEADS, PROJ_ROWS, HEAD_COLS), F32),
                        pltpu.VMEM((ATTN_HEADS, PROJ_ROWS, HEAD_COLS), F32),
                        pltpu.VMEM((D_MODEL, QKV_COLS), BF16)],
        compiler_params=pltpu.CompilerParams(
            dimension_semantics=("arbitrary",), vmem_limit_bytes=VMEM_LIMIT_BYTES),
        name="qkv_proj",
    )(slopes, x2d, g_pre, w_in, w_in, w_attn_o, w_conv_o, w_out)


def _diff_attn_kernel(slopes_ref, qnorm_ref, knorm_ref, lq1_ref, lk1_ref, lq2_ref, lk2_ref,
                      gain_ref, q_ref, ka_ref, kb_ref, v_ref, o_ref, m_ref, acc_ref,
                      *, lam_init):
    tq, tk, ts = ATTN_Q_ROWS, ATTN_K_ROWS, ATTN_SUB_ROWS
    th = PROJ_SUB_ROWS
    batch = pl.program_id(0)
    head = pl.program_id(1)
    qi = pl.program_id(2)
    slope = slopes_ref[head] * LOG2_E
    seq_norm_row = batch * (ka_ref.shape[0] // th)

    q = q_ref[...].astype(F32)
    qlane = lax.broadcasted_iota(jnp.int32, q.shape, 1)
    q_maps = (
        jnp.where(qlane < HEAD_DIM, q, jnp.where(qlane < HEAD_DIM + 3, 1.0, 0.0)).astype(BF16),
        jnp.where(qlane >= HEAD_DIM, q, jnp.where(qlane < 3, 1.0, 0.0)).astype(BF16),
    )
    key_refs = (ka_ref, kb_ref)

    tri_keep = (lax.broadcasted_iota(jnp.int32, (ts, ts), 1)
                <= lax.broadcasted_iota(jnp.int32, (ts, ts), 0))

    def step(key_start, diag_cols, row_stop=tq):
        span_bias = (key_start - qi * tq).astype(F32) * slope
        first = diag_cols is not None
        streams = [(r, mp)
                   for r in (range(tq - ts, -ts, -ts) if first else range(0, row_stop, ts))
                   for mp in range(2)]

        def width(r):
            return tk if diag_cols is None else diag_cols + r + ts

        def scores(r, mp):
            keys = key_refs[mp][pl.ds(key_start, width(r)), :]
            s = lax.dot_general(q_maps[mp][r:r + ts], keys, (((1,), (1,)), ((), ())),
                                preferred_element_type=F32)
            if diag_cols is not None:
                below = width(r) - ts
                blocks = [s[:, :below]] if below else []
                s = jnp.concatenate(blocks + [jnp.where(tri_keep, s[:, below:], MASK_VALUE)],
                                    axis=1)
            return s

        def update(r, mp, s):
            m_cur = jnp.max(s, axis=-1, keepdims=True) + span_bias
            if first:
                m_new = jnp.broadcast_to(m_cur, (ts, HEAD_COLS))
            else:
                m_prev = m_ref[mp, r:r + ts]
                m_new = jnp.maximum(m_prev, m_cur)
            shift = m_new - span_bias
            p = jnp.exp2(s - jnp.concatenate([shift] * (width(r) // HEAD_COLS), axis=1))
            v_ones = jnp.concatenate([v_ref[pl.ds(key_start, width(r)), :],
                                      jnp.ones((width(r), HEAD_COLS), BF16)], axis=1)
            pv = jnp.dot(p.astype(BF16), v_ones, preferred_element_type=F32)
            if not first:
                alpha = jnp.exp2(m_prev - m_new)
                pv = jnp.concatenate([alpha, alpha], axis=1) * acc_ref[mp, r:r + ts] + pv
            acc_ref[mp, r:r + ts] = pv
            m_ref[mp, r:r + ts] = m_new

        pending = [scores(*st) for st in streams[:ATTN_LOOKAHEAD]]
        for i, st in enumerate(streams):
            if i + ATTN_LOOKAHEAD < len(streams):
                pending.append(scores(*streams[i + ATTN_LOOKAHEAD]))
            update(*st, pending.pop(0))

    n_spans = lax.shift_right_logical(qi * tq, TK_SHIFT)
    tail_start = pl.multiple_of(n_spans * tk, tk)
    for extra in range(0, tk, tq):
        @pl.when(qi * tq - tail_start == extra)
        def _diag_step(extra=extra):
            step(tail_start, extra)

    def row_block_bound(hf):
        m_low = jnp.minimum(m_ref[0, hf * th:(hf + 1) * th], m_ref[1, hf * th:(hf + 1) * th])
        m_low = jnp.min(jnp.min(m_low, axis=0, keepdims=True), axis=1, keepdims=True)[0, 0]
        q_norm = qnorm_ref[seq_norm_row + qi * (tq // th) + hf, head]
        return q_norm, m_low - ALIBI_SKIP_BITS

    row_blocks = [row_block_bound(hf) for hf in range(tq // th)]

    def full_span(kj, carry):
        key_start = pl.multiple_of(kj * tk, tk)
        k_norm = knorm_ref[seq_norm_row + kj * (tk // th), head]
        for part in range(1, tk // th):
            k_norm = jnp.maximum(k_norm, knorm_ref[seq_norm_row + kj * (tk // th) + part, head])
        alibi_max = (key_start + (tk - 1) - qi * tq).astype(F32) * slope
        live = [q_norm * k_norm + alibi_max > floor for q_norm, floor in row_blocks]
        rows_needed = jnp.int32(0)
        for hf, is_live in enumerate(live):
            rows_needed = jnp.where(is_live, (hf + 1) * th, rows_needed)
        for hf in range(tq // th):
            @pl.when(rows_needed == (hf + 1) * th)
            def _fold_span(hf=hf):
                step(key_start, None, row_stop=(hf + 1) * th)

        return carry

    lax.fori_loop(0, n_spans, full_span, 0)

    lam = (jnp.exp(jnp.sum(lq1_ref[...] * lk1_ref[...], axis=-1, keepdims=True))
           - jnp.exp(jnp.sum(lq2_ref[...] * lk2_ref[...], axis=-1, keepdims=True))
           + lam_init)
    o = (acc_ref[0, :, 0:HEAD_COLS] / acc_ref[0, :, HEAD_COLS:2 * HEAD_COLS]
         - lam * (acc_ref[1, :, 0:HEAD_COLS] / acc_ref[1, :, HEAD_COLS:2 * HEAD_COLS]))
    ms = jnp.mean(o * o, axis=-1, keepdims=True)
    o = (o * lax.rsqrt(ms + NORM_EPS) * gain_ref[...]) * (1.0 - lam_init)
    o_ref[...] = o.astype(BF16)


def _diff_attn(q, ka, kb, v, slopes, qnorm, knorm, lq1, lk1, lq2, lk2, gain,
               *, batch, seq, lam_init):
    nq = seq // ATTN_Q_ROWS
    small = lambda cols: pl.BlockSpec((1, cols), lambda b, h, i: (0, 0))
    seq_block = pl.BlockSpec((seq, HEAD_COLS), lambda b, h, i: (b, h))
    q_block = pl.BlockSpec((ATTN_Q_ROWS, HEAD_COLS), lambda b, h, i: (b * nq + i, h))
    return pl.pallas_call(
        functools.partial(_diff_attn_kernel, lam_init=lam_init),
        grid=(batch, ATTN_HEADS, nq),
        in_specs=[
            pl.BlockSpec(memory_space=pltpu.SMEM), pl.BlockSpec(memory_space=pltpu.SMEM),
            pl.BlockSpec(memory_space=pltpu.SMEM),
            small(HEAD_DIM), small(HEAD_DIM), small(HEAD_DIM), small(HEAD_DIM),
            small(HEAD_COLS),
            q_block, seq_block, seq_block, seq_block,
        ],
        out_specs=q_block,
        out_shape=jax.ShapeDtypeStruct((batch * seq, ATTN_WIDTH), BF16),
        scratch_shapes=[pltpu.VMEM((2, ATTN_Q_ROWS, HEAD_COLS), F32),
                        pltpu.VMEM((2, ATTN_Q_ROWS, 2 * HEAD_COLS), F32)],
        compiler_params=pltpu.CompilerParams(
            dimension_semantics=("arbitrary", "arbitrary", "arbitrary"),
            vmem_limit_bytes=VMEM_LIMIT_BYTES),
        name="diff_attn",
    )(slopes, qnorm, knorm, lq1, lk1, lq2, lk2, gain, q, ka, kb, v)


def _sigmoid(z):
    return 1.0 / (1.0 + jnp.exp(-z))


def _out_proj_kernel(x_ref, o_ref, gpre_ref, wmid_ref, whi_ref, cw_ref, wa_ref, wc_ref, bm_ref,
                     wo_ref, gp_ref, out_ref, h_ref, u_ref, *, tiles_per_seq):
    tm, ts, cwid = OUT_ROWS, OUT_SUB_ROWS, CONV_WIDTH

    @pl.when(pl.program_id(0) % tiles_per_seq == 0)
    def _sequence_start():
        u_ref[0:HALO_ROWS, :] = jnp.zeros((HALO_ROWS, cwid), F32)

    for r in range(0, tm, ts):
        x = x_ref[r:r + ts, :]
        h_ref[r:r + ts, :] = _rms_norm_bf16(x, gpre_ref[...])

        def proj(w_ref, col, cols, r=r):
            return jnp.dot(h_ref[r:r + ts, :], w_ref[:, col:col + cols],
                           preferred_element_type=F32)

        z_a = proj(wmid_ref, 0, ATTN_WIDTH)
        ya_in = o_ref[r:r + ts, :].astype(F32) * (z_a * _sigmoid(z_a))
        y_attn = jnp.dot(ya_in.astype(BF16), wa_ref[...], preferred_element_type=F32)

        u = proj(wmid_ref, ATTN_WIDTH + cwid, cwid) * proj(whi_ref, 0, cwid)
        u_ref[HALO_ROWS + r:HALO_ROWS + r + ts, :] = u
        conv = cw_ref[CONV_K - 1:CONV_K, :] * u
        for k in range(CONV_K - 1):
            first_row = HALO_ROWS + r - (CONV_K - 1 - k)
            conv = conv + cw_ref[k:k + 1, :] * u_ref[first_row:first_row + ts, :]
        z_c = proj(whi_ref, cwid, cwid)
        yc_in = (proj(wmid_ref, ATTN_WIDTH, cwid) * conv) * (z_c * _sigmoid(z_c))
        y_conv = jnp.dot(yc_in.astype(BF16), wc_ref[...], preferred_element_type=F32)

        g_a = _sigmoid(proj(whi_ref, 2 * cwid, D_MODEL) + bm_ref[:, 0:D_MODEL])
        g_c = _sigmoid(proj(whi_ref, 2 * cwid + D_MODEL, D_MODEL)
                       + bm_ref[:, D_MODEL:2 * D_MODEL])
        y = g_a * y_attn + g_c * y_conv
        out = jnp.dot(y.astype(BF16), wo_ref[...], preferred_element_type=F32)
        ms = jnp.mean(out * out, axis=-1, keepdims=True)
        out_ref[r:r + ts, :] = x + out * lax.rsqrt(ms + NORM_EPS) * gp_ref[...]

    u_ref[0:HALO_ROWS, :] = u_ref[tm:tm + HALO_ROWS, :]


def _out_proj(x2d, o, g_pre, w_in_bf16, conv_w, wa, wc, b_merge, wo, g_post, *, seq):
    n = x2d.shape[0]
    mid_cols = ATTN_WIDTH + 2 * CONV_WIDTH
    hi_cols = 2 * CONV_WIDTH + GATE_COLS
    assert QKV_COLS == mid_cols and QKV_COLS + mid_cols == hi_cols == IN_COLS - hi_cols
    row_block = lambda cols: pl.BlockSpec((OUT_ROWS, cols), lambda i: (i, 0))
    const = lambda shape, col_block=0: pl.BlockSpec(shape, lambda i: (0, col_block),
                                                   pipeline_mode=pl.Buffered(1))
    return pl.pallas_call(
        functools.partial(_out_proj_kernel, tiles_per_seq=seq // OUT_ROWS),
        grid=(n // OUT_ROWS,),
        in_specs=[
            row_block(D_MODEL), row_block(ATTN_WIDTH), const((1, D_MODEL)),
            const((D_MODEL, mid_cols), 1), const((D_MODEL, hi_cols), 1),
            const((CONV_K, CONV_WIDTH)), const((ATTN_WIDTH, D_MODEL)),
            const((CONV_WIDTH, D_MODEL)), const((1, 2 * D_MODEL)), const((D_MODEL, D_MODEL)),
            const((1, D_MODEL)),
        ],
        out_specs=row_block(D_MODEL),
        out_shape=jax.ShapeDtypeStruct((n, D_MODEL), F32),
        scratch_shapes=[pltpu.VMEM((OUT_ROWS, D_MODEL), BF16),
                        pltpu.VMEM((HALO_ROWS + OUT_ROWS, CONV_WIDTH), F32)],
        compiler_params=pltpu.CompilerParams(
            dimension_semantics=("arbitrary",), vmem_limit_bytes=VMEM_LIMIT_BYTES),
        name="out_proj",
    )(x2d, o, g_pre, w_in_bf16, w_in_bf16, conv_w, wa, wc, b_merge, wo, g_post)


def _layer(x, layer_idx, w_in, lq1, lk1, lq2, lk2, subln_gain, conv_w, w_attn_o, w_conv_o,
           b_merge, w_out, g_pre, g_post):
    batch, seq, d = x.shape
    x2d = x.reshape(batch * seq, d)
    row = lambda a: a.reshape(1, -1).astype(F32)
    slopes = jnp.asarray([2.0 ** (-8.0 * (i + 1) / ATTN_HEADS) for i in range(ATTN_HEADS)], F32)

    q, ka, kb, v, qnorm, knorm, w_in_bf16, wa_bf16, wc_bf16, wo_bf16 = _qkv_proj(
        slopes, x2d, row(g_pre), w_in, w_attn_o, w_conv_o, w_out)
    o = _diff_attn(q, ka, kb, v, slopes, qnorm, knorm, row(lq1), row(lk1), row(lq2), row(lk2),
                   row(subln_gain), batch=batch, seq=seq, lam_init=_lambda_init(layer_idx))
    out = _out_proj(x2d, o, row(g_pre), w_in_bf16, conv_w.astype(F32), wa_bf16, wc_bf16,
                    row(b_merge), wo_bf16, row(g_post), seq=seq)
    return out.reshape(batch, seq, d)


def kernel(x, w_in, lambda_q1, lambda_k1, lambda_q2, lambda_k2, subln_gain, conv_w, w_attn_o,
           w_conv_o, b_merge, w_out, g_pre, g_post):
    for l in range(w_in.shape[0]):
        x = _layer(x, l, w_in[l], lambda_q1[l], lambda_k1[l], lambda_q2[l], lambda_k2[l],
                   subln_gain[l], conv_w[l], w_attn_o[l], w_conv_o[l], b_merge[l], w_out[l],
                   g_pre[l], g_post[l])
    return x
```

```python
import functools
import math

import jax
import jax.numpy as jnp
from jax import lax
from jax.experimental import pallas as pl
from jax.experimental.pallas import tpu as pltpu

D_MODEL = 1024
ATTN_HEADS = 4
HEAD_DIM = 64
HEAD_COLS = 2 * HEAD_DIM
ATTN_WIDTH = ATTN_HEADS * HEAD_COLS
CONV_WIDTH = D_MODEL // 2
CONV_K = 3
NORM_EPS = 1e-6
QKV_COLS = 3 * ATTN_WIDTH
CONV_COLS = 4 * CONV_WIDTH
GATE_COLS = 2 * D_MODEL
IN_COLS = QKV_COLS + ATTN_WIDTH + CONV_COLS + GATE_COLS
MASK_VALUE = -1e30
LOG2_E = math.log2(math.e)
Q_SCALE_LOG2 = HEAD_DIM ** -0.5 * LOG2_E
ALIBI_SKIP_BITS = 160.0
NORM_SLACK = 1.01

PROJ_ROWS = 1024
PROJ_SUB_ROWS = 512
ATTN_Q_ROWS = 1024
ATTN_K_ROWS = 1024
TK_SHIFT = ATTN_K_ROWS.bit_length() - 1
ATTN_SUB_ROWS = 256
ATTN_LOOKAHEAD = 3
assert ATTN_K_ROWS == 1 << TK_SHIFT and ATTN_K_ROWS % ATTN_Q_ROWS == 0
assert PROJ_ROWS == ATTN_K_ROWS
assert ATTN_Q_ROWS % PROJ_SUB_ROWS == 0 and PROJ_SUB_ROWS % ATTN_SUB_ROWS == 0
OUT_ROWS = 1024
OUT_SUB_ROWS = 512
HALO_ROWS = 16
VMEM_LIMIT_BYTES = 56 * 1024 * 1024

BF16 = jnp.bfloat16
F32 = jnp.float32


def _lambda_init(layer_idx):
    return 0.8 - 0.6 * math.exp(-0.3 * layer_idx)


def _rms_norm_bf16(x, gain):
    ms = jnp.mean(x * x, axis=-1, keepdims=True)
    return (x * lax.rsqrt(ms + NORM_EPS) * gain).astype(BF16)


def _max_map_norm(x, lane):
    sq = x * x
    n0 = jnp.sum(jnp.where(lane < HEAD_DIM, sq, 0.0), axis=-1, keepdims=True)
    n1 = jnp.sum(jnp.where(lane >= HEAD_DIM, sq, 0.0), axis=-1, keepdims=True)
    return jnp.sqrt(jnp.max(jnp.maximum(n0, n1), axis=0, keepdims=True))[0, 0] * NORM_SLACK


def _qkv_proj_kernel(slopes_ref, x_ref, g_ref, wqkv_ref, win_ref, wa_ref, wc_ref, wo_ref,
                     q_ref, ka_ref, kb_ref, v_ref, qnorm_ref, knorm_ref, win16_ref, wa16_ref,
                     wc16_ref, wo16_ref, h_ref, biasa_ref, biasb_ref, w_ref):
    for w32_ref, w16_ref in ((win_ref, win16_ref), (wa_ref, wa16_ref), (wc_ref, wc16_ref),
                             (wo_ref, wo16_ref)):
        w16_ref[...] = w32_ref[...].astype(BF16)

    @pl.when(pl.program_id(0) == 0)
    def _first_step():
        w_ref[...] = wqkv_ref[...].astype(BF16)
        lane = lax.broadcasted_iota(jnp.int32, (PROJ_ROWS, HEAD_COLS), 1)
        row = lax.broadcasted_iota(jnp.int32, (PROJ_ROWS, HEAD_COLS), 0).astype(F32)
        for hd in range(ATTN_HEADS):
            b = row * (slopes_ref[hd] * LOG2_E)
            b1 = b.astype(BF16).astype(F32)
            b2 = (b - b1).astype(BF16).astype(F32)
            b3 = (b - b1 - b2).astype(BF16).astype(F32)
            for bias_ref, first in ((biasa_ref, HEAD_DIM), (biasb_ref, 0)):
                bias_ref[hd] = jnp.where(lane == first, b1, jnp.where(
                    lane == first + 1, b2, jnp.where(lane == first + 2, b3, 0.0)))

    lane = lax.broadcasted_iota(jnp.int32, (PROJ_SUB_ROWS, HEAD_COLS), 1)
    for r in range(0, PROJ_ROWS, PROJ_SUB_ROWS):
        rows = slice(r, r + PROJ_SUB_ROWS)
        norm_row = pl.program_id(0) * (PROJ_ROWS // PROJ_SUB_ROWS) + r // PROJ_SUB_ROWS
        h_ref[rows, :] = _rms_norm_bf16(x_ref[rows, :], g_ref[...])

        def proj(col, rows=rows):
            return jnp.dot(h_ref[rows, :], w_ref[:, col:col + ATTN_WIDTH],
                           preferred_element_type=F32)

        q = proj(0) * Q_SCALE_LOG2
        q_ref[rows, :] = q.astype(BF16)
        k = proj(ATTN_WIDTH)
        for hd in range(ATTN_HEADS):
            cols = slice(hd * HEAD_COLS, (hd + 1) * HEAD_COLS)
            kh = k[:, cols]
            ka_ref[rows, cols] = jnp.where(lane < HEAD_DIM, kh, biasa_ref[hd, rows, :]).astype(BF16)
            kb_ref[rows, cols] = jnp.where(lane >= HEAD_DIM, kh, biasb_ref[hd, rows, :]).astype(BF16)
            qnorm_ref[norm_row, hd] = _max_map_norm(q[:, cols], lane)
            knorm_ref[norm_row, hd] = _max_map_norm(kh, lane)
        v_ref[rows, :] = proj(2 * ATTN_WIDTH).astype(BF16)


def _qkv_proj(slopes, x2d, g_pre, w_in, w_attn_o, w_conv_o, w_out):
    n = x2d.shape[0]
    steps = n // PROJ_ROWS
    row_block = lambda cols: pl.BlockSpec((PROJ_ROWS, cols), lambda i: (i, 0))
    act = jax.ShapeDtypeStruct((n, ATTN_WIDTH), BF16)
    norms = jax.ShapeDtypeStruct((n // PROJ_SUB_ROWS, ATTN_HEADS), F32)
    weights = (w_in, w_attn_o, w_conv_o, w_out)
    assert all(w.shape[0] % (steps * HALO_ROWS) == 0 for w in weights)
    slab = lambda w: pl.BlockSpec((w.shape[0] // steps, w.shape[1]), lambda i: (i, 0))
    return pl.pallas_call(
        _qkv_proj_kernel,
        grid=(steps,),
        in_specs=[
            pl.BlockSpec(memory_space=pltpu.SMEM),
            row_block(D_MODEL),
            pl.BlockSpec((1, D_MODEL), lambda i: (0, 0)),
            pl.BlockSpec((D_MODEL, QKV_COLS), lambda i: (0, 0), pipeline_mode=pl.Buffered(1)),
        ] + [slab(w) for w in weights],
        out_specs=[row_block(ATTN_WIDTH), row_block(ATTN_WIDTH), row_block(ATTN_WIDTH),
                   row_block(ATTN_WIDTH), pl.BlockSpec(memory_space=pltpu.SMEM),
                   pl.BlockSpec(memory_space=pltpu.SMEM)]
        + [slab(w) for w in weights],
        out_shape=[act, act, act, act, norms, norms]
        + [jax.ShapeDtypeStruct(w.shape, BF16) for w in weights],
        scratch_shapes=[pltpu.VMEM((PROJ_ROWS, D_MODEL), BF16),
                        pltpu.VMEM((ATTN_HEADS, PROJ_ROWS, HEAD_COLS), F32),
                        pltpu.VMEM((ATTN_HEADS, PROJ_ROWS, HEAD_COLS), F32),
                        pltpu.VMEM((D_MODEL, QKV_COLS), BF16)],
        compiler_params=pltpu.CompilerParams(
            dimension_semantics=("arbitrary",), vmem_limit_bytes=VMEM_LIMIT_BYTES),
        name="qkv_proj",
    )(slopes, x2d, g_pre, w_in, w_in, w_attn_o, w_conv_o, w_out)


def _diff_attn_kernel(slopes_ref, qnorm_ref, knorm_ref, lq1_ref, lk1_ref, lq2_ref, lk2_ref,
                      gain_ref, q_ref, ka_ref, kb_ref, v_ref, o_ref, m_ref, acc_ref,
                      *, lam_init):
    tq, tk, ts = ATTN_Q_ROWS, ATTN_K_ROWS, ATTN_SUB_ROWS
    th = PROJ_SUB_ROWS
    batch = pl.program_id(0)
    head = pl.program_id(1)
    qi = pl.program_id(2)
    slope = slopes_ref[head] * LOG2_E
    seq_norm_row = batch * (ka_ref.shape[0] // th)

    q = q_ref[...].astype(F32)
    qlane = lax.broadcasted_iota(jnp.int32, q.shape, 1)
    q_maps = (
        jnp.where(qlane < HEAD_DIM, q, jnp.where(qlane < HEAD_DIM + 3, 1.0, 0.0)).astype(BF16),
        jnp.where(qlane >= HEAD_DIM, q, jnp.where(qlane < 3, 1.0, 0.0)).astype(BF16),
    )
    key_refs = (ka_ref, kb_ref)

    tri_keep = (lax.broadcasted_iota(jnp.int32, (ts, ts), 1)
                <= lax.broadcasted_iota(jnp.int32, (ts, ts), 0))

    def step(key_start, diag_cols, row_stop=tq):
        span_bias = (key_start - qi * tq).astype(F32) * slope
        first = diag_cols is not None
        streams = [(r, mp)
                   for r in (range(tq - ts, -ts, -ts) if first else range(0, row_stop, ts))
                   for mp in range(2)]

        def width(r):
            return tk if diag_cols is None else diag_cols + r + ts

        def scores(r, mp):
            keys = key_refs[mp][pl.ds(key_start, width(r)), :]
            s = lax.dot_general(q_maps[mp][r:r + ts], keys, (((1,), (1,)), ((), ())),
                                preferred_element_type=F32)
            if diag_cols is not None:
                below = width(r) - ts
                blocks = [s[:, :below]] if below else []
                s = jnp.concatenate(blocks + [jnp.where(tri_keep, s[:, below:], MASK_VALUE)],
                                    axis=1)
            return s

        def update(r, mp, s):
            m_cur = jnp.max(s, axis=-1, keepdims=True) + span_bias
            if first:
                m_new = jnp.broadcast_to(m_cur, (ts, HEAD_COLS))
            else:
                m_prev = m_ref[mp, r:r + ts]
                m_new = jnp.maximum(m_prev, m_cur)
            shift = m_new - span_bias
            p = jnp.exp2(s - jnp.concatenate([shift] * (width(r) // HEAD_COLS), axis=1))
            v_ones = jnp.concatenate([v_ref[pl.ds(key_start, width(r)), :],
                                      jnp.ones((width(r), HEAD_COLS), BF16)], axis=1)
            pv = jnp.dot(p.astype(BF16), v_ones, preferred_element_type=F32)
            if not first:
                alpha = jnp.exp2(m_prev - m_new)
                pv = jnp.concatenate([alpha, alpha], axis=1) * acc_ref[mp, r:r + ts] + pv
            acc_ref[mp, r:r + ts] = pv
            m_ref[mp, r:r + ts] = m_new

        pending = [scores(*st) for st in streams[:ATTN_LOOKAHEAD]]
        for i, st in enumerate(streams):
            if i + ATTN_LOOKAHEAD < len(streams):
                pending.append(scores(*streams[i + ATTN_LOOKAHEAD]))
            update(*st, pending.pop(0))

    n_spans = lax.shift_right_logical(qi * tq, TK_SHIFT)
    tail_start = pl.multiple_of(n_spans * tk, tk)
    for extra in range(0, tk, tq):
        @pl.when(qi * tq - tail_start == extra)
        def _diag_step(extra=extra):
            step(tail_start, extra)

    def row_block_bound(hf):
        m_low = jnp.minimum(m_ref[0, hf * th:(hf + 1) * th], m_ref[1, hf * th:(hf + 1) * th])
        m_low = jnp.min(jnp.min(m_low, axis=0, keepdims=True), axis=1, keepdims=True)[0, 0]
        q_norm = qnorm_ref[seq_norm_row + qi * (tq // th) + hf, head]
        return q_norm, m_low - ALIBI_SKIP_BITS

    row_blocks = [row_block_bound(hf) for hf in range(tq // th)]

    def full_span(kj, carry):
        key_start = pl.multiple_of(kj * tk, tk)
        k_norm = knorm_ref[seq_norm_row + kj * (tk // th), head]
        for part in range(1, tk // th):
            k_norm = jnp.maximum(k_norm, knorm_ref[seq_norm_row + kj * (tk // th) + part, head])
        alibi_max = (key_start + (tk - 1) - qi * tq).astype(F32) * slope
        live = [q_norm * k_norm + alibi_max > floor for q_norm, floor in row_blocks]
        rows_needed = jnp.int32(0)
        for hf, is_live in enumerate(live):
            rows_needed = jnp.where(is_live, (hf + 1) * th, rows_needed)
        for hf in range(tq // th):
            @pl.when(rows_needed == (hf + 1) * th)
            def _fold_span(hf=hf):
                step(key_start, None, row_stop=(hf + 1) * th)

        return carry

    lax.fori_loop(0, n_spans, full_span, 0)

    lam = (jnp.exp(jnp.sum(lq1_ref[...] * lk1_ref[...], axis=-1, keepdims=True))
           - jnp.exp(jnp.sum(lq2_ref[...] * lk2_ref[...], axis=-1, keepdims=True))
           + lam_init)
    o = (acc_ref[0, :, 0:HEAD_COLS] / acc_ref[0, :, HEAD_COLS:2 * HEAD_COLS]
         - lam * (acc_ref[1, :, 0:HEAD_COLS] / acc_ref[1, :, HEAD_COLS:2 * HEAD_COLS]))
    ms = jnp.mean(o * o, axis=-1, keepdims=True)
    o = (o * lax.rsqrt(ms + NORM_EPS) * gain_ref[...]) * (1.0 - lam_init)
    o_ref[...] = o.astype(BF16)


def _diff_attn(q, ka, kb, v, slopes, qnorm, knorm, lq1, lk1, lq2, lk2, gain,
               *, batch, seq, lam_init):
    nq = seq // ATTN_Q_ROWS
    small = lambda cols: pl.BlockSpec((1, cols), lambda b, h, i: (0, 0))
    seq_block = pl.BlockSpec((seq, HEAD_COLS), lambda b, h, i: (b, h))
    q_block = pl.BlockSpec((ATTN_Q_ROWS, HEAD_COLS), lambda b, h, i: (b * nq + i, h))
    return pl.pallas_call(
        functools.partial(_diff_attn_kernel, lam_init=lam_init),
        grid=(batch, ATTN_HEADS, nq),
        in_specs=[
            pl.BlockSpec(memory_space=pltpu.SMEM), pl.BlockSpec(memory_space=pltpu.SMEM),
            pl.BlockSpec(memory_space=pltpu.SMEM),
            small(HEAD_DIM), small(HEAD_DIM), small(HEAD_DIM), small(HEAD_DIM),
            small(HEAD_COLS),
            q_block, seq_block, seq_block, seq_block,
        ],
        out_specs=q_block,
        out_shape=jax.ShapeDtypeStruct((batch * seq, ATTN_WIDTH), BF16),
        scratch_shapes=[pltpu.VMEM((2, ATTN_Q_ROWS, HEAD_COLS), F32),
                        pltpu.VMEM((2, ATTN_Q_ROWS, 2 * HEAD_COLS), F32)],
        compiler_params=pltpu.CompilerParams(
            dimension_semantics=("arbitrary", "arbitrary", "arbitrary"),
            vmem_limit_bytes=VMEM_LIMIT_BYTES),
        name="diff_attn",
    )(slopes, qnorm, knorm, lq1, lk1, lq2, lk2, gain, q, ka, kb, v)


def _sigmoid(z):
    return 1.0 / (1.0 + jnp.exp(-z))


def _out_proj_kernel(x_ref, o_ref, gpre_ref, wmid_ref, whi_ref, cw_ref, wa_ref, wc_ref, bm_ref,
                     wo_ref, gp_ref, out_ref, h_ref, u_ref, *, tiles_per_seq):
    tm, ts, cwid = OUT_ROWS, OUT_SUB_ROWS, CONV_WIDTH

    @pl.when(pl.program_id(0) % tiles_per_seq == 0)
    def _sequence_start():
        u_ref[0:HALO_ROWS, :] = jnp.zeros((HALO_ROWS, cwid), F32)

    blocks = range(0, tm, ts)

    def proj(r, w_ref, col, cols):
        return jnp.dot(h_ref[r:r + ts, :], w_ref[:, col:col + cols], preferred_element_type=F32)

    for r in blocks:
        h_ref[r:r + ts, :] = _rms_norm_bf16(x_ref[r:r + ts, :], gpre_ref[...])

    y_attn = {}
    for r in blocks:
        z_a = proj(r, wmid_ref, 0, ATTN_WIDTH)
        ya_in = o_ref[r:r + ts, :].astype(F32) * (z_a * _sigmoid(z_a))
        y_attn[r] = jnp.dot(ya_in.astype(BF16), wa_ref[...], preferred_element_type=F32)

    for r in blocks:
        u = proj(r, wmid_ref, ATTN_WIDTH + cwid, cwid) * proj(r, whi_ref, 0, cwid)
        u_ref[HALO_ROWS + r:HALO_ROWS + r + ts, :] = u
    y_conv = {}
    for r in blocks:
        conv = cw_ref[CONV_K - 1:CONV_K, :] * u_ref[HALO_ROWS + r:HALO_ROWS + r + ts, :]
        for k in range(CONV_K - 1):
            first_row = HALO_ROWS + r - (CONV_K - 1 - k)
            conv = conv + cw_ref[k:k + 1, :] * u_ref[first_row:first_row + ts, :]
        z_c = proj(r, whi_ref, cwid, cwid)
        yc_in = (proj(r, wmid_ref, ATTN_WIDTH, cwid) * conv) * (z_c * _sigmoid(z_c))
        y_conv[r] = jnp.dot(yc_in.astype(BF16), wc_ref[...], preferred_element_type=F32)
    u_ref[0:HALO_ROWS, :] = u_ref[tm:tm + HALO_ROWS, :]

    y = {}
    for r in blocks:
        g_a = _sigmoid(proj(r, whi_ref, 2 * cwid, D_MODEL) + bm_ref[:, 0:D_MODEL])
        g_c = _sigmoid(proj(r, whi_ref, 2 * cwid + D_MODEL, D_MODEL)
                       + bm_ref[:, D_MODEL:2 * D_MODEL])
        y[r] = (g_a * y_attn[r] + g_c * y_conv[r]).astype(BF16)
    for r in blocks:
        out = jnp.dot(y[r], wo_ref[...], preferred_element_type=F32)
        ms = jnp.mean(out * out, axis=-1, keepdims=True)
        out_ref[r:r + ts, :] = x_ref[r:r + ts, :] + out * lax.rsqrt(ms + NORM_EPS) * gp_ref[...]


def _out_proj(x2d, o, g_pre, w_in_bf16, conv_w, wa, wc, b_merge, wo, g_post, *, seq):
    n = x2d.shape[0]
    mid_cols = ATTN_WIDTH + 2 * CONV_WIDTH
    hi_cols = 2 * CONV_WIDTH + GATE_COLS
    assert QKV_COLS == mid_cols and QKV_COLS + mid_cols == hi_cols == IN_COLS - hi_cols
    row_block = lambda cols: pl.BlockSpec((OUT_ROWS, cols), lambda i: (i, 0))
    const = lambda shape, col_block=0: pl.BlockSpec(shape, lambda i: (0, col_block),
                                                   pipeline_mode=pl.Buffered(1))
    return pl.pallas_call(
        functools.partial(_out_proj_kernel, tiles_per_seq=seq // OUT_ROWS),
        grid=(n // OUT_ROWS,),
        in_specs=[
            row_block(D_MODEL), row_block(ATTN_WIDTH), const((1, D_MODEL)),
            const((D_MODEL, mid_cols), 1), const((D_MODEL, hi_cols), 1),
            const((CONV_K, CONV_WIDTH)), const((ATTN_WIDTH, D_MODEL)),
            const((CONV_WIDTH, D_MODEL)), const((1, 2 * D_MODEL)), const((D_MODEL, D_MODEL)),
            const((1, D_MODEL)),
        ],
        out_specs=row_block(D_MODEL),
        out_shape=jax.ShapeDtypeStruct((n, D_MODEL), F32),
        scratch_shapes=[pltpu.VMEM((OUT_ROWS, D_MODEL), BF16),
                        pltpu.VMEM((HALO_ROWS + OUT_ROWS, CONV_WIDTH), F32)],
        compiler_params=pltpu.CompilerParams(
            dimension_semantics=("arbitrary",), vmem_limit_bytes=VMEM_LIMIT_BYTES),
        name="out_proj",
    )(x2d, o, g_pre, w_in_bf16, w_in_bf16, conv_w, wa, wc, b_merge, wo, g_post)


def _layer(x, layer_idx, w_in, lq1, lk1, lq2, lk2, subln_gain, conv_w, w_attn_o, w_conv_o,
           b_merge, w_out, g_pre, g_post):
    batch, seq, d = x.shape
    x2d = x.reshape(batch * seq, d)
    row = lambda a: a.reshape(1, -1).astype(F32)
    slopes = jnp.asarray([2.0 ** (-8.0 * (i + 1) / ATTN_HEADS) for i in range(ATTN_HEADS)], F32)

    q, ka, kb, v, qnorm, knorm, w_in_bf16, wa_bf16, wc_bf16, wo_bf16 = _qkv_proj(
        slopes, x2d, row(g_pre), w_in, w_attn_o, w_conv_o, w_out)
    o = _diff_attn(q, ka, kb, v, slopes, qnorm, knorm, row(lq1), row(lk1), row(lq2), row(lk2),
                   row(subln_gain), batch=batch, seq=seq, lam_init=_lambda_init(layer_idx))
    out = _out_proj(x2d, o, row(g_pre), w_in_bf16, conv_w.astype(F32), wa_bf16, wc_bf16,
                    row(b_merge), wo_bf16, row(g_post), seq=seq)
    return out.reshape(batch, seq, d)


def kernel(x, w_in, lambda_q1, lambda_k1, lambda_q2, lambda_k2, subln_gain, conv_w, w_attn_o,
           w_conv_o, b_merge, w_out, g_pre, g_post):
    for l in range(w_in.shape[0]):
        x = _layer(x, l, w_in[l], lambda_q1[l], lambda_k1[l], lambda_q2[l], lambda_k2[l],
                   subln_gain[l], conv_w[l], w_attn_o[l], w_conv_o[l], b_merge[l], w_out[l],
                   g_pre[l], g_post[l])
    return x
```

```python
import functools
import math

import jax
import jax.numpy as jnp
from jax import lax
from jax.experimental import pallas as pl
from jax.experimental.pallas import tpu as pltpu

D_MODEL = 1024
ATTN_HEADS = 4
HEAD_DIM = 64
HEAD_COLS = 2 * HEAD_DIM
ATTN_WIDTH = ATTN_HEADS * HEAD_COLS
CONV_WIDTH = D_MODEL // 2
CONV_K = 3
NORM_EPS = 1e-6
QKV_COLS = 3 * ATTN_WIDTH
CONV_COLS = 4 * CONV_WIDTH
GATE_COLS = 2 * D_MODEL
IN_COLS = QKV_COLS + ATTN_WIDTH + CONV_COLS + GATE_COLS
MASK_VALUE = -1e30
LOG2_E = math.log2(math.e)
Q_SCALE_LOG2 = HEAD_DIM ** -0.5 * LOG2_E
ALIBI_SKIP_BITS = 160.0
NORM_SLACK = 1.01

PROJ_ROWS = 1024
PROJ_SUB_ROWS = 512
ATTN_Q_ROWS = 1024
ATTN_K_ROWS = 1024
TK_SHIFT = ATTN_K_ROWS.bit_length() - 1
ATTN_SUB_ROWS = 256
ATTN_LOOKAHEAD = 3
assert ATTN_K_ROWS == 1 << TK_SHIFT and ATTN_K_ROWS % ATTN_Q_ROWS == 0
assert PROJ_ROWS == ATTN_K_ROWS
assert ATTN_Q_ROWS % PROJ_SUB_ROWS == 0 and PROJ_SUB_ROWS % ATTN_SUB_ROWS == 0
OUT_ROWS = 1024
OUT_SUB_ROWS = 512
HALO_ROWS = 16
VMEM_LIMIT_BYTES = 56 * 1024 * 1024

BF16 = jnp.bfloat16
F32 = jnp.float32


def _lambda_init(layer_idx):
    return 0.8 - 0.6 * math.exp(-0.3 * layer_idx)


def _rms_norm_bf16(x, gain):
    ms = jnp.mean(x * x, axis=-1, keepdims=True)
    return (x * lax.rsqrt(ms + NORM_EPS) * gain).astype(BF16)


def _max_map_norm(x, lane):
    sq = x * x
    n0 = jnp.sum(jnp.where(lane < HEAD_DIM, sq, 0.0), axis=-1, keepdims=True)
    n1 = jnp.sum(jnp.where(lane >= HEAD_DIM, sq, 0.0), axis=-1, keepdims=True)
    return jnp.sqrt(jnp.max(jnp.maximum(n0, n1), axis=0, keepdims=True))[0, 0] * NORM_SLACK


def _qkv_proj_kernel(slopes_ref, x_ref, g_ref, wqkv_ref, win_ref, wa_ref, wc_ref, wo_ref,
                     q_ref, ka_ref, kb_ref, v_ref, qnorm_ref, knorm_ref, win16_ref, wa16_ref,
                     wc16_ref, wo16_ref, h_ref, biasa_ref, biasb_ref, w_ref):
    for w32_ref, w16_ref in ((win_ref, win16_ref), (wa_ref, wa16_ref), (wc_ref, wc16_ref),
                             (wo_ref, wo16_ref)):
        w16_ref[...] = w32_ref[...].astype(BF16)

    @pl.when(pl.program_id(0) == 0)
    def _first_step():
        w_ref[...] = wqkv_ref[...].astype(BF16)
        lane = lax.broadcasted_iota(jnp.int32, (PROJ_ROWS, HEAD_COLS), 1)
        row = lax.broadcasted_iota(jnp.int32, (PROJ_ROWS, HEAD_COLS), 0).astype(F32)
        for hd in range(ATTN_HEADS):
            b = row * (slopes_ref[hd] * LOG2_E)
            b1 = b.astype(BF16).astype(F32)
            b2 = (b - b1).astype(BF16).astype(F32)
            b3 = (b - b1 - b2).astype(BF16).astype(F32)
            for bias_ref, first in ((biasa_ref, HEAD_DIM), (biasb_ref, 0)):
                bias_ref[hd] = jnp.where(lane == first, b1, jnp.where(
                    lane == first + 1, b2, jnp.where(lane == first + 2, b3, 0.0)))

    lane = lax.broadcasted_iota(jnp.int32, (PROJ_SUB_ROWS, HEAD_COLS), 1)
    for r in range(0, PROJ_ROWS, PROJ_SUB_ROWS):
        rows = slice(r, r + PROJ_SUB_ROWS)
        norm_row = pl.program_id(0) * (PROJ_ROWS // PROJ_SUB_ROWS) + r // PROJ_SUB_ROWS
        h_ref[rows, :] = _rms_norm_bf16(x_ref[rows, :], g_ref[...])

        def proj(col, rows=rows):
            return jnp.dot(h_ref[rows, :], w_ref[:, col:col + ATTN_WIDTH],
                           preferred_element_type=F32)

        q = proj(0) * Q_SCALE_LOG2
        q_ref[rows, :] = q.astype(BF16)
        k = proj(ATTN_WIDTH)
        for hd in range(ATTN_HEADS):
            cols = slice(hd * HEAD_COLS, (hd + 1) * HEAD_COLS)
            kh = k[:, cols]
            ka_ref[rows, cols] = jnp.where(lane < HEAD_DIM, kh, biasa_ref[hd, rows, :]).astype(BF16)
            kb_ref[rows, cols] = jnp.where(lane >= HEAD_DIM, kh, biasb_ref[hd, rows, :]).astype(BF16)
            qnorm_ref[norm_row, hd] = _max_map_norm(q[:, cols], lane)
            knorm_ref[norm_row, hd] = _max_map_norm(kh, lane)
        v_ref[rows, :] = proj(2 * ATTN_WIDTH).astype(BF16)


def _qkv_proj(slopes, x2d, g_pre, w_in, w_attn_o, w_conv_o, w_out):
    n = x2d.shape[0]
    steps = n // PROJ_ROWS
    row_block = lambda cols: pl.BlockSpec((PROJ_ROWS, cols), lambda i: (i, 0))
    act = jax.ShapeDtypeStruct((n, ATTN_WIDTH), BF16)
    norms = jax.ShapeDtypeStruct((n // PROJ_SUB_ROWS, ATTN_HEADS), F32)
    weights = (w_in, w_attn_o, w_conv_o, w_out)
    assert all(w.shape[0] % (steps * HALO_ROWS) == 0 for w in weights)
    slab = lambda w: pl.BlockSpec((w.shape[0] // steps, w.shape[1]), lambda i: (i, 0))
    return pl.pallas_call(
        _qkv_proj_kernel,
        grid=(steps,),
        in_specs=[
            pl.BlockSpec(memory_space=pltpu.SMEM),
            row_block(D_MODEL),
            pl.BlockSpec((1, D_MODEL), lambda i: (0, 0)),
            pl.BlockSpec((D_MODEL, QKV_COLS), lambda i: (0, 0), pipeline_mode=pl.Buffered(1)),
        ] + [slab(w) for w in weights],
        out_specs=[row_block(ATTN_WIDTH), row_block(ATTN_WIDTH), row_block(ATTN_WIDTH),
                   row_block(ATTN_WIDTH), pl.BlockSpec(memory_space=pltpu.SMEM),
                   pl.BlockSpec(memory_space=pltpu.SMEM)]
        + [slab(w) for w in weights],
        out_shape=[act, act, act, act, norms, norms]
        + [jax.ShapeDtypeStruct(w.shape, BF16) for w in weights],
        scratch_shapes=[pltpu.VMEM((PROJ_ROWS, D_MODEL), BF16),
                        pltpu.VMEM((ATTN_HEADS, PROJ_ROWS, HEAD_COLS), F32),
                        pltpu.VMEM((ATTN_HEADS, PROJ_ROWS, HEAD_COLS), F32),
                        pltpu.VMEM((D_MODEL, QKV_COLS), BF16)],
        compiler_params=pltpu.CompilerParams(
            dimension_semantics=("arbitrary",), vmem_limit_bytes=VMEM_LIMIT_BYTES),
        name="qkv_proj",
    )(slopes, x2d, g_pre, w_in, w_in, w_attn_o, w_conv_o, w_out)


def _diff_attn_kernel(slopes_ref, qnorm_ref, knorm_ref, lq1_ref, lk1_ref, lq2_ref, lk2_ref,
                      gain_ref, q_ref, ka_ref, kb_ref, v_ref, o_ref, m_ref, acc_ref,
                      *, lam_init):
    tq, tk, ts = ATTN_Q_ROWS, ATTN_K_ROWS, ATTN_SUB_ROWS
    th = PROJ_SUB_ROWS
    batch = pl.program_id(0)
    head = pl.program_id(1)
    qi = pl.program_id(2)
    slope = slopes_ref[head] * LOG2_E
    seq_norm_row = batch * (ka_ref.shape[0] // th)

    q = q_ref[...].astype(F32)
    qlane = lax.broadcasted_iota(jnp.int32, q.shape, 1)
    q_maps = (
        jnp.where(qlane < HEAD_DIM, q, jnp.where(qlane < HEAD_DIM + 3, 1.0, 0.0)).astype(BF16),
        jnp.where(qlane >= HEAD_DIM, q, jnp.where(qlane < 3, 1.0, 0.0)).astype(BF16),
    )
    key_refs = (ka_ref, kb_ref)

    tri_keep = (lax.broadcasted_iota(jnp.int32, (ts, ts), 1)
                <= lax.broadcasted_iota(jnp.int32, (ts, ts), 0))

    def step(key_start, diag_cols, row_stop=tq):
        span_bias = (key_start - qi * tq).astype(F32) * slope
        first = diag_cols is not None
        streams = [(r, mp)
                   for r in (range(tq - ts, -ts, -ts) if first else range(0, row_stop, ts))
                   for mp in range(2)]

        def width(r):
            return tk if diag_cols is None else diag_cols + r + ts

        def scores(r, mp):
            keys = key_refs[mp][pl.ds(key_start, width(r)), :]
            s = lax.dot_general(q_maps[mp][r:r + ts], keys, (((1,), (1,)), ((), ())),
                                preferred_element_type=F32)
            if diag_cols is not None:
                below = width(r) - ts
                blocks = [s[:, :below]] if below else []
                s = jnp.concatenate(blocks + [jnp.where(tri_keep, s[:, below:], MASK_VALUE)],
                                    axis=1)
            return s

        def update(r, mp, s):
            m_cur = jnp.max(s, axis=-1, keepdims=True) + span_bias
            if first:
                m_new = jnp.broadcast_to(m_cur, (ts, HEAD_COLS))
            else:
                m_prev = m_ref[mp, r:r + ts]
                m_new = jnp.maximum(m_prev, m_cur)
            shift = m_new - span_bias
            p = jnp.exp2(s - jnp.concatenate([shift] * (width(r) // HEAD_COLS), axis=1))
            v_ones = jnp.concatenate([v_ref[pl.ds(key_start, width(r)), :],
                                      jnp.ones((width(r), HEAD_COLS), BF16)], axis=1)
            pv = jnp.dot(p.astype(BF16), v_ones, preferred_element_type=F32)
            if not first:
                alpha = jnp.exp2(m_prev - m_new)
                pv = jnp.concatenate([alpha, alpha], axis=1) * acc_ref[mp, r:r + ts] + pv
            acc_ref[mp, r:r + ts] = pv
            m_ref[mp, r:r + ts] = m_new

        pending = [scores(*st) for st in streams[:ATTN_LOOKAHEAD]]
        for i, st in enumerate(streams):
            if i + ATTN_LOOKAHEAD < len(streams):
                pending.append(scores(*streams[i + ATTN_LOOKAHEAD]))
            update(*st, pending.pop(0))

    n_spans = lax.shift_right_logical(qi * tq, TK_SHIFT)
    tail_start = pl.multiple_of(n_spans * tk, tk)
    for extra in range(0, tk, tq):
        @pl.when(qi * tq - tail_start == extra)
        def _diag_step(extra=extra):
            step(tail_start, extra)

    def row_block_bound(hf):
        m_low = jnp.minimum(m_ref[0, hf * th:(hf + 1) * th], m_ref[1, hf * th:(hf + 1) * th])
        m_low = jnp.min(jnp.min(m_low, axis=0, keepdims=True), axis=1, keepdims=True)[0, 0]
        q_norm = qnorm_ref[seq_norm_row + qi * (tq // th) + hf, head]
        return q_norm, m_low - ALIBI_SKIP_BITS

    row_blocks = [row_block_bound(hf) for hf in range(tq // th)]

    def full_span(kj, carry):
        key_start = pl.multiple_of(kj * tk, tk)
        k_norm = knorm_ref[seq_norm_row + kj * (tk // th), head]
        for part in range(1, tk // th):
            k_norm = jnp.maximum(k_norm, knorm_ref[seq_norm_row + kj * (tk // th) + part, head])
        alibi_max = (key_start + (tk - 1) - qi * tq).astype(F32) * slope
        live = [q_norm * k_norm + alibi_max > floor for q_norm, floor in row_blocks]
        rows_needed = jnp.int32(0)
        for hf, is_live in enumerate(live):
            rows_needed = jnp.where(is_live, (hf + 1) * th, rows_needed)
        for hf in range(tq // th):
            @pl.when(rows_needed == (hf + 1) * th)
            def _fold_span(hf=hf):
                step(key_start, None, row_stop=(hf + 1) * th)

        return carry

    lax.fori_loop(0, n_spans, full_span, 0)

    lam = (jnp.exp(jnp.sum(lq1_ref[...] * lk1_ref[...], axis=-1, keepdims=True))
           - jnp.exp(jnp.sum(lq2_ref[...] * lk2_ref[...], axis=-1, keepdims=True))
           + lam_init)
    o = (acc_ref[0, :, 0:HEAD_COLS] / acc_ref[0, :, HEAD_COLS:2 * HEAD_COLS]
         - lam * (acc_ref[1, :, 0:HEAD_COLS] / acc_ref[1, :, HEAD_COLS:2 * HEAD_COLS]))
    ms = jnp.mean(o * o, axis=-1, keepdims=True)
    o = (o * lax.rsqrt(ms + NORM_EPS) * gain_ref[...]) * (1.0 - lam_init)
    o_ref[...] = o.astype(BF16)


def _diff_attn(q, ka, kb, v, slopes, qnorm, knorm, lq1, lk1, lq2, lk2, gain,
               *, batch, seq, lam_init):
    nq = seq // ATTN_Q_ROWS
    small = lambda cols: pl.BlockSpec((1, cols), lambda b, h, i: (0, 0))
    seq_block = pl.BlockSpec((seq, HEAD_COLS), lambda b, h, i: (b, h))
    q_block = pl.BlockSpec((ATTN_Q_ROWS, HEAD_COLS), lambda b, h, i: (b * nq + i, h))
    return pl.pallas_call(
        functools.partial(_diff_attn_kernel, lam_init=lam_init),
        grid=(batch, ATTN_HEADS, nq),
        in_specs=[
            pl.BlockSpec(memory_space=pltpu.SMEM), pl.BlockSpec(memory_space=pltpu.SMEM),
            pl.BlockSpec(memory_space=pltpu.SMEM),
            small(HEAD_DIM), small(HEAD_DIM), small(HEAD_DIM), small(HEAD_DIM),
            small(HEAD_COLS),
            q_block, seq_block, seq_block, seq_block,
        ],
        out_specs=q_block,
        out_shape=jax.ShapeDtypeStruct((batch * seq, ATTN_WIDTH), BF16),
        scratch_shapes=[pltpu.VMEM((2, ATTN_Q_ROWS, HEAD_COLS), F32),
                        pltpu.VMEM((2, ATTN_Q_ROWS, 2 * HEAD_COLS), F32)],
        compiler_params=pltpu.CompilerParams(
            dimension_semantics=("arbitrary", "arbitrary", "arbitrary"),
            vmem_limit_bytes=VMEM_LIMIT_BYTES),
        name="diff_attn",
    )(slopes, qnorm, knorm, lq1, lk1, lq2, lk2, gain, q, ka, kb, v)


def _sigmoid(z):
    return 0.5 * jnp.tanh(0.5 * z) + 0.5


def _out_proj_kernel(x_ref, o_ref, gpre_ref, wmid_ref, whi_ref, cw_ref, wa_ref, wc_ref, bm_ref,
                     wo_ref, gp_ref, out_ref, h_ref, u_ref, *, tiles_per_seq):
    tm, ts, cwid = OUT_ROWS, OUT_SUB_ROWS, CONV_WIDTH

    @pl.when(pl.program_id(0) % tiles_per_seq == 0)
    def _sequence_start():
        u_ref[0:HALO_ROWS, :] = jnp.zeros((HALO_ROWS, cwid), F32)

    blocks = range(0, tm, ts)

    def proj(r, w_ref, col, cols):
        return jnp.dot(h_ref[r:r + ts, :], w_ref[:, col:col + cols], preferred_element_type=F32)

    for r in blocks:
        h_ref[r:r + ts, :] = _rms_norm_bf16(x_ref[r:r + ts, :], gpre_ref[...])

    y_attn = {}
    for r in blocks:
        z_a = proj(r, wmid_ref, 0, ATTN_WIDTH)
        ya_in = o_ref[r:r + ts, :].astype(F32) * (z_a * _sigmoid(z_a))
        y_attn[r] = jnp.dot(ya_in.astype(BF16), wa_ref[...], preferred_element_type=F32)

    for r in blocks:
        u = proj(r, wmid_ref, ATTN_WIDTH + cwid, cwid) * proj(r, whi_ref, 0, cwid)
        u_ref[HALO_ROWS + r:HALO_ROWS + r + ts, :] = u
    y_conv = {}
    for r in blocks:
        conv = cw_ref[CONV_K - 1:CONV_K, :] * u_ref[HALO_ROWS + r:HALO_ROWS + r + ts, :]
        for k in range(CONV_K - 1):
            first_row = HALO_ROWS + r - (CONV_K - 1 - k)
            conv = conv + cw_ref[k:k + 1, :] * u_ref[first_row:first_row + ts, :]
        z_c = proj(r, whi_ref, cwid, cwid)
        yc_in = (proj(r, wmid_ref, ATTN_WIDTH, cwid) * conv) * (z_c * _sigmoid(z_c))
        y_conv[r] = jnp.dot(yc_in.astype(BF16), wc_ref[...], preferred_element_type=F32)
    u_ref[0:HALO_ROWS, :] = u_ref[tm:tm + HALO_ROWS, :]

    y = {}
    for r in blocks:
        g_a = _sigmoid(proj(r, whi_ref, 2 * cwid, D_MODEL) + bm_ref[:, 0:D_MODEL])
        g_c = _sigmoid(proj(r, whi_ref, 2 * cwid + D_MODEL, D_MODEL)
                       + bm_ref[:, D_MODEL:2 * D_MODEL])
        y[r] = (g_a * y_attn[r] + g_c * y_conv[r]).astype(BF16)
    for r in blocks:
        out = jnp.dot(y[r], wo_ref[...], preferred_element_type=F32)
        ms = jnp.mean(out * out, axis=-1, keepdims=True)
        out_ref[r:r + ts, :] = x_ref[r:r + ts, :] + out * lax.rsqrt(ms + NORM_EPS) * gp_ref[...]


def _out_proj(x2d, o, g_pre, w_in_bf16, conv_w, wa, wc, b_merge, wo, g_post, *, seq):
    n = x2d.shape[0]
    mid_cols = ATTN_WIDTH + 2 * CONV_WIDTH
    hi_cols = 2 * CONV_WIDTH + GATE_COLS
    assert QKV_COLS == mid_cols and QKV_COLS + mid_cols == hi_cols == IN_COLS - hi_cols
    row_block = lambda cols: pl.BlockSpec((OUT_ROWS, cols), lambda i: (i, 0))
    const = lambda shape, col_block=0: pl.BlockSpec(shape, lambda i: (0, col_block),
                                                   pipeline_mode=pl.Buffered(1))
    return pl.pallas_call(
        functools.partial(_out_proj_kernel, tiles_per_seq=seq // OUT_ROWS),
        grid=(n // OUT_ROWS,),
        in_specs=[
            row_block(D_MODEL), row_block(ATTN_WIDTH), const((1, D_MODEL)),
            const((D_MODEL, mid_cols), 1), const((D_MODEL, hi_cols), 1),
            const((CONV_K, CONV_WIDTH)), const((ATTN_WIDTH, D_MODEL)),
            const((CONV_WIDTH, D_MODEL)), const((1, 2 * D_MODEL)), const((D_MODEL, D_MODEL)),
            const((1, D_MODEL)),
        ],
        out_specs=row_block(D_MODEL),
        out_shape=jax.ShapeDtypeStruct((n, D_MODEL), F32),
        scratch_shapes=[pltpu.VMEM((OUT_ROWS, D_MODEL), BF16),
                        pltpu.VMEM((HALO_ROWS + OUT_ROWS, CONV_WIDTH), F32)],
        compiler_params=pltpu.CompilerParams(
            dimension_semantics=("arbitrary",), vmem_limit_bytes=VMEM_LIMIT_BYTES),
        name="out_proj",
    )(x2d, o, g_pre, w_in_bf16, w_in_bf16, conv_w, wa, wc, b_merge, wo, g_post)


def _layer(x, layer_idx, w_in, lq1, lk1, lq2, lk2, subln_gain, conv_w, w_attn_o, w_conv_o,
           b_merge, w_out, g_pre, g_post):
    batch, seq, d = x.shape
    x2d = x.reshape(batch * seq, d)
    row = lambda a: a.reshape(1, -1).astype(F32)
    slopes = jnp.asarray([2.0 ** (-8.0 * (i + 1) / ATTN_HEADS) for i in range(ATTN_HEADS)], F32)

    q, ka, kb, v, qnorm, knorm, w_in_bf16, wa_bf16, wc_bf16, wo_bf16 = _qkv_proj(
        slopes, x2d, row(g_pre), w_in, w_attn_o, w_conv_o, w_out)
    o = _diff_attn(q, ka, kb, v, slopes, qnorm, knorm, row(lq1), row(lk1), row(lq2), row(lk2),
                   row(subln_gain), batch=batch, seq=seq, lam_init=_lambda_init(layer_idx))
    out = _out_proj(x2d, o, row(g_pre), w_in_bf16, conv_w.astype(F32), wa_bf16, wc_bf16,
                    row(b_merge), wo_bf16, row(g_post), seq=seq)
    return out.reshape(batch, seq, d)


def kernel(x, w_in, lambda_q1, lambda_k1, lambda_q2, lambda_k2, subln_gain, conv_w, w_attn_o,
           w_conv_o, b_merge, w_out, g_pre, g_post):
    for l in range(w_in.shape[0]):
        x = _layer(x, l, w_in[l], lambda_q1[l], lambda_k1[l], lambda_q2[l], lambda_k2[l],
                   subln_gain[l], conv_w[l], w_attn_o[l], w_conv_o[l], b_merge[l], w_out[l],
                   g_pre[l], g_post[l])
    return x
```

```python
import functools
import math

import jax
import jax.numpy as jnp
from jax import lax
from jax.experimental import pallas as pl
from jax.experimental.pallas import tpu as pltpu

D_MODEL = 1024
ATTN_HEADS = 4
HEAD_DIM = 64
HEAD_COLS = 2 * HEAD_DIM
ATTN_WIDTH = ATTN_HEADS * HEAD_COLS
CONV_WIDTH = D_MODEL // 2
CONV_K = 3
NORM_EPS = 1e-6
QKV_COLS = 3 * ATTN_WIDTH
CONV_COLS = 4 * CONV_WIDTH
GATE_COLS = 2 * D_MODEL
IN_COLS = QKV_COLS + ATTN_WIDTH + CONV_COLS + GATE_COLS
MASK_VALUE = -1e30
LOG2_E = math.log2(math.e)
Q_SCALE_LOG2 = HEAD_DIM ** -0.5 * LOG2_E
ALIBI_SKIP_BITS = 160.0
NORM_SLACK = 1.01

PROJ_ROWS = 1024
PROJ_SUB_ROWS = 512
ATTN_Q_ROWS = 1024
ATTN_K_ROWS = 1024
TK_SHIFT = ATTN_K_ROWS.bit_length() - 1
ATTN_SUB_ROWS = 256
ATTN_LOOKAHEAD = 3
assert ATTN_K_ROWS == 1 << TK_SHIFT and ATTN_K_ROWS % ATTN_Q_ROWS == 0
assert PROJ_ROWS == ATTN_K_ROWS
assert ATTN_Q_ROWS % PROJ_SUB_ROWS == 0 and PROJ_SUB_ROWS % ATTN_SUB_ROWS == 0
OUT_ROWS = 1024
OUT_SUB_ROWS = 512
HALO_ROWS = 16
VMEM_LIMIT_BYTES = 56 * 1024 * 1024

BF16 = jnp.bfloat16
F32 = jnp.float32


def _lambda_init(layer_idx):
    return 0.8 - 0.6 * math.exp(-0.3 * layer_idx)


def _rms_norm_bf16(x, gain):
    ms = jnp.mean(x * x, axis=-1, keepdims=True)
    return (x * lax.rsqrt(ms + NORM_EPS) * gain).astype(BF16)


def _max_map_norm(x, lane):
    sq = x * x
    n0 = jnp.sum(jnp.where(lane < HEAD_DIM, sq, 0.0), axis=-1, keepdims=True)
    n1 = jnp.sum(jnp.where(lane >= HEAD_DIM, sq, 0.0), axis=-1, keepdims=True)
    return jnp.sqrt(jnp.max(jnp.maximum(n0, n1), axis=0, keepdims=True))[0, 0] * NORM_SLACK


def _qkv_proj_kernel(slopes_ref, x_ref, g_ref, wqkv_ref, win_ref, wa_ref, wc_ref, wo_ref,
                     q_ref, ka_ref, kb_ref, v_ref, qnorm_ref, knorm_ref, win16_ref, wa16_ref,
                     wc16_ref, wo16_ref, h_ref, biasa_ref, biasb_ref, w_ref):
    for w32_ref, w16_ref in ((win_ref, win16_ref), (wa_ref, wa16_ref), (wc_ref, wc16_ref),
                             (wo_ref, wo16_ref)):
        w16_ref[...] = w32_ref[...].astype(BF16)

    @pl.when(pl.program_id(0) == 0)
    def _first_step():
        w_ref[...] = wqkv_ref[...].astype(BF16)
        lane = lax.broadcasted_iota(jnp.int32, (PROJ_ROWS, HEAD_COLS), 1)
        row = lax.broadcasted_iota(jnp.int32, (PROJ_ROWS, HEAD_COLS), 0).astype(F32)
        for hd in range(ATTN_HEADS):
            b = row * (slopes_ref[hd] * LOG2_E)
            b1 = b.astype(BF16).astype(F32)
            b2 = (b - b1).astype(BF16).astype(F32)
            b3 = (b - b1 - b2).astype(BF16).astype(F32)
            for bias_ref, first in ((biasa_ref, HEAD_DIM), (biasb_ref, 0)):
                bias_ref[hd] = jnp.where(lane == first, b1, jnp.where(
                    lane == first + 1, b2, jnp.where(lane == first + 2, b3, 0.0)))

    lane = lax.broadcasted_iota(jnp.int32, (PROJ_SUB_ROWS, HEAD_COLS), 1)
    for r in range(0, PROJ_ROWS, PROJ_SUB_ROWS):
        rows = slice(r, r + PROJ_SUB_ROWS)
        norm_row = pl.program_id(0) * (PROJ_ROWS // PROJ_SUB_ROWS) + r // PROJ_SUB_ROWS
        h_ref[rows, :] = _rms_norm_bf16(x_ref[rows, :], g_ref[...])

        def proj(col, rows=rows):
            return jnp.dot(h_ref[rows, :], w_ref[:, col:col + ATTN_WIDTH],
                           preferred_element_type=F32)

        q = proj(0) * Q_SCALE_LOG2
        q_ref[rows, :] = q.astype(BF16)
        k = proj(ATTN_WIDTH)
        for hd in range(ATTN_HEADS):
            cols = slice(hd * HEAD_COLS, (hd + 1) * HEAD_COLS)
            kh = k[:, cols]
            ka_ref[rows, cols] = jnp.where(lane < HEAD_DIM, kh, biasa_ref[hd, rows, :]).astype(BF16)
            kb_ref[rows, cols] = jnp.where(lane >= HEAD_DIM, kh, biasb_ref[hd, rows, :]).astype(BF16)
            qnorm_ref[norm_row, hd] = _max_map_norm(q[:, cols], lane)
            knorm_ref[norm_row, hd] = _max_map_norm(kh, lane)
        v_ref[rows, :] = proj(2 * ATTN_WIDTH).astype(BF16)


def _qkv_proj(slopes, x2d, g_pre, w_in, w_attn_o, w_conv_o, w_out):
    n = x2d.shape[0]
    steps = n // PROJ_ROWS
    row_block = lambda cols: pl.BlockSpec((PROJ_ROWS, cols), lambda i: (i, 0))
    act = jax.ShapeDtypeStruct((n, ATTN_WIDTH), BF16)
    norms = jax.ShapeDtypeStruct((n // PROJ_SUB_ROWS, ATTN_HEADS), F32)
    weights = (w_in, w_attn_o, w_conv_o, w_out)
    assert all(w.shape[0] % (steps * HALO_ROWS) == 0 for w in weights)
    slab = lambda w: pl.BlockSpec((w.shape[0] // steps, w.shape[1]), lambda i: (i, 0))
    return pl.pallas_call(
        _qkv_proj_kernel,
        grid=(steps,),
        in_specs=[
            pl.BlockSpec(memory_space=pltpu.SMEM),
            row_block(D_MODEL),
            pl.BlockSpec((1, D_MODEL), lambda i: (0, 0)),
            pl.BlockSpec((D_MODEL, QKV_COLS), lambda i: (0, 0), pipeline_mode=pl.Buffered(1)),
        ] + [slab(w) for w in weights],
        out_specs=[row_block(ATTN_WIDTH), row_block(ATTN_WIDTH), row_block(ATTN_WIDTH),
                   row_block(ATTN_WIDTH), pl.BlockSpec(memory_space=pltpu.SMEM),
                   pl.BlockSpec(memory_space=pltpu.SMEM)]
        + [slab(w) for w in weights],
        out_shape=[act, act, act, act, norms, norms]
        + [jax.ShapeDtypeStruct(w.shape, BF16) for w in weights],
        scratch_shapes=[pltpu.VMEM((PROJ_ROWS, D_MODEL), BF16),
                        pltpu.VMEM((ATTN_HEADS, PROJ_ROWS, HEAD_COLS), F32),
                        pltpu.VMEM((ATTN_HEADS, PROJ_ROWS, HEAD_COLS), F32),
                        pltpu.VMEM((D_MODEL, QKV_COLS), BF16)],
        compiler_params=pltpu.CompilerParams(
            dimension_semantics=("arbitrary",), vmem_limit_bytes=VMEM_LIMIT_BYTES),
        name="qkv_proj",
    )(slopes, x2d, g_pre, w_in, w_in, w_attn_o, w_conv_o, w_out)


def _diff_attn_kernel(slopes_ref, qnorm_ref, knorm_ref, q_ref, ka_ref, kb_ref, v_ref,
                      o0_ref, o1_ref, m_ref, acc_ref):
    tq, tk, ts = ATTN_Q_ROWS, ATTN_K_ROWS, ATTN_SUB_ROWS
    th = PROJ_SUB_ROWS
    batch = pl.program_id(0)
    head = pl.program_id(1)
    qi = pl.program_id(2)
    slope = slopes_ref[head] * LOG2_E
    seq_norm_row = batch * (ka_ref.shape[0] // th)

    q = q_ref[...].astype(F32)
    qlane = lax.broadcasted_iota(jnp.int32, q.shape, 1)
    q_maps = (
        jnp.where(qlane < HEAD_DIM, q, jnp.where(qlane < HEAD_DIM + 3, 1.0, 0.0)).astype(BF16),
        jnp.where(qlane >= HEAD_DIM, q, jnp.where(qlane < 3, 1.0, 0.0)).astype(BF16),
    )
    key_refs = (ka_ref, kb_ref)

    tri_keep = (lax.broadcasted_iota(jnp.int32, (ts, ts), 1)
                <= lax.broadcasted_iota(jnp.int32, (ts, ts), 0))

    def step(key_start, diag_cols, row_stop=tq):
        span_bias = (key_start - qi * tq).astype(F32) * slope
        first = diag_cols is not None
        streams = [(r, mp)
                   for r in (range(tq - ts, -ts, -ts) if first else range(0, row_stop, ts))
                   for mp in range(2)]

        def width(r):
            return tk if diag_cols is None else diag_cols + r + ts

        def scores(r, mp):
            keys = key_refs[mp][pl.ds(key_start, width(r)), :]
            s = lax.dot_general(q_maps[mp][r:r + ts], keys, (((1,), (1,)), ((), ())),
                                preferred_element_type=F32)
            if diag_cols is not None:
                below = width(r) - ts
                blocks = [s[:, :below]] if below else []
                s = jnp.concatenate(blocks + [jnp.where(tri_keep, s[:, below:], MASK_VALUE)],
                                    axis=1)
            return s

        def update(r, mp, s):
            m_cur = jnp.max(s, axis=-1, keepdims=True) + span_bias
            if first:
                m_new = jnp.broadcast_to(m_cur, (ts, HEAD_COLS))
            else:
                m_prev = m_ref[mp, r:r + ts]
                m_new = jnp.maximum(m_prev, m_cur)
            shift = m_new - span_bias
            p = jnp.exp2(s - jnp.concatenate([shift] * (width(r) // HEAD_COLS), axis=1))
            v_ones = jnp.concatenate([v_ref[pl.ds(key_start, width(r)), :],
                                      jnp.ones((width(r), HEAD_COLS), BF16)], axis=1)
            pv = jnp.dot(p.astype(BF16), v_ones, preferred_element_type=F32)
            if not first:
                alpha = jnp.exp2(m_prev - m_new)
                pv = jnp.concatenate([alpha, alpha], axis=1) * acc_ref[mp, r:r + ts] + pv
            acc_ref[mp, r:r + ts] = pv
            m_ref[mp, r:r + ts] = m_new

        pending = [scores(*st) for st in streams[:ATTN_LOOKAHEAD]]
        for i, st in enumerate(streams):
            if i + ATTN_LOOKAHEAD < len(streams):
                pending.append(scores(*streams[i + ATTN_LOOKAHEAD]))
            update(*st, pending.pop(0))

    n_spans = lax.shift_right_logical(qi * tq, TK_SHIFT)
    tail_start = pl.multiple_of(n_spans * tk, tk)
    for extra in range(0, tk, tq):
        @pl.when(qi * tq - tail_start == extra)
        def _diag_step(extra=extra):
            step(tail_start, extra)

    def row_block_bound(hf):
        m_low = jnp.minimum(m_ref[0, hf * th:(hf + 1) * th], m_ref[1, hf * th:(hf + 1) * th])
        m_low = jnp.min(jnp.min(m_low, axis=0, keepdims=True), axis=1, keepdims=True)[0, 0]
        q_norm = qnorm_ref[seq_norm_row + qi * (tq // th) + hf, head]
        return q_norm, m_low - ALIBI_SKIP_BITS

    row_blocks = [row_block_bound(hf) for hf in range(tq // th)]

    def full_span(kj, carry):
        key_start = pl.multiple_of(kj * tk, tk)
        k_norm = knorm_ref[seq_norm_row + kj * (tk // th), head]
        for part in range(1, tk // th):
            k_norm = jnp.maximum(k_norm, knorm_ref[seq_norm_row + kj * (tk // th) + part, head])
        alibi_max = (key_start + (tk - 1) - qi * tq).astype(F32) * slope
        live = [q_norm * k_norm + alibi_max > floor for q_norm, floor in row_blocks]
        rows_needed = jnp.int32(0)
        for hf, is_live in enumerate(live):
            rows_needed = jnp.where(is_live, (hf + 1) * th, rows_needed)
        for hf in range(tq // th):
            @pl.when(rows_needed == (hf + 1) * th)
            def _fold_span(hf=hf):
                step(key_start, None, row_stop=(hf + 1) * th)

        return carry

    lax.fori_loop(0, n_spans, full_span, 0)

    for mp, o_ref in enumerate((o0_ref, o1_ref)):
        o_ref[...] = (acc_ref[mp, :, 0:HEAD_COLS]
                      / acc_ref[mp, :, HEAD_COLS:2 * HEAD_COLS]).astype(BF16)


def _diff_attn(q, ka, kb, v, slopes, qnorm, knorm, *, batch, seq):
    nq = seq // ATTN_Q_ROWS
    seq_block = pl.BlockSpec((seq, HEAD_COLS), lambda b, h, i: (b, h))
    q_block = pl.BlockSpec((ATTN_Q_ROWS, HEAD_COLS), lambda b, h, i: (b * nq + i, h))
    return pl.pallas_call(
        _diff_attn_kernel,
        grid=(batch, ATTN_HEADS, nq),
        in_specs=[
            pl.BlockSpec(memory_space=pltpu.SMEM), pl.BlockSpec(memory_space=pltpu.SMEM),
            pl.BlockSpec(memory_space=pltpu.SMEM),
            q_block, seq_block, seq_block, seq_block,
        ],
        out_specs=[q_block, q_block],
        out_shape=[jax.ShapeDtypeStruct((batch * seq, ATTN_WIDTH), BF16)] * 2,
        scratch_shapes=[pltpu.VMEM((2, ATTN_Q_ROWS, HEAD_COLS), F32),
                        pltpu.VMEM((2, ATTN_Q_ROWS, 2 * HEAD_COLS), F32)],
        compiler_params=pltpu.CompilerParams(
            dimension_semantics=("arbitrary", "arbitrary", "arbitrary"),
            vmem_limit_bytes=VMEM_LIMIT_BYTES),
        name="diff_attn",
    )(slopes, qnorm, knorm, q, ka, kb, v)


def _sigmoid(z):
    return 0.5 * jnp.tanh(0.5 * z) + 0.5


def _out_proj_kernel(x_ref, o0_ref, o1_ref, lq1_ref, lk1_ref, lq2_ref, lk2_ref, subln_ref,
                     gpre_ref, wmid_ref, whi_ref, cw_ref, wa_ref, wc_ref, bm_ref, wo_ref, gp_ref,
                     out_ref, h_ref, u_ref, *, tiles_per_seq, lam_init):
    tm, ts, cwid = OUT_ROWS, OUT_SUB_ROWS, CONV_WIDTH

    @pl.when(pl.program_id(0) % tiles_per_seq == 0)
    def _sequence_start():
        u_ref[0:HALO_ROWS, :] = jnp.zeros((HALO_ROWS, cwid), F32)

    blocks = range(0, tm, ts)

    def proj(r, w_ref, col, cols):
        return jnp.dot(h_ref[r:r + ts, :], w_ref[:, col:col + cols], preferred_element_type=F32)

    for r in blocks:
        h_ref[r:r + ts, :] = _rms_norm_bf16(x_ref[r:r + ts, :], gpre_ref[...])

    for r in blocks:
        u = proj(r, wmid_ref, ATTN_WIDTH + cwid, cwid) * proj(r, whi_ref, 0, cwid)
        u_ref[HALO_ROWS + r:HALO_ROWS + r + ts, :] = u
    y_conv = {}
    for r in blocks:
        conv = cw_ref[CONV_K - 1:CONV_K, :] * u_ref[HALO_ROWS + r:HALO_ROWS + r + ts, :]
        for k in range(CONV_K - 1):
            first_row = HALO_ROWS + r - (CONV_K - 1 - k)
            conv = conv + cw_ref[k:k + 1, :] * u_ref[first_row:first_row + ts, :]
        z_c = proj(r, whi_ref, cwid, cwid)
        yc_in = (proj(r, wmid_ref, ATTN_WIDTH, cwid) * conv) * (z_c * _sigmoid(z_c))
        y_conv[r] = jnp.dot(yc_in.astype(BF16), wc_ref[...], preferred_element_type=F32)
    u_ref[0:HALO_ROWS, :] = u_ref[tm:tm + HALO_ROWS, :]

    g_a, g_c = {}, {}
    for r in blocks:
        g_a[r] = _sigmoid(proj(r, whi_ref, 2 * cwid, D_MODEL) + bm_ref[:, 0:D_MODEL])
        g_c[r] = _sigmoid(proj(r, whi_ref, 2 * cwid + D_MODEL, D_MODEL)
                          + bm_ref[:, D_MODEL:2 * D_MODEL])

    lam = (jnp.exp(jnp.sum(lq1_ref[...] * lk1_ref[...], axis=-1, keepdims=True))
           - jnp.exp(jnp.sum(lq2_ref[...] * lk2_ref[...], axis=-1, keepdims=True))
           + lam_init)
    subln_gain = subln_ref[...] * (1.0 - lam_init)
    y_attn = {}
    for r in blocks:
        o = o0_ref[r:r + ts, :].astype(F32) - lam * o1_ref[r:r + ts, :].astype(F32)
        heads = []
        for hd in range(ATTN_HEADS):
            oh = o[:, hd * HEAD_COLS:(hd + 1) * HEAD_COLS]
            ms = jnp.mean(oh * oh, axis=-1, keepdims=True)
            heads.append(oh * lax.rsqrt(ms + NORM_EPS) * subln_gain)
        z_a = proj(r, wmid_ref, 0, ATTN_WIDTH)
        ya_in = jnp.concatenate(heads, axis=1) * (z_a * _sigmoid(z_a))
        y_attn[r] = jnp.dot(ya_in.astype(BF16), wa_ref[...], preferred_element_type=F32)

    for r in blocks:
        y = (g_a[r] * y_attn[r] + g_c[r] * y_conv[r]).astype(BF16)
        out = jnp.dot(y, wo_ref[...], preferred_element_type=F32)
        ms = jnp.mean(out * out, axis=-1, keepdims=True)
        out_ref[r:r + ts, :] = x_ref[r:r + ts, :] + out * lax.rsqrt(ms + NORM_EPS) * gp_ref[...]


def _out_proj(x2d, o0, o1, lq1, lk1, lq2, lk2, subln_gain, g_pre, w_in_bf16, conv_w, wa, wc,
              b_merge, wo, g_post, *, seq, lam_init):
    n = x2d.shape[0]
    mid_cols = ATTN_WIDTH + 2 * CONV_WIDTH
    hi_cols = 2 * CONV_WIDTH + GATE_COLS
    assert QKV_COLS == mid_cols and QKV_COLS + mid_cols == hi_cols == IN_COLS - hi_cols
    row_block = lambda cols: pl.BlockSpec((OUT_ROWS, cols), lambda i: (i, 0))
    const = lambda shape, col_block=0: pl.BlockSpec(shape, lambda i: (0, col_block),
                                                   pipeline_mode=pl.Buffered(1))
    return pl.pallas_call(
        functools.partial(_out_proj_kernel, tiles_per_seq=seq // OUT_ROWS, lam_init=lam_init),
        grid=(n // OUT_ROWS,),
        in_specs=[
            row_block(D_MODEL), row_block(ATTN_WIDTH), row_block(ATTN_WIDTH),
            const((1, HEAD_DIM)), const((1, HEAD_DIM)), const((1, HEAD_DIM)), const((1, HEAD_DIM)),
            const((1, HEAD_COLS)), const((1, D_MODEL)),
            const((D_MODEL, mid_cols), 1), const((D_MODEL, hi_cols), 1),
            const((CONV_K, CONV_WIDTH)), const((ATTN_WIDTH, D_MODEL)),
            const((CONV_WIDTH, D_MODEL)), const((1, 2 * D_MODEL)), const((D_MODEL, D_MODEL)),
            const((1, D_MODEL)),
        ],
        out_specs=row_block(D_MODEL),
        out_shape=jax.ShapeDtypeStruct((n, D_MODEL), F32),
        scratch_shapes=[pltpu.VMEM((OUT_ROWS, D_MODEL), BF16),
                        pltpu.VMEM((HALO_ROWS + OUT_ROWS, CONV_WIDTH), F32)],
        compiler_params=pltpu.CompilerParams(
            dimension_semantics=("arbitrary",), vmem_limit_bytes=VMEM_LIMIT_BYTES),
        name="out_proj",
    )(x2d, o0, o1, lq1, lk1, lq2, lk2, subln_gain, g_pre, w_in_bf16, w_in_bf16, conv_w, wa, wc,
      b_merge, wo, g_post)


def _layer(x, layer_idx, w_in, lq1, lk1, lq2, lk2, subln_gain, conv_w, w_attn_o, w_conv_o,
           b_merge, w_out, g_pre, g_post):
    batch, seq, d = x.shape
    x2d = x.reshape(batch * seq, d)
    row = lambda a: a.reshape(1, -1).astype(F32)
    slopes = jnp.asarray([2.0 ** (-8.0 * (i + 1) / ATTN_HEADS) for i in range(ATTN_HEADS)], F32)

    q, ka, kb, v, qnorm, knorm, w_in_bf16, wa_bf16, wc_bf16, wo_bf16 = _qkv_proj(
        slopes, x2d, row(g_pre), w_in, w_attn_o, w_conv_o, w_out)
    o0, o1 = _diff_attn(q, ka, kb, v, slopes, qnorm, knorm, batch=batch, seq=seq)
    out = _out_proj(x2d, o0, o1, row(lq1), row(lk1), row(lq2), row(lk2), row(subln_gain),
                    row(g_pre), w_in_bf16, conv_w.astype(F32), wa_bf16, wc_bf16, row(b_merge),
                    wo_bf16, row(g_post), seq=seq, lam_init=_lambda_init(layer_idx))
    return out.reshape(batch, seq, d)


def kernel(x, w_in, lambda_q1, lambda_k1, lambda_q2, lambda_k2, subln_gain, conv_w, w_attn_o,
           w_conv_o, b_merge, w_out, g_pre, g_post):
    for l in range(w_in.shape[0]):
        x = _layer(x, l, w_in[l], lambda_q1[l], lambda_k1[l], lambda_q2[l], lambda_k2[l],
                   subln_gain[l], conv_w[l], w_attn_o[l], w_conv_o[l], b_merge[l], w_out[l],
                   g_pre[l], g_post[l])
    return x
```

```python
import functools
import math

import jax
import jax.numpy as jnp
from jax import lax
from jax.experimental import pallas as pl
from jax.experimental.pallas import tpu as pltpu

D_MODEL = 1024
ATTN_HEADS = 4
HEAD_DIM = 64
HEAD_COLS = 2 * HEAD_DIM
ATTN_WIDTH = ATTN_HEADS * HEAD_COLS
CONV_WIDTH = D_MODEL // 2
CONV_K = 3
NORM_EPS = 1e-6
QKV_COLS = 3 * ATTN_WIDTH
CONV_COLS = 4 * CONV_WIDTH
GATE_COLS = 2 * D_MODEL
IN_COLS = QKV_COLS + ATTN_WIDTH + CONV_COLS + GATE_COLS
MASK_VALUE = -1e30
LOG2_E = math.log2(math.e)
Q_SCALE_LOG2 = HEAD_DIM ** -0.5 * LOG2_E
ALIBI_SKIP_BITS = 160.0
NORM_SLACK = 1.01

PROJ_ROWS = 1024
PROJ_SUB_ROWS = 512
ATTN_Q_ROWS = 1024
ATTN_K_ROWS = 1024
TK_SHIFT = ATTN_K_ROWS.bit_length() - 1
ATTN_SUB_ROWS = 256
ATTN_LOOKAHEAD = 3
assert ATTN_K_ROWS == 1 << TK_SHIFT and ATTN_K_ROWS % ATTN_Q_ROWS == 0
assert PROJ_ROWS == ATTN_K_ROWS
assert ATTN_Q_ROWS % PROJ_SUB_ROWS == 0 and PROJ_SUB_ROWS % ATTN_SUB_ROWS == 0
OUT_ROWS = 1024
OUT_SUB_ROWS = 512
HALO_ROWS = 16
VMEM_LIMIT_BYTES = 56 * 1024 * 1024

BF16 = jnp.bfloat16
F32 = jnp.float32


def _lambda_init(layer_idx):
    return 0.8 - 0.6 * math.exp(-0.3 * layer_idx)


def _rms_norm_bf16(x, gain):
    ms = jnp.mean(x * x, axis=-1, keepdims=True)
    return (x * lax.rsqrt(ms + NORM_EPS) * gain).astype(BF16)


def _max_map_norm(x, lane):
    sq = x * x
    n0 = jnp.sum(jnp.where(lane < HEAD_DIM, sq, 0.0), axis=-1, keepdims=True)
    n1 = jnp.sum(jnp.where(lane >= HEAD_DIM, sq, 0.0), axis=-1, keepdims=True)
    return jnp.sqrt(jnp.max(jnp.maximum(n0, n1), axis=0, keepdims=True))[0, 0] * NORM_SLACK


def _qkv_proj_kernel(slopes_ref, x_ref, g_ref, wqkv_ref, win_ref, wa_ref, wc_ref, wo_ref,
                     q_ref, ka_ref, kb_ref, v_ref, qnorm_ref, knorm_ref, win16_ref, wa16_ref,
                     wc16_ref, wo16_ref, h_ref, biasa_ref, biasb_ref, w_ref):
    for w32_ref, w16_ref in ((win_ref, win16_ref), (wa_ref, wa16_ref), (wc_ref, wc16_ref),
                             (wo_ref, wo16_ref)):
        w16_ref[...] = w32_ref[...].astype(BF16)

    @pl.when(pl.program_id(0) == 0)
    def _first_step():
        w_ref[...] = wqkv_ref[...].astype(BF16)
        lane = lax.broadcasted_iota(jnp.int32, (PROJ_ROWS, HEAD_COLS), 1)
        row = lax.broadcasted_iota(jnp.int32, (PROJ_ROWS, HEAD_COLS), 0).astype(F32)
        for hd in range(ATTN_HEADS):
            b = row * (slopes_ref[hd] * LOG2_E)
            b1 = b.astype(BF16).astype(F32)
            b2 = (b - b1).astype(BF16).astype(F32)
            b3 = (b - b1 - b2).astype(BF16).astype(F32)
            for bias_ref, first in ((biasa_ref, HEAD_DIM), (biasb_ref, 0)):
                bias_ref[hd] = jnp.where(lane == first, b1, jnp.where(
                    lane == first + 1, b2, jnp.where(lane == first + 2, b3, 0.0)))

    lane = lax.broadcasted_iota(jnp.int32, (PROJ_SUB_ROWS, HEAD_COLS), 1)
    for r in range(0, PROJ_ROWS, PROJ_SUB_ROWS):
        rows = slice(r, r + PROJ_SUB_ROWS)
        norm_row = pl.program_id(0) * (PROJ_ROWS // PROJ_SUB_ROWS) + r // PROJ_SUB_ROWS
        h_ref[rows, :] = _rms_norm_bf16(x_ref[rows, :], g_ref[...])

        def proj(col, rows=rows):
            return jnp.dot(h_ref[rows, :], w_ref[:, col:col + ATTN_WIDTH],
                           preferred_element_type=F32)

        q = proj(0) * Q_SCALE_LOG2
        q_ref[rows, :] = q.astype(BF16)
        k = proj(ATTN_WIDTH)
        for hd in range(ATTN_HEADS):
            cols = slice(hd * HEAD_COLS, (hd + 1) * HEAD_COLS)
            kh = k[:, cols]
            ka_ref[rows, cols] = jnp.where(lane < HEAD_DIM, kh, biasa_ref[hd, rows, :]).astype(BF16)
            kb_ref[rows, cols] = jnp.where(lane >= HEAD_DIM, kh, biasb_ref[hd, rows, :]).astype(BF16)
            qnorm_ref[norm_row, hd] = _max_map_norm(q[:, cols], lane)
            knorm_ref[norm_row, hd] = _max_map_norm(kh, lane)
        v_ref[rows, :] = proj(2 * ATTN_WIDTH).astype(BF16)


def _qkv_proj(slopes, x2d, g_pre, w_in, w_attn_o, w_conv_o, w_out):
    n = x2d.shape[0]
    steps = n // PROJ_ROWS
    row_block = lambda cols: pl.BlockSpec((PROJ_ROWS, cols), lambda i: (i, 0))
    act = jax.ShapeDtypeStruct((n, ATTN_WIDTH), BF16)
    norms = jax.ShapeDtypeStruct((n // PROJ_SUB_ROWS, ATTN_HEADS), F32)
    weights = (w_in, w_attn_o, w_conv_o, w_out)
    assert all(w.shape[0] % (steps * HALO_ROWS) == 0 for w in weights)
    slab = lambda w: pl.BlockSpec((w.shape[0] // steps, w.shape[1]), lambda i: (i, 0))
    return pl.pallas_call(
        _qkv_proj_kernel,
        grid=(steps,),
        in_specs=[
            pl.BlockSpec(memory_space=pltpu.SMEM),
            row_block(D_MODEL),
            pl.BlockSpec((1, D_MODEL), lambda i: (0, 0)),
            pl.BlockSpec((D_MODEL, QKV_COLS), lambda i: (0, 0), pipeline_mode=pl.Buffered(1)),
        ] + [slab(w) for w in weights],
        out_specs=[row_block(ATTN_WIDTH), row_block(ATTN_WIDTH), row_block(ATTN_WIDTH),
                   row_block(ATTN_WIDTH), pl.BlockSpec(memory_space=pltpu.SMEM),
                   pl.BlockSpec(memory_space=pltpu.SMEM)]
        + [slab(w) for w in weights],
        out_shape=[act, act, act, act, norms, norms]
        + [jax.ShapeDtypeStruct(w.shape, BF16) for w in weights],
        scratch_shapes=[pltpu.VMEM((PROJ_ROWS, D_MODEL), BF16),
                        pltpu.VMEM((ATTN_HEADS, PROJ_ROWS, HEAD_COLS), F32),
                        pltpu.VMEM((ATTN_HEADS, PROJ_ROWS, HEAD_COLS), F32),
                        pltpu.VMEM((D_MODEL, QKV_COLS), BF16)],
        compiler_params=pltpu.CompilerParams(
            dimension_semantics=("arbitrary",), vmem_limit_bytes=VMEM_LIMIT_BYTES),
        name="qkv_proj",
    )(slopes, x2d, g_pre, w_in, w_in, w_attn_o, w_conv_o, w_out)


def _diff_attn_kernel(slopes_ref, qnorm_ref, knorm_ref, q_ref, ka_ref, kb_ref, v_ref,
                      o0_ref, o1_ref, m_ref, acc_ref):
    tq, tk, ts = ATTN_Q_ROWS, ATTN_K_ROWS, ATTN_SUB_ROWS
    th = PROJ_SUB_ROWS
    batch = pl.program_id(0)
    head = pl.program_id(1)
    qi = pl.program_id(2)
    slope = slopes_ref[head] * LOG2_E
    seq_norm_row = batch * (ka_ref.shape[0] // th)

    q = q_ref[...].astype(F32)
    qlane = lax.broadcasted_iota(jnp.int32, q.shape, 1)
    q_maps = (
        jnp.where(qlane < HEAD_DIM, q, jnp.where(qlane < HEAD_DIM + 3, 1.0, 0.0)).astype(BF16),
        jnp.where(qlane >= HEAD_DIM, q, jnp.where(qlane < 3, 1.0, 0.0)).astype(BF16),
    )
    key_refs = (ka_ref, kb_ref)

    tri_keep = (lax.broadcasted_iota(jnp.int32, (ts, ts), 1)
                <= lax.broadcasted_iota(jnp.int32, (ts, ts), 0))

    def step(key_start, diag_cols, row_stop=tq, key_lo=0):
        span_bias = (key_start - qi * tq).astype(F32) * slope
        first = diag_cols is not None
        streams = [(r, mp)
                   for r in (range(tq - ts, -ts, -ts) if first else range(0, row_stop, ts))
                   for mp in range(2)]

        def width(r):
            return tk - key_lo if diag_cols is None else diag_cols + r + ts

        def scores(r, mp):
            keys = key_refs[mp][pl.ds(key_start + key_lo, width(r)), :]
            s = lax.dot_general(q_maps[mp][r:r + ts], keys, (((1,), (1,)), ((), ())),
                                preferred_element_type=F32)
            if diag_cols is not None:
                below = width(r) - ts
                blocks = [s[:, :below]] if below else []
                s = jnp.concatenate(blocks + [jnp.where(tri_keep, s[:, below:], MASK_VALUE)],
                                    axis=1)
            return s

        def update(r, mp, s):
            m_cur = jnp.max(s, axis=-1, keepdims=True) + span_bias
            if first:
                m_new = jnp.broadcast_to(m_cur, (ts, HEAD_COLS))
            else:
                m_prev = m_ref[mp, r:r + ts]
                m_new = jnp.maximum(m_prev, m_cur)
            shift = m_new - span_bias
            p = jnp.exp2(s - jnp.concatenate([shift] * (width(r) // HEAD_COLS), axis=1))
            v_ones = jnp.concatenate([v_ref[pl.ds(key_start + key_lo, width(r)), :],
                                      jnp.ones((width(r), HEAD_COLS), BF16)], axis=1)
            pv = jnp.dot(p.astype(BF16), v_ones, preferred_element_type=F32)
            if not first:
                alpha = jnp.exp2(m_prev - m_new)
                pv = jnp.concatenate([alpha, alpha], axis=1) * acc_ref[mp, r:r + ts] + pv
            acc_ref[mp, r:r + ts] = pv
            m_ref[mp, r:r + ts] = m_new

        pending = [scores(*st) for st in streams[:ATTN_LOOKAHEAD]]
        for i, st in enumerate(streams):
            if i + ATTN_LOOKAHEAD < len(streams):
                pending.append(scores(*streams[i + ATTN_LOOKAHEAD]))
            update(*st, pending.pop(0))

    n_spans = lax.shift_right_logical(qi * tq, TK_SHIFT)
    tail_start = pl.multiple_of(n_spans * tk, tk)
    for extra in range(0, tk, tq):
        @pl.when(qi * tq - tail_start == extra)
        def _diag_step(extra=extra):
            step(tail_start, extra)

    def row_block_bound(hf):
        m_low = jnp.minimum(m_ref[0, hf * th:(hf + 1) * th], m_ref[1, hf * th:(hf + 1) * th])
        m_low = jnp.min(jnp.min(m_low, axis=0, keepdims=True), axis=1, keepdims=True)[0, 0]
        q_norm = qnorm_ref[seq_norm_row + qi * (tq // th) + hf, head]
        return q_norm, m_low - ALIBI_SKIP_BITS

    row_blocks = [row_block_bound(hf) for hf in range(tq // th)]

    def full_span(kj, carry):
        key_start = pl.multiple_of(kj * tk, tk)
        rows_needed = jnp.int32(0)
        keys_from = jnp.int32(tk)
        for kh in reversed(range(tk // th)):
            k_norm = knorm_ref[seq_norm_row + kj * (tk // th) + kh, head]
            alibi_max = (key_start + ((kh + 1) * th - 1) - qi * tq).astype(F32) * slope
            for hf, (q_norm, floor) in enumerate(row_blocks):
                live = q_norm * k_norm + alibi_max > floor
                rows_needed = jnp.where(live, jnp.maximum(rows_needed, (hf + 1) * th),
                                        rows_needed)
                keys_from = jnp.where(live, kh * th, keys_from)
        for hf in range(tq // th):
            for kh in range(tk // th):
                @pl.when(jnp.logical_and(rows_needed == (hf + 1) * th, keys_from == kh * th))
                def _fold_span(hf=hf, kh=kh):
                    step(key_start, None, row_stop=(hf + 1) * th, key_lo=kh * th)

        return carry

    lax.fori_loop(0, n_spans, full_span, 0)

    for mp, o_ref in enumerate((o0_ref, o1_ref)):
        o_ref[...] = (acc_ref[mp, :, 0:HEAD_COLS]
                      / acc_ref[mp, :, HEAD_COLS:2 * HEAD_COLS]).astype(BF16)


def _diff_attn(q, ka, kb, v, slopes, qnorm, knorm, *, batch, seq):
    nq = seq // ATTN_Q_ROWS
    seq_block = pl.BlockSpec((seq, HEAD_COLS), lambda b, h, i: (b, h))
    q_block = pl.BlockSpec((ATTN_Q_ROWS, HEAD_COLS), lambda b, h, i: (b * nq + i, h))
    return pl.pallas_call(
        _diff_attn_kernel,
        grid=(batch, ATTN_HEADS, nq),
        in_specs=[
            pl.BlockSpec(memory_space=pltpu.SMEM), pl.BlockSpec(memory_space=pltpu.SMEM),
            pl.BlockSpec(memory_space=pltpu.SMEM),
            q_block, seq_block, seq_block, seq_block,
        ],
        out_specs=[q_block, q_block],
        out_shape=[jax.ShapeDtypeStruct((batch * seq, ATTN_WIDTH), BF16)] * 2,
        scratch_shapes=[pltpu.VMEM((2, ATTN_Q_ROWS, HEAD_COLS), F32),
                        pltpu.VMEM((2, ATTN_Q_ROWS, 2 * HEAD_COLS), F32)],
        compiler_params=pltpu.CompilerParams(
            dimension_semantics=("arbitrary", "arbitrary", "arbitrary"),
            vmem_limit_bytes=VMEM_LIMIT_BYTES),
        name="diff_attn",
    )(slopes, qnorm, knorm, q, ka, kb, v)


def _sigmoid(z):
    return 0.5 * jnp.tanh(0.5 * z) + 0.5


def _out_proj_kernel(x_ref, o0_ref, o1_ref, lq1_ref, lk1_ref, lq2_ref, lk2_ref, subln_ref,
                     gpre_ref, wmid_ref, whi_ref, cw_ref, wa_ref, wc_ref, bm_ref, wo_ref, gp_ref,
                     out_ref, h_ref, u_ref, *, tiles_per_seq, lam_init):
    tm, ts, cwid = OUT_ROWS, OUT_SUB_ROWS, CONV_WIDTH

    @pl.when(pl.program_id(0) % tiles_per_seq == 0)
    def _sequence_start():
        u_ref[0:HALO_ROWS, :] = jnp.zeros((HALO_ROWS, cwid), F32)

    blocks = range(0, tm, ts)

    def proj(r, w_ref, col, cols):
        return jnp.dot(h_ref[r:r + ts, :], w_ref[:, col:col + cols], preferred_element_type=F32)

    for r in blocks:
        h_ref[r:r + ts, :] = _rms_norm_bf16(x_ref[r:r + ts, :], gpre_ref[...])

    for r in blocks:
        u = proj(r, wmid_ref, ATTN_WIDTH + cwid, cwid) * proj(r, whi_ref, 0, cwid)
        u_ref[HALO_ROWS + r:HALO_ROWS + r + ts, :] = u
    y_conv = {}
    for r in blocks:
        conv = cw_ref[CONV_K - 1:CONV_K, :] * u_ref[HALO_ROWS + r:HALO_ROWS + r + ts, :]
        for k in range(CONV_K - 1):
            first_row = HALO_ROWS + r - (CONV_K - 1 - k)
            conv = conv + cw_ref[k:k + 1, :] * u_ref[first_row:first_row + ts, :]
        z_c = proj(r, whi_ref, cwid, cwid)
        yc_in = (proj(r, wmid_ref, ATTN_WIDTH, cwid) * conv) * (z_c * _sigmoid(z_c))
        y_conv[r] = jnp.dot(yc_in.astype(BF16), wc_ref[...], preferred_element_type=F32)
    u_ref[0:HALO_ROWS, :] = u_ref[tm:tm + HALO_ROWS, :]

    g_a, g_c = {}, {}
    for r in blocks:
        g_a[r] = _sigmoid(proj(r, whi_ref, 2 * cwid, D_MODEL) + bm_ref[:, 0:D_MODEL])
        g_c[r] = _sigmoid(proj(r, whi_ref, 2 * cwid + D_MODEL, D_MODEL)
                          + bm_ref[:, D_MODEL:2 * D_MODEL])

    lam = (jnp.exp(jnp.sum(lq1_ref[...] * lk1_ref[...], axis=-1, keepdims=True))
           - jnp.exp(jnp.sum(lq2_ref[...] * lk2_ref[...], axis=-1, keepdims=True))
           + lam_init)
    subln_gain = subln_ref[...] * (1.0 - lam_init)
    y_attn = {}
    for r in blocks:
        o = o0_ref[r:r + ts, :].astype(F32) - lam * o1_ref[r:r + ts, :].astype(F32)
        heads = []
        for hd in range(ATTN_HEADS):
            oh = o[:, hd * HEAD_COLS:(hd + 1) * HEAD_COLS]
            ms = jnp.mean(oh * oh, axis=-1, keepdims=True)
            heads.append(oh * lax.rsqrt(ms + NORM_EPS) * subln_gain)
        z_a = proj(r, wmid_ref, 0, ATTN_WIDTH)
        ya_in = jnp.concatenate(heads, axis=1) * (z_a * _sigmoid(z_a))
        y_attn[r] = jnp.dot(ya_in.astype(BF16), wa_ref[...], preferred_element_type=F32)

    for r in blocks:
        y = (g_a[r] * y_attn[r] + g_c[r] * y_conv[r]).astype(BF16)
        out = jnp.dot(y, wo_ref[...], preferred_element_type=F32)
        ms = jnp.mean(out * out, axis=-1, keepdims=True)
        out_ref[r:r + ts, :] = x_ref[r:r + ts, :] + out * lax.rsqrt(ms + NORM_EPS) * gp_ref[...]


def _out_proj(x2d, o0, o1, lq1, lk1, lq2, lk2, subln_gain, g_pre, w_in_bf16, conv_w, wa, wc,
              b_merge, wo, g_post, *, seq, lam_init):
    n = x2d.shape[0]
    mid_cols = ATTN_WIDTH + 2 * CONV_WIDTH
    hi_cols = 2 * CONV_WIDTH + GATE_COLS
    assert QKV_COLS == mid_cols and QKV_COLS + mid_cols == hi_cols == IN_COLS - hi_cols
    row_block = lambda cols: pl.BlockSpec((OUT_ROWS, cols), lambda i: (i, 0))
    const = lambda shape, col_block=0: pl.BlockSpec(shape, lambda i: (0, col_block),
                                                   pipeline_mode=pl.Buffered(1))
    return pl.pallas_call(
        functools.partial(_out_proj_kernel, tiles_per_seq=seq // OUT_ROWS, lam_init=lam_init),
        grid=(n // OUT_ROWS,),
        in_specs=[
            row_block(D_MODEL), row_block(ATTN_WIDTH), row_block(ATTN_WIDTH),
            const((1, HEAD_DIM)), const((1, HEAD_DIM)), const((1, HEAD_DIM)), const((1, HEAD_DIM)),
            const((1, HEAD_COLS)), const((1, D_MODEL)),
            const((D_MODEL, mid_cols), 1), const((D_MODEL, hi_cols), 1),
            const((CONV_K, CONV_WIDTH)), const((ATTN_WIDTH, D_MODEL)),
            const((CONV_WIDTH, D_MODEL)), const((1, 2 * D_MODEL)), const((D_MODEL, D_MODEL)),
            const((1, D_MODEL)),
        ],
        out_specs=row_block(D_MODEL),
        out_shape=jax.ShapeDtypeStruct((n, D_MODEL), F32),
        scratch_shapes=[pltpu.VMEM((OUT_ROWS, D_MODEL), BF16),
                        pltpu.VMEM((HALO_ROWS + OUT_ROWS, CONV_WIDTH), F32)],
        compiler_params=pltpu.CompilerParams(
            dimension_semantics=("arbitrary",), vmem_limit_bytes=VMEM_LIMIT_BYTES),
        name="out_proj",
    )(x2d, o0, o1, lq1, lk1, lq2, lk2, subln_gain, g_pre, w_in_bf16, w_in_bf16, conv_w, wa, wc,
      b_merge, wo, g_post)


def _layer(x, layer_idx, w_in, lq1, lk1, lq2, lk2, subln_gain, conv_w, w_attn_o, w_conv_o,
           b_merge, w_out, g_pre, g_post):
    batch, seq, d = x.shape
    x2d = x.reshape(batch * seq, d)
    row = lambda a: a.reshape(1, -1).astype(F32)
    slopes = jnp.asarray([2.0 ** (-8.0 * (i + 1) / ATTN_HEADS) for i in range(ATTN_HEADS)], F32)

    q, ka, kb, v, qnorm, knorm, w_in_bf16, wa_bf16, wc_bf16, wo_bf16 = _qkv_proj(
        slopes, x2d, row(g_pre), w_in, w_attn_o, w_conv_o, w_out)
    o0, o1 = _diff_attn(q, ka, kb, v, slopes, qnorm, knorm, batch=batch, seq=seq)
    out = _out_proj(x2d, o0, o1, row(lq1), row(lk1), row(lq2), row(lk2), row(subln_gain),
                    row(g_pre), w_in_bf16, conv_w.astype(F32), wa_bf16, wc_bf16, row(b_merge),
                    wo_bf16, row(g_post), seq=seq, lam_init=_lambda_init(layer_idx))
    return out.reshape(batch, seq, d)


def kernel(x, w_in, lambda_q1, lambda_k1, lambda_q2, lambda_k2, subln_gain, conv_w, w_attn_o,
           w_conv_o, b_merge, w_out, g_pre, g_post):
    for l in range(w_in.shape[0]):
        x = _layer(x, l, w_in[l], lambda_q1[l], lambda_k1[l], lambda_q2[l], lambda_k2[l],
                   subln_gain[l], conv_w[l], w_attn_o[l], w_conv_o[l], b_merge[l], w_out[l],
                   g_pre[l], g_post[l])
    return x
```

```python
import functools
import math

import jax
import jax.numpy as jnp
from jax import lax
from jax.experimental import pallas as pl
from jax.experimental.pallas import tpu as pltpu

D_MODEL = 1024
ATTN_HEADS = 4
HEAD_DIM = 64
HEAD_COLS = 2 * HEAD_DIM
ATTN_WIDTH = ATTN_HEADS * HEAD_COLS
CONV_WIDTH = D_MODEL // 2
CONV_K = 3
NORM_EPS = 1e-6
QKV_COLS = 3 * ATTN_WIDTH
CONV_COLS = 4 * CONV_WIDTH
GATE_COLS = 2 * D_MODEL
IN_COLS = QKV_COLS + ATTN_WIDTH + CONV_COLS + GATE_COLS
MASK_VALUE = -1e30
LOG2_E = math.log2(math.e)
Q_SCALE_LOG2 = HEAD_DIM ** -0.5 * LOG2_E
ALIBI_SKIP_BITS = 160.0
NORM_SLACK = 1.01

PROJ_ROWS = 1024
PROJ_SUB_ROWS = 512
ATTN_Q_ROWS = 1024
ATTN_K_ROWS = 1024
TK_SHIFT = ATTN_K_ROWS.bit_length() - 1
ATTN_SUB_ROWS = 256
ATTN_LOOKAHEAD = 3
assert ATTN_K_ROWS == 1 << TK_SHIFT and ATTN_K_ROWS % ATTN_Q_ROWS == 0
assert PROJ_ROWS == ATTN_K_ROWS
assert ATTN_Q_ROWS % PROJ_SUB_ROWS == 0 and PROJ_SUB_ROWS % ATTN_SUB_ROWS == 0
OUT_ROWS = 1024
OUT_SUB_ROWS = 512
HALO_ROWS = 16
VMEM_LIMIT_BYTES = 56 * 1024 * 1024

BF16 = jnp.bfloat16
F32 = jnp.float32


def _lambda_init(layer_idx):
    return 0.8 - 0.6 * math.exp(-0.3 * layer_idx)


def _rms_norm_bf16(x, gain):
    ms = jnp.mean(x * x, axis=-1, keepdims=True)
    return (x * lax.rsqrt(ms + NORM_EPS) * gain).astype(BF16)


def _max_map_norm(x, lane):
    sq = x * x
    n0 = jnp.sum(jnp.where(lane < HEAD_DIM, sq, 0.0), axis=-1, keepdims=True)
    n1 = jnp.sum(jnp.where(lane >= HEAD_DIM, sq, 0.0), axis=-1, keepdims=True)
    return jnp.sqrt(jnp.max(jnp.maximum(n0, n1), axis=0, keepdims=True))[0, 0] * NORM_SLACK


def _qkv_proj_kernel(slopes_ref, x_ref, g_ref, wqkv_ref, q_ref, ka_ref, kb_ref, v_ref,
                     qnorm_ref, knorm_ref, h_ref, biasa_ref, biasb_ref, w_ref):
    @pl.when(pl.program_id(0) == 0)
    def _first_step():
        w_ref[...] = wqkv_ref[...].astype(BF16)
        lane = lax.broadcasted_iota(jnp.int32, (PROJ_ROWS, HEAD_COLS), 1)
        row = lax.broadcasted_iota(jnp.int32, (PROJ_ROWS, HEAD_COLS), 0).astype(F32)
        for hd in range(ATTN_HEADS):
            b = row * (slopes_ref[hd] * LOG2_E)
            b1 = b.astype(BF16).astype(F32)
            b2 = (b - b1).astype(BF16).astype(F32)
            b3 = (b - b1 - b2).astype(BF16).astype(F32)
            for bias_ref, first in ((biasa_ref, HEAD_DIM), (biasb_ref, 0)):
                bias_ref[hd] = jnp.where(lane == first, b1, jnp.where(
                    lane == first + 1, b2, jnp.where(lane == first + 2, b3, 0.0)))

    lane = lax.broadcasted_iota(jnp.int32, (PROJ_SUB_ROWS, HEAD_COLS), 1)
    for r in range(0, PROJ_ROWS, PROJ_SUB_ROWS):
        rows = slice(r, r + PROJ_SUB_ROWS)
        norm_row = pl.program_id(0) * (PROJ_ROWS // PROJ_SUB_ROWS) + r // PROJ_SUB_ROWS
        h_ref[rows, :] = _rms_norm_bf16(x_ref[rows, :], g_ref[...])

        def proj(col, rows=rows):
            return jnp.dot(h_ref[rows, :], w_ref[:, col:col + ATTN_WIDTH],
                           preferred_element_type=F32)

        q = proj(0) * Q_SCALE_LOG2
        q_ref[rows, :] = q.astype(BF16)
        k = proj(ATTN_WIDTH)
        for hd in range(ATTN_HEADS):
            cols = slice(hd * HEAD_COLS, (hd + 1) * HEAD_COLS)
            kh = k[:, cols]
            ka_ref[rows, cols] = jnp.where(lane < HEAD_DIM, kh, biasa_ref[hd, rows, :]).astype(BF16)
            kb_ref[rows, cols] = jnp.where(lane >= HEAD_DIM, kh, biasb_ref[hd, rows, :]).astype(BF16)
            qnorm_ref[norm_row, hd] = _max_map_norm(q[:, cols], lane)
            knorm_ref[norm_row, hd] = _max_map_norm(kh, lane)
        v_ref[rows, :] = proj(2 * ATTN_WIDTH).astype(BF16)


def _qkv_proj(slopes, x2d, g_pre, w_in):
    n = x2d.shape[0]
    row_block = lambda cols: pl.BlockSpec((PROJ_ROWS, cols), lambda i: (i, 0))
    act = jax.ShapeDtypeStruct((n, ATTN_WIDTH), BF16)
    norms = jax.ShapeDtypeStruct((n // PROJ_SUB_ROWS, ATTN_HEADS), F32)
    return pl.pallas_call(
        _qkv_proj_kernel,
        grid=(n // PROJ_ROWS,),
        in_specs=[
            pl.BlockSpec(memory_space=pltpu.SMEM),
            row_block(D_MODEL),
            pl.BlockSpec((1, D_MODEL), lambda i: (0, 0)),
            pl.BlockSpec((D_MODEL, QKV_COLS), lambda i: (0, 0), pipeline_mode=pl.Buffered(1)),
        ],
        out_specs=[row_block(ATTN_WIDTH), row_block(ATTN_WIDTH), row_block(ATTN_WIDTH),
                   row_block(ATTN_WIDTH), pl.BlockSpec(memory_space=pltpu.SMEM),
                   pl.BlockSpec(memory_space=pltpu.SMEM)],
        out_shape=[act, act, act, act, norms, norms],
        scratch_shapes=[pltpu.VMEM((PROJ_ROWS, D_MODEL), BF16),
                        pltpu.VMEM((ATTN_HEADS, PROJ_ROWS, HEAD_COLS), F32),
                        pltpu.VMEM((ATTN_HEADS, PROJ_ROWS, HEAD_COLS), F32),
                        pltpu.VMEM((D_MODEL, QKV_COLS), BF16)],
        compiler_params=pltpu.CompilerParams(
            dimension_semantics=("arbitrary",), vmem_limit_bytes=VMEM_LIMIT_BYTES),
        name="qkv_proj",
    )(slopes, x2d, g_pre, w_in)


def _diff_attn_kernel(slopes_ref, qnorm_ref, knorm_ref, q_ref, ka_ref, kb_ref, v_ref,
                      win_ref, wa_ref, wc_ref, wo_ref, o0_ref, o1_ref, win16_ref, wa16_ref,
                      wc16_ref, wo16_ref, m_ref, acc_ref):
    for w32_ref, w16_ref in ((win_ref, win16_ref), (wa_ref, wa16_ref), (wc_ref, wc16_ref),
                             (wo_ref, wo16_ref)):
        w16_ref[...] = w32_ref[...].astype(BF16)

    tq, tk, ts = ATTN_Q_ROWS, ATTN_K_ROWS, ATTN_SUB_ROWS
    th = PROJ_SUB_ROWS
    batch = pl.program_id(0)
    head = pl.program_id(1)
    qi = pl.program_id(2)
    slope = slopes_ref[head] * LOG2_E
    seq_norm_row = batch * (ka_ref.shape[0] // th)

    q = q_ref[...].astype(F32)
    qlane = lax.broadcasted_iota(jnp.int32, q.shape, 1)
    q_maps = (
        jnp.where(qlane < HEAD_DIM, q, jnp.where(qlane < HEAD_DIM + 3, 1.0, 0.0)).astype(BF16),
        jnp.where(qlane >= HEAD_DIM, q, jnp.where(qlane < 3, 1.0, 0.0)).astype(BF16),
    )
    key_refs = (ka_ref, kb_ref)

    tri_keep = (lax.broadcasted_iota(jnp.int32, (ts, ts), 1)
                <= lax.broadcasted_iota(jnp.int32, (ts, ts), 0))

    def step(key_start, diag_cols, row_stop=tq, key_lo=0):
        span_bias = (key_start - qi * tq).astype(F32) * slope
        first = diag_cols is not None
        streams = [(r, mp)
                   for r in (range(tq - ts, -ts, -ts) if first else range(0, row_stop, ts))
                   for mp in range(2)]

        def width(r):
            return tk - key_lo if diag_cols is None else diag_cols + r + ts

        def scores(r, mp):
            keys = key_refs[mp][pl.ds(key_start + key_lo, width(r)), :]
            s = lax.dot_general(q_maps[mp][r:r + ts], keys, (((1,), (1,)), ((), ())),
                                preferred_element_type=F32)
            if diag_cols is not None:
                below = width(r) - ts
                blocks = [s[:, :below]] if below else []
                s = jnp.concatenate(blocks + [jnp.where(tri_keep, s[:, below:], MASK_VALUE)],
                                    axis=1)
            return s

        def update(r, mp, s):
            m_cur = jnp.max(s, axis=-1, keepdims=True) + span_bias
            if first:
                m_new = jnp.broadcast_to(m_cur, (ts, HEAD_COLS))
            else:
                m_prev = m_ref[mp, r:r + ts]
                m_new = jnp.maximum(m_prev, m_cur)
            shift = m_new - span_bias
            p = jnp.exp2(s - jnp.concatenate([shift] * (width(r) // HEAD_COLS), axis=1))
            v_ones = jnp.concatenate([v_ref[pl.ds(key_start + key_lo, width(r)), :],
                                      jnp.ones((width(r), HEAD_COLS), BF16)], axis=1)
            pv = jnp.dot(p.astype(BF16), v_ones, preferred_element_type=F32)
            if not first:
                alpha = jnp.exp2(m_prev - m_new)
                pv = jnp.concatenate([alpha, alpha], axis=1) * acc_ref[mp, r:r + ts] + pv
            acc_ref[mp, r:r + ts] = pv
            m_ref[mp, r:r + ts] = m_new

        pending = [scores(*st) for st in streams[:ATTN_LOOKAHEAD]]
        for i, st in enumerate(streams):
            if i + ATTN_LOOKAHEAD < len(streams):
                pending.append(scores(*streams[i + ATTN_LOOKAHEAD]))
            update(*st, pending.pop(0))

    n_spans = lax.shift_right_logical(qi * tq, TK_SHIFT)
    tail_start = pl.multiple_of(n_spans * tk, tk)
    for extra in range(0, tk, tq):
        @pl.when(qi * tq - tail_start == extra)
        def _diag_step(extra=extra):
            step(tail_start, extra)

    def row_block_bound(hf):
        m_low = jnp.minimum(m_ref[0, hf * th:(hf + 1) * th], m_ref[1, hf * th:(hf + 1) * th])
        m_low = jnp.min(jnp.min(m_low, axis=0, keepdims=True), axis=1, keepdims=True)[0, 0]
        q_norm = qnorm_ref[seq_norm_row + qi * (tq // th) + hf, head]
        return q_norm, m_low - ALIBI_SKIP_BITS

    row_blocks = [row_block_bound(hf) for hf in range(tq // th)]

    def full_span(kj, carry):
        key_start = pl.multiple_of(kj * tk, tk)
        rows_needed = jnp.int32(0)
        keys_from = jnp.int32(tk)
        for kh in reversed(range(tk // th)):
            k_norm = knorm_ref[seq_norm_row + kj * (tk // th) + kh, head]
            alibi_max = (key_start + ((kh + 1) * th - 1) - qi * tq).astype(F32) * slope
            for hf, (q_norm, floor) in enumerate(row_blocks):
                live = q_norm * k_norm + alibi_max > floor
                rows_needed = jnp.where(live, jnp.maximum(rows_needed, (hf + 1) * th),
                                        rows_needed)
                keys_from = jnp.where(live, kh * th, keys_from)
        for hf in range(tq // th):
            for kh in range(tk // th):
                @pl.when(jnp.logical_and(rows_needed == (hf + 1) * th, keys_from == kh * th))
                def _fold_span(hf=hf, kh=kh):
                    step(key_start, None, row_stop=(hf + 1) * th, key_lo=kh * th)

        return carry

    lax.fori_loop(0, n_spans, full_span, 0)

    for mp, o_ref in enumerate((o0_ref, o1_ref)):
        o_ref[...] = (acc_ref[mp, :, 0:HEAD_COLS]
                      / acc_ref[mp, :, HEAD_COLS:2 * HEAD_COLS]).astype(BF16)


def _diff_attn(q, ka, kb, v, slopes, qnorm, knorm, weights, *, batch, seq):
    nq = seq // ATTN_Q_ROWS
    steps = batch * ATTN_HEADS * nq

    def slab(w):
        assert w.shape[0] % HALO_ROWS == 0 and w.shape[0] // HALO_ROWS <= steps
        last = w.shape[0] // HALO_ROWS - 1
        return pl.BlockSpec(
            (HALO_ROWS, w.shape[1]),
            lambda b, h, i: (jnp.minimum((b * ATTN_HEADS + h) * nq + i, last), 0))

    seq_block = pl.BlockSpec((seq, HEAD_COLS), lambda b, h, i: (b, h))
    q_block = pl.BlockSpec((ATTN_Q_ROWS, HEAD_COLS), lambda b, h, i: (b * nq + i, h))
    return pl.pallas_call(
        _diff_attn_kernel,
        grid=(batch, ATTN_HEADS, nq),
        in_specs=[
            pl.BlockSpec(memory_space=pltpu.SMEM), pl.BlockSpec(memory_space=pltpu.SMEM),
            pl.BlockSpec(memory_space=pltpu.SMEM),
            q_block, seq_block, seq_block, seq_block,
        ] + [slab(w) for w in weights],
        out_specs=[q_block, q_block] + [slab(w) for w in weights],
        out_shape=[jax.ShapeDtypeStruct((batch * seq, ATTN_WIDTH), BF16)] * 2
        + [jax.ShapeDtypeStruct(w.shape, BF16) for w in weights],
        scratch_shapes=[pltpu.VMEM((2, ATTN_Q_ROWS, HEAD_COLS), F32),
                        pltpu.VMEM((2, ATTN_Q_ROWS, 2 * HEAD_COLS), F32)],
        compiler_params=pltpu.CompilerParams(
            dimension_semantics=("arbitrary", "arbitrary", "arbitrary"),
            vmem_limit_bytes=VMEM_LIMIT_BYTES),
        name="diff_attn",
    )(slopes, qnorm, knorm, q, ka, kb, v, *weights)


def _sigmoid(z):
    return 0.5 * jnp.tanh(0.5 * z) + 0.5


def _out_proj_kernel(x_ref, o0_ref, o1_ref, lq1_ref, lk1_ref, lq2_ref, lk2_ref, subln_ref,
                     gpre_ref, wmid_ref, whi_ref, cw_ref, wa_ref, wc_ref, bm_ref, wo_ref, gp_ref,
                     out_ref, h_ref, u_ref, *, tiles_per_seq, lam_init):
    tm, ts, cwid = OUT_ROWS, OUT_SUB_ROWS, CONV_WIDTH

    @pl.when(pl.program_id(0) % tiles_per_seq == 0)
    def _sequence_start():
        u_ref[0:HALO_ROWS, :] = jnp.zeros((HALO_ROWS, cwid), F32)

    blocks = range(0, tm, ts)

    def proj(r, w_ref, col, cols):
        return jnp.dot(h_ref[r:r + ts, :], w_ref[:, col:col + cols], preferred_element_type=F32)

    for r in blocks:
        h_ref[r:r + ts, :] = _rms_norm_bf16(x_ref[r:r + ts, :], gpre_ref[...])

    for r in blocks:
        u = proj(r, wmid_ref, ATTN_WIDTH + cwid, cwid) * proj(r, whi_ref, 0, cwid)
        u_ref[HALO_ROWS + r:HALO_ROWS + r + ts, :] = u
    y_conv = {}
    for r in blocks:
        conv = cw_ref[CONV_K - 1:CONV_K, :] * u_ref[HALO_ROWS + r:HALO_ROWS + r + ts, :]
        for k in range(CONV_K - 1):
            first_row = HALO_ROWS + r - (CONV_K - 1 - k)
            conv = conv + cw_ref[k:k + 1, :] * u_ref[first_row:first_row + ts, :]
        z_c = proj(r, whi_ref, cwid, cwid)
        yc_in = (proj(r, wmid_ref, ATTN_WIDTH, cwid) * conv) * (z_c * _sigmoid(z_c))
        y_conv[r] = jnp.dot(yc_in.astype(BF16), wc_ref[...], preferred_element_type=F32)
    u_ref[0:HALO_ROWS, :] = u_ref[tm:tm + HALO_ROWS, :]

    g_a, g_c = {}, {}
    for r in blocks:
        g_a[r] = _sigmoid(proj(r, whi_ref, 2 * cwid, D_MODEL) + bm_ref[:, 0:D_MODEL])
        g_c[r] = _sigmoid(proj(r, whi_ref, 2 * cwid + D_MODEL, D_MODEL)
                          + bm_ref[:, D_MODEL:2 * D_MODEL])

    lam = (jnp.exp(jnp.sum(lq1_ref[...] * lk1_ref[...], axis=-1, keepdims=True))
           - jnp.exp(jnp.sum(lq2_ref[...] * lk2_ref[...], axis=-1, keepdims=True))
           + lam_init)
    subln_gain = subln_ref[...] * (1.0 - lam_init)
    y_attn = {}
    for r in blocks:
        o = o0_ref[r:r + ts, :].astype(F32) - lam * o1_ref[r:r + ts, :].astype(F32)
        heads = []
        for hd in range(ATTN_HEADS):
            oh = o[:, hd * HEAD_COLS:(hd + 1) * HEAD_COLS]
            ms = jnp.mean(oh * oh, axis=-1, keepdims=True)
            heads.append(oh * lax.rsqrt(ms + NORM_EPS) * subln_gain)
        z_a = proj(r, wmid_ref, 0, ATTN_WIDTH)
        ya_in = jnp.concatenate(heads, axis=1) * (z_a * _sigmoid(z_a))
        y_attn[r] = jnp.dot(ya_in.astype(BF16), wa_ref[...], preferred_element_type=F32)

    for r in blocks:
        y = (g_a[r] * y_attn[r] + g_c[r] * y_conv[r]).astype(BF16)
        out = jnp.dot(y, wo_ref[...], preferred_element_type=F32)
        ms = jnp.mean(out * out, axis=-1, keepdims=True)
        out_ref[r:r + ts, :] = x_ref[r:r + ts, :] + out * lax.rsqrt(ms + NORM_EPS) * gp_ref[...]


def _out_proj(x2d, o0, o1, lq1, lk1, lq2, lk2, subln_gain, g_pre, w_in_bf16, conv_w, wa, wc,
              b_merge, wo, g_post, *, seq, lam_init):
    n = x2d.shape[0]
    mid_cols = ATTN_WIDTH + 2 * CONV_WIDTH
    hi_cols = 2 * CONV_WIDTH + GATE_COLS
    assert QKV_COLS == mid_cols and QKV_COLS + mid_cols == hi_cols == IN_COLS - hi_cols
    row_block = lambda cols: pl.BlockSpec((OUT_ROWS, cols), lambda i: (i, 0))
    const = lambda shape, col_block=0: pl.BlockSpec(shape, lambda i: (0, col_block),
                                                   pipeline_mode=pl.Buffered(1))
    return pl.pallas_call(
        functools.partial(_out_proj_kernel, tiles_per_seq=seq // OUT_ROWS, lam_init=lam_init),
        grid=(n // OUT_ROWS,),
        in_specs=[
            row_block(D_MODEL), row_block(ATTN_WIDTH), row_block(ATTN_WIDTH),
            const((1, HEAD_DIM)), const((1, HEAD_DIM)), const((1, HEAD_DIM)), const((1, HEAD_DIM)),
            const((1, HEAD_COLS)), const((1, D_MODEL)),
            const((D_MODEL, mid_cols), 1), const((D_MODEL, hi_cols), 1),
            const((CONV_K, CONV_WIDTH)), const((ATTN_WIDTH, D_MODEL)),
            const((CONV_WIDTH, D_MODEL)), const((1, 2 * D_MODEL)), const((D_MODEL, D_MODEL)),
            const((1, D_MODEL)),
        ],
        out_specs=row_block(D_MODEL),
        out_shape=jax.ShapeDtypeStruct((n, D_MODEL), F32),
        scratch_shapes=[pltpu.VMEM((OUT_ROWS, D_MODEL), BF16),
                        pltpu.VMEM((HALO_ROWS + OUT_ROWS, CONV_WIDTH), F32)],
        compiler_params=pltpu.CompilerParams(
            dimension_semantics=("arbitrary",), vmem_limit_bytes=VMEM_LIMIT_BYTES),
        name="out_proj",
    )(x2d, o0, o1, lq1, lk1, lq2, lk2, subln_gain, g_pre, w_in_bf16, w_in_bf16, conv_w, wa, wc,
      b_merge, wo, g_post)


def _layer(x, layer_idx, w_in, lq1, lk1, lq2, lk2, subln_gain, conv_w, w_attn_o, w_conv_o,
           b_merge, w_out, g_pre, g_post):
    batch, seq, d = x.shape
    x2d = x.reshape(batch * seq, d)
    row = lambda a: a.reshape(1, -1).astype(F32)
    slopes = jnp.asarray([2.0 ** (-8.0 * (i + 1) / ATTN_HEADS) for i in range(ATTN_HEADS)], F32)

    q, ka, kb, v, qnorm, knorm = _qkv_proj(slopes, x2d, row(g_pre), w_in)
    o0, o1, w_in_bf16, wa_bf16, wc_bf16, wo_bf16 = _diff_attn(
        q, ka, kb, v, slopes, qnorm, knorm, (w_in, w_attn_o, w_conv_o, w_out),
        batch=batch, seq=seq)
    out = _out_proj(x2d, o0, o1, row(lq1), row(lk1), row(lq2), row(lk2), row(subln_gain),
                    row(g_pre), w_in_bf16, conv_w.astype(F32), wa_bf16, wc_bf16, row(b_merge),
                    wo_bf16, row(g_post), seq=seq, lam_init=_lambda_init(layer_idx))
    return out.reshape(batch, seq, d)


def kernel(x, w_in, lambda_q1, lambda_k1, lambda_q2, lambda_k2, subln_gain, conv_w, w_attn_o,
           w_conv_o, b_merge, w_out, g_pre, g_post):
    for l in range(w_in.shape[0]):
        x = _layer(x, l, w_in[l], lambda_q1[l], lambda_k1[l], lambda_q2[l], lambda_k2[l],
                   subln_gain[l], conv_w[l], w_attn_o[l], w_conv_o[l], b_merge[l], w_out[l],
                   g_pre[l], g_post[l])
    return x
```

```python
import functools
import math

import jax
import jax.numpy as jnp
from jax import lax
from jax.experimental import pallas as pl
from jax.experimental.pallas import tpu as pltpu

D_MODEL = 1024
ATTN_HEADS = 4
HEAD_DIM = 64
HEAD_COLS = 2 * HEAD_DIM
ATTN_WIDTH = ATTN_HEADS * HEAD_COLS
CONV_WIDTH = D_MODEL // 2
CONV_K = 3
NORM_EPS = 1e-6
QKV_COLS = 3 * ATTN_WIDTH
CONV_COLS = 4 * CONV_WIDTH
GATE_COLS = 2 * D_MODEL
IN_COLS = QKV_COLS + ATTN_WIDTH + CONV_COLS + GATE_COLS
MASK_VALUE = -1e30
LOG2_E = math.log2(math.e)
Q_SCALE_LOG2 = HEAD_DIM ** -0.5 * LOG2_E
ALIBI_SKIP_BITS = 160.0
NORM_SLACK = 1.01

PROJ_ROWS = 1024
PROJ_SUB_ROWS = 512
ATTN_Q_ROWS = 1024
ATTN_K_ROWS = 1024
TK_SHIFT = ATTN_K_ROWS.bit_length() - 1
ATTN_SUB_ROWS = 256
ATTN_LOOKAHEAD = 3
assert ATTN_K_ROWS == 1 << TK_SHIFT and ATTN_K_ROWS % ATTN_Q_ROWS == 0
assert PROJ_ROWS == ATTN_K_ROWS
assert ATTN_Q_ROWS % PROJ_SUB_ROWS == 0 and PROJ_SUB_ROWS % ATTN_SUB_ROWS == 0
OUT_ROWS = 1024
OUT_SUB_ROWS = 512
HALO_ROWS = 16
VMEM_LIMIT_BYTES = 56 * 1024 * 1024

BF16 = jnp.bfloat16
F32 = jnp.float32


def _lambda_init(layer_idx):
    return 0.8 - 0.6 * math.exp(-0.3 * layer_idx)


def _rms_norm_bf16(x, gain):
    ms = jnp.mean(x * x, axis=-1, keepdims=True)
    return (x * lax.rsqrt(ms + NORM_EPS) * gain).astype(BF16)


def _max_map_norm(x, lane):
    sq = x * x
    n0 = jnp.sum(jnp.where(lane < HEAD_DIM, sq, 0.0), axis=-1, keepdims=True)
    n1 = jnp.sum(jnp.where(lane >= HEAD_DIM, sq, 0.0), axis=-1, keepdims=True)
    return jnp.sqrt(jnp.max(jnp.maximum(n0, n1), axis=0, keepdims=True))[0, 0] * NORM_SLACK


def _qkv_proj_kernel(slopes_ref, x_ref, g_ref, wqkv_ref, win_ref, wa_ref, wc_ref, wo_ref,
                     q_ref, ka_ref, kb_ref, v_ref, qnorm_ref, knorm_ref, win16_ref, wa16_ref,
                     wc16_ref, wo16_ref, h_ref, biasa_ref, biasb_ref, w_ref):
    for w32_ref, w16_ref in ((win_ref, win16_ref), (wa_ref, wa16_ref), (wc_ref, wc16_ref),
                             (wo_ref, wo16_ref)):
        w16_ref[...] = w32_ref[...].astype(BF16)

    @pl.when(pl.program_id(0) == 0)
    def _first_step():
        w_ref[...] = wqkv_ref[...].astype(BF16)
        lane = lax.broadcasted_iota(jnp.int32, (PROJ_ROWS, HEAD_COLS), 1)
        row = lax.broadcasted_iota(jnp.int32, (PROJ_ROWS, HEAD_COLS), 0).astype(F32)
        for hd in range(ATTN_HEADS):
            b = row * (slopes_ref[hd] * LOG2_E)
            b1 = b.astype(BF16).astype(F32)
            b2 = (b - b1).astype(BF16).astype(F32)
            b3 = (b - b1 - b2).astype(BF16).astype(F32)
            for bias_ref, first in ((biasa_ref, HEAD_DIM), (biasb_ref, 0)):
                bias_ref[hd] = jnp.where(lane == first, b1, jnp.where(
                    lane == first + 1, b2, jnp.where(lane == first + 2, b3, 0.0)))

    lane = lax.broadcasted_iota(jnp.int32, (PROJ_SUB_ROWS, HEAD_COLS), 1)
    for r in range(0, PROJ_ROWS, PROJ_SUB_ROWS):
        rows = slice(r, r + PROJ_SUB_ROWS)
        norm_row = pl.program_id(0) * (PROJ_ROWS // PROJ_SUB_ROWS) + r // PROJ_SUB_ROWS
        h_ref[rows, :] = _rms_norm_bf16(x_ref[rows, :], g_ref[...])

        def proj(col, rows=rows):
            return jnp.dot(h_ref[rows, :], w_ref[:, col:col + ATTN_WIDTH],
                           preferred_element_type=F32)

        q = proj(0) * Q_SCALE_LOG2
        q_ref[rows, :] = q.astype(BF16)
        k = proj(ATTN_WIDTH)
        for hd in range(ATTN_HEADS):
            cols = slice(hd * HEAD_COLS, (hd + 1) * HEAD_COLS)
            kh = k[:, cols]
            ka_ref[rows, cols] = jnp.where(lane < HEAD_DIM, kh, biasa_ref[hd, rows, :]).astype(BF16)
            kb_ref[rows, cols] = jnp.where(lane >= HEAD_DIM, kh, biasb_ref[hd, rows, :]).astype(BF16)
            qnorm_ref[norm_row, hd] = _max_map_norm(q[:, cols], lane)
            knorm_ref[norm_row, hd] = _max_map_norm(kh, lane)
        v_ref[rows, :] = proj(2 * ATTN_WIDTH).astype(BF16)


def _qkv_proj(slopes, x2d, g_pre, w_in, w_attn_o, w_conv_o, w_out):
    n = x2d.shape[0]
    steps = n // PROJ_ROWS
    row_block = lambda cols: pl.BlockSpec((PROJ_ROWS, cols), lambda i: (i, 0))
    act = jax.ShapeDtypeStruct((n, ATTN_WIDTH), BF16)
    norms = jax.ShapeDtypeStruct((n // PROJ_SUB_ROWS, ATTN_HEADS), F32)
    weights = (w_in, w_attn_o, w_conv_o, w_out)
    assert all(w.shape[0] % (steps * HALO_ROWS) == 0 for w in weights)
    slab = lambda w: pl.BlockSpec((w.shape[0] // steps, w.shape[1]), lambda i: (i, 0))
    return pl.pallas_call(
        _qkv_proj_kernel,
        grid=(steps,),
        in_specs=[
            pl.BlockSpec(memory_space=pltpu.SMEM),
            row_block(D_MODEL),
            pl.BlockSpec((1, D_MODEL), lambda i: (0, 0)),
            pl.BlockSpec((D_MODEL, QKV_COLS), lambda i: (0, 0), pipeline_mode=pl.Buffered(1)),
        ] + [slab(w) for w in weights],
        out_specs=[row_block(ATTN_WIDTH), row_block(ATTN_WIDTH), row_block(ATTN_WIDTH),
                   row_block(ATTN_WIDTH), pl.BlockSpec(memory_space=pltpu.SMEM),
                   pl.BlockSpec(memory_space=pltpu.SMEM)]
        + [slab(w) for w in weights],
        out_shape=[act, act, act, act, norms, norms]
        + [jax.ShapeDtypeStruct(w.shape, BF16) for w in weights],
        scratch_shapes=[pltpu.VMEM((PROJ_ROWS, D_MODEL), BF16),
                        pltpu.VMEM((ATTN_HEADS, PROJ_ROWS, HEAD_COLS), F32),
                        pltpu.VMEM((ATTN_HEADS, PROJ_ROWS, HEAD_COLS), F32),
                        pltpu.VMEM((D_MODEL, QKV_COLS), BF16)],
        compiler_params=pltpu.CompilerParams(
            dimension_semantics=("arbitrary",), vmem_limit_bytes=VMEM_LIMIT_BYTES),
        name="qkv_proj",
    )(slopes, x2d, g_pre, w_in, w_in, w_attn_o, w_conv_o, w_out)


def _diff_attn_kernel(slopes_ref, qnorm_ref, knorm_ref, q_ref, ka_ref, kb_ref, v_ref,
                      o0_ref, o1_ref, m_ref, acc_ref):
    tq, tk, ts = ATTN_Q_ROWS, ATTN_K_ROWS, ATTN_SUB_ROWS
    th = PROJ_SUB_ROWS
    batch = pl.program_id(0)
    head = pl.program_id(1)
    qi = pl.program_id(2)
    slope = slopes_ref[head] * LOG2_E
    seq_norm_row = batch * (ka_ref.shape[0] // th)

    q = q_ref[...].astype(F32)
    qlane = lax.broadcasted_iota(jnp.int32, q.shape, 1)
    q_maps = (
        jnp.where(qlane < HEAD_DIM, q, jnp.where(qlane < HEAD_DIM + 3, 1.0, 0.0)).astype(BF16),
        jnp.where(qlane >= HEAD_DIM, q, jnp.where(qlane < 3, 1.0, 0.0)).astype(BF16),
    )
    key_refs = (ka_ref, kb_ref)

    tri_keep = (lax.broadcasted_iota(jnp.int32, (ts, ts), 1)
                <= lax.broadcasted_iota(jnp.int32, (ts, ts), 0))

    def step(key_start, diag_cols, row_stop=tq, key_lo=0):
        span_bias = (key_start - qi * tq).astype(F32) * slope
        first = diag_cols is not None
        streams = [(r, mp)
                   for r in (range(tq - ts, -ts, -ts) if first else range(0, row_stop, ts))
                   for mp in range(2)]

        def width(r):
            return tk - key_lo if diag_cols is None else diag_cols + r + ts

        def scores(r, mp):
            keys = key_refs[mp][pl.ds(key_start + key_lo, width(r)), :]
            s = lax.dot_general(q_maps[mp][r:r + ts], keys, (((1,), (1,)), ((), ())),
                                preferred_element_type=F32)
            if diag_cols is not None:
                below = width(r) - ts
                blocks = [s[:, :below]] if below else []
                s = jnp.concatenate(blocks + [jnp.where(tri_keep, s[:, below:], MASK_VALUE)],
                                    axis=1)
            return s

        probs = {}

        def update(r, mp, s):
            m_cur = jnp.max(s, axis=-1, keepdims=True) + span_bias
            if first:
                m_new = jnp.broadcast_to(m_cur, (ts, HEAD_COLS))
                alpha = None
            else:
                m_prev = m_ref[mp, r:r + ts]
                m_new = jnp.maximum(m_prev, m_cur)
                alpha = jnp.exp2(m_prev - m_new)
            shift = m_new - span_bias
            p = jnp.exp2(s - jnp.concatenate([shift] * (width(r) // HEAD_COLS), axis=1))
            m_ref[mp, r:r + ts] = m_new
            probs[mp] = (p.astype(BF16), alpha)
            if mp == 0:
                return
            v_ones = jnp.concatenate([v_ref[pl.ds(key_start + key_lo, width(r)), :],
                                      jnp.ones((width(r), HEAD_COLS), BF16)], axis=1)
            pv_pair = jnp.dot(jnp.concatenate([probs[0][0], probs[1][0]], axis=0), v_ones,
                              preferred_element_type=F32)
            for mq in range(2):
                pv = pv_pair[mq * ts:(mq + 1) * ts]
                alpha = probs[mq][1]
                if alpha is not None:
                    pv = jnp.concatenate([alpha, alpha], axis=1) * acc_ref[mq, r:r + ts] + pv
                acc_ref[mq, r:r + ts] = pv

        pending = [scores(*st) for st in streams[:ATTN_LOOKAHEAD]]
        for i, st in enumerate(streams):
            if i + ATTN_LOOKAHEAD < len(streams):
                pending.append(scores(*streams[i + ATTN_LOOKAHEAD]))
            update(*st, pending.pop(0))

    n_spans = lax.shift_right_logical(qi * tq, TK_SHIFT)
    tail_start = pl.multiple_of(n_spans * tk, tk)
    for extra in range(0, tk, tq):
        @pl.when(qi * tq - tail_start == extra)
        def _diag_step(extra=extra):
            step(tail_start, extra)

    def row_block_bound(hf):
        m_low = jnp.minimum(m_ref[0, hf * th:(hf + 1) * th], m_ref[1, hf * th:(hf + 1) * th])
        m_low = jnp.min(jnp.min(m_low, axis=0, keepdims=True), axis=1, keepdims=True)[0, 0]
        q_norm = qnorm_ref[seq_norm_row + qi * (tq // th) + hf, head]
        return q_norm, m_low - ALIBI_SKIP_BITS

    row_blocks = [row_block_bound(hf) for hf in range(tq // th)]

    def full_span(kj, carry):
        key_start = pl.multiple_of(kj * tk, tk)
        rows_needed = jnp.int32(0)
        keys_from = jnp.int32(tk)
        for kh in reversed(range(tk // th)):
            k_norm = knorm_ref[seq_norm_row + kj * (tk // th) + kh, head]
            alibi_max = (key_start + ((kh + 1) * th - 1) - qi * tq).astype(F32) * slope
            for hf, (q_norm, floor) in enumerate(row_blocks):
                live = q_norm * k_norm + alibi_max > floor
                rows_needed = jnp.where(live, jnp.maximum(rows_needed, (hf + 1) * th),
                                        rows_needed)
                keys_from = jnp.where(live, kh * th, keys_from)
        for hf in range(tq // th):
            for kh in range(tk // th):
                @pl.when(jnp.logical_and(rows_needed == (hf + 1) * th, keys_from == kh * th))
                def _fold_span(hf=hf, kh=kh):
                    step(key_start, None, row_stop=(hf + 1) * th, key_lo=kh * th)

        return carry

    lax.fori_loop(0, n_spans, full_span, 0)

    for mp, o_ref in enumerate((o0_ref, o1_ref)):
        o_ref[...] = (acc_ref[mp, :, 0:HEAD_COLS]
                      / acc_ref[mp, :, HEAD_COLS:2 * HEAD_COLS]).astype(BF16)


def _diff_attn(q, ka, kb, v, slopes, qnorm, knorm, *, batch, seq):
    nq = seq // ATTN_Q_ROWS
    seq_block = pl.BlockSpec((seq, HEAD_COLS), lambda b, h, i: (b, h))
    q_block = pl.BlockSpec((ATTN_Q_ROWS, HEAD_COLS), lambda b, h, i: (b * nq + i, h))
    return pl.pallas_call(
        _diff_attn_kernel,
        grid=(batch, ATTN_HEADS, nq),
        in_specs=[
            pl.BlockSpec(memory_space=pltpu.SMEM), pl.BlockSpec(memory_space=pltpu.SMEM),
            pl.BlockSpec(memory_space=pltpu.SMEM),
            q_block, seq_block, seq_block, seq_block,
        ],
        out_specs=[q_block, q_block],
        out_shape=[jax.ShapeDtypeStruct((batch * seq, ATTN_WIDTH), BF16)] * 2,
        scratch_shapes=[pltpu.VMEM((2, ATTN_Q_ROWS, HEAD_COLS), F32),
                        pltpu.VMEM((2, ATTN_Q_ROWS, 2 * HEAD_COLS), F32)],
        compiler_params=pltpu.CompilerParams(
            dimension_semantics=("arbitrary", "arbitrary", "arbitrary"),
            vmem_limit_bytes=VMEM_LIMIT_BYTES),
        name="diff_attn",
    )(slopes, qnorm, knorm, q, ka, kb, v)


def _sigmoid(z):
    return 0.5 * jnp.tanh(0.5 * z) + 0.5


def _out_proj_kernel(x_ref, o0_ref, o1_ref, lq1_ref, lk1_ref, lq2_ref, lk2_ref, subln_ref,
                     gpre_ref, wmid_ref, whi_ref, cw_ref, wa_ref, wc_ref, bm_ref, wo_ref, gp_ref,
                     out_ref, h_ref, u_ref, *, tiles_per_seq, lam_init):
    tm, ts, cwid = OUT_ROWS, OUT_SUB_ROWS, CONV_WIDTH

    @pl.when(pl.program_id(0) % tiles_per_seq == 0)
    def _sequence_start():
        u_ref[0:HALO_ROWS, :] = jnp.zeros((HALO_ROWS, cwid), F32)

    blocks = range(0, tm, ts)

    def proj(r, w_ref, col, cols):
        return jnp.dot(h_ref[r:r + ts, :], w_ref[:, col:col + cols], preferred_element_type=F32)

    for r in blocks:
        h_ref[r:r + ts, :] = _rms_norm_bf16(x_ref[r:r + ts, :], gpre_ref[...])

    for r in blocks:
        u = proj(r, wmid_ref, ATTN_WIDTH + cwid, cwid) * proj(r, whi_ref, 0, cwid)
        u_ref[HALO_ROWS + r:HALO_ROWS + r + ts, :] = u
    y_conv = {}
    for r in blocks:
        conv = cw_ref[CONV_K - 1:CONV_K, :] * u_ref[HALO_ROWS + r:HALO_ROWS + r + ts, :]
        for k in range(CONV_K - 1):
            first_row = HALO_ROWS + r - (CONV_K - 1 - k)
            conv = conv + cw_ref[k:k + 1, :] * u_ref[first_row:first_row + ts, :]
        z_c = proj(r, whi_ref, cwid, cwid)
        yc_in = (proj(r, wmid_ref, ATTN_WIDTH, cwid) * conv) * (z_c * _sigmoid(z_c))
        y_conv[r] = jnp.dot(yc_in.astype(BF16), wc_ref[...], preferred_element_type=F32)
    u_ref[0:HALO_ROWS, :] = u_ref[tm:tm + HALO_ROWS, :]

    g_a, g_c = {}, {}
    for r in blocks:
        g_a[r] = _sigmoid(proj(r, whi_ref, 2 * cwid, D_MODEL) + bm_ref[:, 0:D_MODEL])
        g_c[r] = _sigmoid(proj(r, whi_ref, 2 * cwid + D_MODEL, D_MODEL)
                          + bm_ref[:, D_MODEL:2 * D_MODEL])

    lam = (jnp.exp(jnp.sum(lq1_ref[...] * lk1_ref[...], axis=-1, keepdims=True))
           - jnp.exp(jnp.sum(lq2_ref[...] * lk2_ref[...], axis=-1, keepdims=True))
           + lam_init)
    subln_gain = subln_ref[...] * (1.0 - lam_init)
    y_attn = {}
    for r in blocks:
        o = o0_ref[r:r + ts, :].astype(F32) - lam * o1_ref[r:r + ts, :].astype(F32)
        heads = []
        for hd in range(ATTN_HEADS):
            oh = o[:, hd * HEAD_COLS:(hd + 1) * HEAD_COLS]
            ms = jnp.mean(oh * oh, axis=-1, keepdims=True)
            heads.append(oh * lax.rsqrt(ms + NORM_EPS) * subln_gain)
        z_a = proj(r, wmid_ref, 0, ATTN_WIDTH)
        ya_in = jnp.concatenate(heads, axis=1) * (z_a * _sigmoid(z_a))
        y_attn[r] = jnp.dot(ya_in.astype(BF16), wa_ref[...], preferred_element_type=F32)

    for r in blocks:
        y = (g_a[r] * y_attn[r] + g_c[r] * y_conv[r]).astype(BF16)
        out = jnp.dot(y, wo_ref[...], preferred_element_type=F32)
        ms = jnp.mean(out * out, axis=-1, keepdims=True)
        out_ref[r:r + ts, :] = x_ref[r:r + ts, :] + out * lax.rsqrt(ms + NORM_EPS) * gp_ref[...]


def _out_proj(x2d, o0, o1, lq1, lk1, lq2, lk2, subln_gain, g_pre, w_in_bf16, conv_w, wa, wc,
              b_merge, wo, g_post, *, seq, lam_init):
    n = x2d.shape[0]
    mid_cols = ATTN_WIDTH + 2 * CONV_WIDTH
    hi_cols = 2 * CONV_WIDTH + GATE_COLS
    assert QKV_COLS == mid_cols and QKV_COLS + mid_cols == hi_cols == IN_COLS - hi_cols
    row_block = lambda cols: pl.BlockSpec((OUT_ROWS, cols), lambda i: (i, 0))
    const = lambda shape, col_block=0: pl.BlockSpec(shape, lambda i: (0, col_block),
                                                   pipeline_mode=pl.Buffered(1))
    return pl.pallas_call(
        functools.partial(_out_proj_kernel, tiles_per_seq=seq // OUT_ROWS, lam_init=lam_init),
        grid=(n // OUT_ROWS,),
        in_specs=[
            row_block(D_MODEL), row_block(ATTN_WIDTH), row_block(ATTN_WIDTH),
            const((1, HEAD_DIM)), const((1, HEAD_DIM)), const((1, HEAD_DIM)), const((1, HEAD_DIM)),
            const((1, HEAD_COLS)), const((1, D_MODEL)),
            const((D_MODEL, mid_cols), 1), const((D_MODEL, hi_cols), 1),
            const((CONV_K, CONV_WIDTH)), const((ATTN_WIDTH, D_MODEL)),
            const((CONV_WIDTH, D_MODEL)), const((1, 2 * D_MODEL)), const((D_MODEL, D_MODEL)),
            const((1, D_MODEL)),
        ],
        out_specs=row_block(D_MODEL),
        out_shape=jax.ShapeDtypeStruct((n, D_MODEL), F32),
        scratch_shapes=[pltpu.VMEM((OUT_ROWS, D_MODEL), BF16),
                        pltpu.VMEM((HALO_ROWS + OUT_ROWS, CONV_WIDTH), F32)],
        compiler_params=pltpu.CompilerParams(
            dimension_semantics=("arbitrary",), vmem_limit_bytes=VMEM_LIMIT_BYTES),
        name="out_proj",
    )(x2d, o0, o1, lq1, lk1, lq2, lk2, subln_gain, g_pre, w_in_bf16, w_in_bf16, conv_w, wa, wc,
      b_merge, wo, g_post)


def _layer(x, layer_idx, w_in, lq1, lk1, lq2, lk2, subln_gain, conv_w, w_attn_o, w_conv_o,
           b_merge, w_out, g_pre, g_post):
    batch, seq, d = x.shape
    x2d = x.reshape(batch * seq, d)
    row = lambda a: a.reshape(1, -1).astype(F32)
    slopes = jnp.asarray([2.0 ** (-8.0 * (i + 1) / ATTN_HEADS) for i in range(ATTN_HEADS)], F32)

    q, ka, kb, v, qnorm, knorm, w_in_bf16, wa_bf16, wc_bf16, wo_bf16 = _qkv_proj(
        slopes, x2d, row(g_pre), w_in, w_attn_o, w_conv_o, w_out)
    o0, o1 = _diff_attn(q, ka, kb, v, slopes, qnorm, knorm, batch=batch, seq=seq)
    out = _out_proj(x2d, o0, o1, row(lq1), row(lk1), row(lq2), row(lk2), row(subln_gain),
                    row(g_pre), w_in_bf16, conv_w.astype(F32), wa_bf16, wc_bf16, row(b_merge),
                    wo_bf16, row(g_post), seq=seq, lam_init=_lambda_init(layer_idx))
    return out.reshape(batch, seq, d)


def kernel(x, w_in, lambda_q1, lambda_k1, lambda_q2, lambda_k2, subln_gain, conv_w, w_attn_o,
           w_conv_o, b_merge, w_out, g_pre, g_post):
    for l in range(w_in.shape[0]):
        x = _layer(x, l, w_in[l], lambda_q1[l], lambda_k1[l], lambda_q2[l], lambda_k2[l],
                   subln_gain[l], conv_w[l], w_attn_o[l], w_conv_o[l], b_merge[l], w_out[l],
                   g_pre[l], g_post[l])
    return x
```

```python
import functools
import math

import jax
import jax.numpy as jnp
from jax import lax
from jax.experimental import pallas as pl
from jax.experimental.pallas import tpu as pltpu

D_MODEL = 1024
ATTN_HEADS = 4
HEAD_DIM = 64
HEAD_COLS = 2 * HEAD_DIM
ATTN_WIDTH = ATTN_HEADS * HEAD_COLS
CONV_WIDTH = D_MODEL // 2
CONV_K = 3
NORM_EPS = 1e-6
QKV_COLS = 3 * ATTN_WIDTH
CONV_COLS = 4 * CONV_WIDTH
GATE_COLS = 2 * D_MODEL
IN_COLS = QKV_COLS + ATTN_WIDTH + CONV_COLS + GATE_COLS
MASK_VALUE = -1e30
LOG2_E = math.log2(math.e)
Q_SCALE_LOG2 = HEAD_DIM ** -0.5 * LOG2_E
ALIBI_SKIP_BITS = 160.0
NORM_SLACK = 1.01

PROJ_ROWS = 1024
PROJ_SUB_ROWS = 512
ATTN_Q_ROWS = 1024
ATTN_K_ROWS = 1024
TK_SHIFT = ATTN_K_ROWS.bit_length() - 1
ATTN_SUB_ROWS = 256
ATTN_LOOKAHEAD = 3
assert ATTN_K_ROWS == 1 << TK_SHIFT and ATTN_K_ROWS % ATTN_Q_ROWS == 0
assert PROJ_ROWS == ATTN_K_ROWS
assert ATTN_Q_ROWS % PROJ_SUB_ROWS == 0 and PROJ_SUB_ROWS % ATTN_SUB_ROWS == 0
OUT_ROWS = 1024
OUT_SUB_ROWS = 512
HALO_ROWS = 16
VMEM_LIMIT_BYTES = 56 * 1024 * 1024

BF16 = jnp.bfloat16
F32 = jnp.float32


def _lambda_init(layer_idx):
    return 0.8 - 0.6 * math.exp(-0.3 * layer_idx)


def _rms_norm_bf16(x, gain):
    ms = jnp.mean(x * x, axis=-1, keepdims=True)
    return (x * lax.rsqrt(ms + NORM_EPS) * gain).astype(BF16)


def _max_map_norm(x, lane):
    sq = x * x
    n0 = jnp.sum(jnp.where(lane < HEAD_DIM, sq, 0.0), axis=-1, keepdims=True)
    n1 = jnp.sum(jnp.where(lane >= HEAD_DIM, sq, 0.0), axis=-1, keepdims=True)
    return jnp.sqrt(jnp.max(jnp.maximum(n0, n1), axis=0, keepdims=True))[0, 0] * NORM_SLACK


def _qkv_proj_kernel(slopes_ref, x_ref, g_ref, wqkv_ref, win_ref, wa_ref, wc_ref, wo_ref,
                     q_ref, ka_ref, kb_ref, v_ref, qnorm_ref, knorm_ref, win16_ref, wa16_ref,
                     wc16_ref, wo16_ref, h_ref, biasa_ref, biasb_ref, w_ref):
    for w32_ref, w16_ref in ((win_ref, win16_ref), (wa_ref, wa16_ref), (wc_ref, wc16_ref),
                             (wo_ref, wo16_ref)):
        w16_ref[...] = w32_ref[...].astype(BF16)

    @pl.when(pl.program_id(0) == 0)
    def _first_step():
        w_ref[...] = wqkv_ref[...].astype(BF16)
        lane = lax.broadcasted_iota(jnp.int32, (PROJ_ROWS, HEAD_COLS), 1)
        row = lax.broadcasted_iota(jnp.int32, (PROJ_ROWS, HEAD_COLS), 0).astype(F32)
        for hd in range(ATTN_HEADS):
            b = row * (slopes_ref[hd] * LOG2_E)
            b1 = b.astype(BF16).astype(F32)
            b2 = (b - b1).astype(BF16).astype(F32)
            b3 = (b - b1 - b2).astype(BF16).astype(F32)
            for bias_ref, first in ((biasa_ref, HEAD_DIM), (biasb_ref, 0)):
                bias_ref[hd] = jnp.where(lane == first, b1, jnp.where(
                    lane == first + 1, b2, jnp.where(lane == first + 2, b3, 0.0)))

    lane = lax.broadcasted_iota(jnp.int32, (PROJ_SUB_ROWS, HEAD_COLS), 1)
    for r in range(0, PROJ_ROWS, PROJ_SUB_ROWS):
        rows = slice(r, r + PROJ_SUB_ROWS)
        norm_row = pl.program_id(0) * (PROJ_ROWS // PROJ_SUB_ROWS) + r // PROJ_SUB_ROWS
        h_ref[rows, :] = _rms_norm_bf16(x_ref[rows, :], g_ref[...])

        def proj(col, rows=rows):
            return jnp.dot(h_ref[rows, :], w_ref[:, col:col + ATTN_WIDTH],
                           preferred_element_type=F32)

        q = proj(0) * Q_SCALE_LOG2
        q_ref[rows, :] = q.astype(BF16)
        k = proj(ATTN_WIDTH)
        for hd in range(ATTN_HEADS):
            cols = slice(hd * HEAD_COLS, (hd + 1) * HEAD_COLS)
            kh = k[:, cols]
            ka_ref[rows, cols] = jnp.where(lane < HEAD_DIM, kh, biasa_ref[hd, rows, :]).astype(BF16)
            kb_ref[rows, cols] = jnp.where(lane >= HEAD_DIM, kh, biasb_ref[hd, rows, :]).astype(BF16)
            qnorm_ref[norm_row, hd] = _max_map_norm(q[:, cols], lane)
            knorm_ref[norm_row, hd] = _max_map_norm(kh, lane)
        v_ref[rows, :] = proj(2 * ATTN_WIDTH).astype(BF16)


def _qkv_proj(slopes, x2d, g_pre, w_in, w_attn_o, w_conv_o, w_out):
    n = x2d.shape[0]
    steps = n // PROJ_ROWS
    row_block = lambda cols: pl.BlockSpec((PROJ_ROWS, cols), lambda i: (i, 0))
    act = jax.ShapeDtypeStruct((n, ATTN_WIDTH), BF16)
    norms = jax.ShapeDtypeStruct((n // PROJ_SUB_ROWS, ATTN_HEADS), F32)
    weights = (w_in, w_attn_o, w_conv_o, w_out)
    assert all(w.shape[0] % (steps * HALO_ROWS) == 0 for w in weights)
    slab = lambda w: pl.BlockSpec((w.shape[0] // steps, w.shape[1]), lambda i: (i, 0))
    return pl.pallas_call(
        _qkv_proj_kernel,
        grid=(steps,),
        in_specs=[
            pl.BlockSpec(memory_space=pltpu.SMEM),
            row_block(D_MODEL),
            pl.BlockSpec((1, D_MODEL), lambda i: (0, 0)),
            pl.BlockSpec((D_MODEL, QKV_COLS), lambda i: (0, 0), pipeline_mode=pl.Buffered(1)),
        ] + [slab(w) for w in weights],
        out_specs=[row_block(ATTN_WIDTH), row_block(ATTN_WIDTH), row_block(ATTN_WIDTH),
                   row_block(ATTN_WIDTH), pl.BlockSpec(memory_space=pltpu.SMEM),
                   pl.BlockSpec(memory_space=pltpu.SMEM)]
        + [slab(w) for w in weights],
        out_shape=[act, act, act, act, norms, norms]
        + [jax.ShapeDtypeStruct(w.shape, BF16) for w in weights],
        scratch_shapes=[pltpu.VMEM((PROJ_ROWS, D_MODEL), BF16),
                        pltpu.VMEM((ATTN_HEADS, PROJ_ROWS, HEAD_COLS), F32),
                        pltpu.VMEM((ATTN_HEADS, PROJ_ROWS, HEAD_COLS), F32),
                        pltpu.VMEM((D_MODEL, QKV_COLS), BF16)],
        compiler_params=pltpu.CompilerParams(
            dimension_semantics=("arbitrary",), vmem_limit_bytes=VMEM_LIMIT_BYTES),
        name="qkv_proj",
    )(slopes, x2d, g_pre, w_in, w_in, w_attn_o, w_conv_o, w_out)


def _diff_attn_kernel(slopes_ref, qnorm_ref, knorm_ref, q_ref, ka_ref, kb_ref, v_ref,
                      o0_ref, o1_ref, m_ref, acc_ref):
    tq, tk, ts = ATTN_Q_ROWS, ATTN_K_ROWS, ATTN_SUB_ROWS
    th = PROJ_SUB_ROWS
    batch = pl.program_id(0)
    head = pl.program_id(1)
    qi = pl.program_id(2)
    slope = slopes_ref[head] * LOG2_E
    seq_norm_row = batch * (ka_ref.shape[0] // th)

    q = q_ref[...].astype(F32)
    qlane = lax.broadcasted_iota(jnp.int32, q.shape, 1)
    q_maps = (
        jnp.where(qlane < HEAD_DIM, q, jnp.where(qlane < HEAD_DIM + 3, 1.0, 0.0)).astype(BF16),
        jnp.where(qlane >= HEAD_DIM, q, jnp.where(qlane < 3, 1.0, 0.0)).astype(BF16),
    )
    key_refs = (ka_ref, kb_ref)

    tri_keep = (lax.broadcasted_iota(jnp.int32, (ts, ts), 1)
                <= lax.broadcasted_iota(jnp.int32, (ts, ts), 0))

    def step(key_start, diag_cols, row_stop=tq, key_lo=0):
        span_bias = (key_start - qi * tq).astype(F32) * slope
        first = diag_cols is not None
        streams = [(r, mp)
                   for r in (range(tq - ts, -ts, -ts) if first else range(0, row_stop, ts))
                   for mp in range(2)]

        def width(r):
            return tk - key_lo if diag_cols is None else diag_cols + r + ts

        def scores(r, mp):
            keys = key_refs[mp][pl.ds(key_start + key_lo, width(r)), :]
            s = lax.dot_general(q_maps[mp][r:r + ts], keys, (((1,), (1,)), ((), ())),
                                preferred_element_type=F32)
            if diag_cols is not None:
                below = width(r) - ts
                blocks = [s[:, :below]] if below else []
                s = jnp.concatenate(blocks + [jnp.where(tri_keep, s[:, below:], MASK_VALUE)],
                                    axis=1)
            return s

        probs = {}

        def update(r, mp, s):
            m_cur = jnp.max(s, axis=-1, keepdims=True) + span_bias
            if first:
                m_new = jnp.broadcast_to(m_cur, (ts, HEAD_COLS))
                alpha = None
            else:
                m_prev = m_ref[mp, r:r + ts]
                m_new = jnp.maximum(m_prev, m_cur)
                alpha = jnp.exp2(m_prev - m_new)
            shift = m_new - span_bias
            p = jnp.exp2(s - jnp.concatenate([shift] * (width(r) // HEAD_COLS), axis=1))
            m_ref[mp, r:r + ts] = m_new
            probs[mp] = (p.astype(BF16), alpha)
            if mp == 0:
                return
            v_ones = jnp.concatenate([v_ref[pl.ds(key_start + key_lo, width(r)), :],
                                      jnp.ones((width(r), HEAD_COLS), BF16)], axis=1)
            pv_pair = jnp.dot(jnp.concatenate([probs[0][0], probs[1][0]], axis=0), v_ones,
                              preferred_element_type=F32)
            for mq in range(2):
                pv = pv_pair[mq * ts:(mq + 1) * ts]
                alpha = probs[mq][1]
                if alpha is not None:
                    pv = jnp.concatenate([alpha, alpha], axis=1) * acc_ref[mq, r:r + ts] + pv
                acc_ref[mq, r:r + ts] = pv

        pending = [scores(*st) for st in streams[:ATTN_LOOKAHEAD]]
        for i, st in enumerate(streams):
            if i + ATTN_LOOKAHEAD < len(streams):
                pending.append(scores(*streams[i + ATTN_LOOKAHEAD]))
            update(*st, pending.pop(0))

    n_spans = lax.shift_right_logical(qi * tq, TK_SHIFT)
    tail_start = pl.multiple_of(n_spans * tk, tk)
    for extra in range(0, tk, tq):
        @pl.when(qi * tq - tail_start == extra)
        def _diag_step(extra=extra):
            step(tail_start, extra)

    def row_block_bound(hf):
        m_low = jnp.minimum(m_ref[0, hf * th:(hf + 1) * th], m_ref[1, hf * th:(hf + 1) * th])
        m_low = jnp.min(jnp.min(m_low, axis=0, keepdims=True), axis=1, keepdims=True)[0, 0]
        q_norm = qnorm_ref[seq_norm_row + qi * (tq // th) + hf, head]
        return q_norm, m_low - ALIBI_SKIP_BITS

    row_blocks = [row_block_bound(hf) for hf in range(tq // th)]

    def full_span(kj, carry):
        key_start = pl.multiple_of(kj * tk, tk)
        rows_needed = jnp.int32(0)
        keys_from = jnp.int32(tk)
        for kh in reversed(range(tk // th)):
            k_norm = knorm_ref[seq_norm_row + kj * (tk // th) + kh, head]
            alibi_max = (key_start + ((kh + 1) * th - 1) - qi * tq).astype(F32) * slope
            for hf, (q_norm, floor) in enumerate(row_blocks):
                live = q_norm * k_norm + alibi_max > floor
                rows_needed = jnp.where(live, jnp.maximum(rows_needed, (hf + 1) * th),
                                        rows_needed)
                keys_from = jnp.where(live, kh * th, keys_from)
        for hf in range(tq // th):
            for kh in range(tk // th):
                @pl.when(jnp.logical_and(rows_needed == (hf + 1) * th, keys_from == kh * th))
                def _fold_span(hf=hf, kh=kh):
                    step(key_start, None, row_stop=(hf + 1) * th, key_lo=kh * th)

        return carry

    lax.fori_loop(0, n_spans, full_span, 0)

    for mp, o_ref in enumerate((o0_ref, o1_ref)):
        o_ref[...] = (acc_ref[mp, :, 0:HEAD_COLS]
                      / acc_ref[mp, :, HEAD_COLS:2 * HEAD_COLS]).astype(BF16)


def _diff_attn(q, ka, kb, v, slopes, qnorm, knorm, *, batch, seq):
    nq = seq // ATTN_Q_ROWS
    seq_block = pl.BlockSpec((seq, HEAD_COLS), lambda b, h, i: (b, h))
    q_block = pl.BlockSpec((ATTN_Q_ROWS, HEAD_COLS), lambda b, h, i: (b * nq + i, h))
    return pl.pallas_call(
        _diff_attn_kernel,
        grid=(batch, ATTN_HEADS, nq),
        in_specs=[
            pl.BlockSpec(memory_space=pltpu.SMEM), pl.BlockSpec(memory_space=pltpu.SMEM),
            pl.BlockSpec(memory_space=pltpu.SMEM),
            q_block, seq_block, seq_block, seq_block,
        ],
        out_specs=[q_block, q_block],
        out_shape=[jax.ShapeDtypeStruct((batch * seq, ATTN_WIDTH), BF16)] * 2,
        scratch_shapes=[pltpu.VMEM((2, ATTN_Q_ROWS, HEAD_COLS), F32),
                        pltpu.VMEM((2, ATTN_Q_ROWS, 2 * HEAD_COLS), F32)],
        compiler_params=pltpu.CompilerParams(
            dimension_semantics=("arbitrary", "arbitrary", "arbitrary"),
            vmem_limit_bytes=VMEM_LIMIT_BYTES),
        name="diff_attn",
    )(slopes, qnorm, knorm, q, ka, kb, v)


def _sigmoid(z):
    return 0.5 * jnp.tanh(0.5 * z) + 0.5


def _out_proj_kernel(x_ref, o0_ref, o1_ref, lq1_ref, lk1_ref, lq2_ref, lk2_ref, subln_ref,
                     gpre_ref, wmid_ref, whi_ref, cw_ref, wa_ref, wc_ref, bm_ref, wo_ref, gp_ref,
                     out_ref, h_ref, u_ref, *, tiles_per_seq, lam_init):
    tm, ts, cwid = OUT_ROWS, OUT_SUB_ROWS, CONV_WIDTH

    @pl.when(pl.program_id(0) % tiles_per_seq == 0)
    def _sequence_start():
        u_ref[0:HALO_ROWS, :] = jnp.zeros((HALO_ROWS, cwid), F32)

    blocks = range(0, tm, ts)

    def proj(r, w_ref, col, cols):
        return jnp.dot(h_ref[r:r + ts, :], w_ref[:, col:col + cols], preferred_element_type=F32)

    for r in blocks:
        h_ref[r:r + ts, :] = _rms_norm_bf16(x_ref[r:r + ts, :], gpre_ref[...])

    for r in blocks:
        u = proj(r, wmid_ref, ATTN_WIDTH + cwid, cwid) * proj(r, whi_ref, 0, cwid)
        u_ref[HALO_ROWS + r:HALO_ROWS + r + ts, :] = u
    y_conv = {}
    for r in blocks:
        conv = cw_ref[CONV_K - 1:CONV_K, :] * u_ref[HALO_ROWS + r:HALO_ROWS + r + ts, :]
        for k in range(CONV_K - 1):
            first_row = HALO_ROWS + r - (CONV_K - 1 - k)
            conv = conv + cw_ref[k:k + 1, :] * u_ref[first_row:first_row + ts, :]
        z_c = proj(r, whi_ref, cwid, cwid)
        yc_in = (proj(r, wmid_ref, ATTN_WIDTH, cwid) * conv) * (z_c * _sigmoid(z_c))
        y_conv[r] = jnp.dot(yc_in.astype(BF16), wc_ref[...], preferred_element_type=F32)
    u_ref[0:HALO_ROWS, :] = u_ref[tm:tm + HALO_ROWS, :]

    g_a = {}
    for r in blocks:
        g_a[r] = _sigmoid(proj(r, whi_ref, 2 * cwid, D_MODEL) + bm_ref[:, 0:D_MODEL])

    lam = (jnp.exp(jnp.sum(lq1_ref[...] * lk1_ref[...], axis=-1, keepdims=True))
           - jnp.exp(jnp.sum(lq2_ref[...] * lk2_ref[...], axis=-1, keepdims=True))
           + lam_init)
    subln_gain = subln_ref[...] * (1.0 - lam_init)
    y_merged = {}
    for r in blocks:
        o = o0_ref[r:r + ts, :].astype(F32) - lam * o1_ref[r:r + ts, :].astype(F32)
        heads = []
        for hd in range(ATTN_HEADS):
            oh = o[:, hd * HEAD_COLS:(hd + 1) * HEAD_COLS]
            ms = jnp.mean(oh * oh, axis=-1, keepdims=True)
            heads.append(oh * lax.rsqrt(ms + NORM_EPS) * subln_gain)
        z_a = proj(r, wmid_ref, 0, ATTN_WIDTH)
        ya_in = jnp.concatenate(heads, axis=1) * (z_a * _sigmoid(z_a))
        y_attn = jnp.dot(ya_in.astype(BF16), wa_ref[...], preferred_element_type=F32)
        y_merged[r] = g_a[r] * y_attn

    for r in blocks:
        g_c = _sigmoid(proj(r, whi_ref, 2 * cwid + D_MODEL, D_MODEL)
                       + bm_ref[:, D_MODEL:2 * D_MODEL])
        y_merged[r] = (y_merged[r] + g_c * y_conv[r]).astype(BF16)
    for r in blocks:
        out = jnp.dot(y_merged[r], wo_ref[...], preferred_element_type=F32)
        ms = jnp.mean(out * out, axis=-1, keepdims=True)
        out_ref[r:r + ts, :] = x_ref[r:r + ts, :] + out * lax.rsqrt(ms + NORM_EPS) * gp_ref[...]


def _out_proj(x2d, o0, o1, lq1, lk1, lq2, lk2, subln_gain, g_pre, w_in_bf16, conv_w, wa, wc,
              b_merge, wo, g_post, *, seq, lam_init):
    n = x2d.shape[0]
    mid_cols = ATTN_WIDTH + 2 * CONV_WIDTH
    hi_cols = 2 * CONV_WIDTH + GATE_COLS
    assert QKV_COLS == mid_cols and QKV_COLS + mid_cols == hi_cols == IN_COLS - hi_cols
    row_block = lambda cols: pl.BlockSpec((OUT_ROWS, cols), lambda i: (i, 0))
    const = lambda shape, col_block=0: pl.BlockSpec(shape, lambda i: (0, col_block),
                                                   pipeline_mode=pl.Buffered(1))
    return pl.pallas_call(
        functools.partial(_out_proj_kernel, tiles_per_seq=seq // OUT_ROWS, lam_init=lam_init),
        grid=(n // OUT_ROWS,),
        in_specs=[
            row_block(D_MODEL), row_block(ATTN_WIDTH), row_block(ATTN_WIDTH),
            const((1, HEAD_DIM)), const((1, HEAD_DIM)), const((1, HEAD_DIM)), const((1, HEAD_DIM)),
            const((1, HEAD_COLS)), const((1, D_MODEL)),
            const((D_MODEL, mid_cols), 1), const((D_MODEL, hi_cols), 1),
            const((CONV_K, CONV_WIDTH)), const((ATTN_WIDTH, D_MODEL)),
            const((CONV_WIDTH, D_MODEL)), const((1, 2 * D_MODEL)), const((D_MODEL, D_MODEL)),
            const((1, D_MODEL)),
        ],
        out_specs=row_block(D_MODEL),
        out_shape=jax.ShapeDtypeStruct((n, D_MODEL), F32),
        scratch_shapes=[pltpu.VMEM((OUT_ROWS, D_MODEL), BF16),
                        pltpu.VMEM((HALO_ROWS + OUT_ROWS, CONV_WIDTH), F32)],
        compiler_params=pltpu.CompilerParams(
            dimension_semantics=("arbitrary",), vmem_limit_bytes=VMEM_LIMIT_BYTES),
        name="out_proj",
    )(x2d, o0, o1, lq1, lk1, lq2, lk2, subln_gain, g_pre, w_in_bf16, w_in_bf16, conv_w, wa, wc,
      b_merge, wo, g_post)


def _layer(x, layer_idx, w_in, lq1, lk1, lq2, lk2, subln_gain, conv_w, w_attn_o, w_conv_o,
           b_merge, w_out, g_pre, g_post):
    batch, seq, d = x.shape
    x2d = x.reshape(batch * seq, d)
    row = lambda a: a.reshape(1, -1).astype(F32)
    slopes = jnp.asarray([2.0 ** (-8.0 * (i + 1) / ATTN_HEADS) for i in range(ATTN_HEADS)], F32)

    q, ka, kb, v, qnorm, knorm, w_in_bf16, wa_bf16, wc_bf16, wo_bf16 = _qkv_proj(
        slopes, x2d, row(g_pre), w_in, w_attn_o, w_conv_o, w_out)
    o0, o1 = _diff_attn(q, ka, kb, v, slopes, qnorm, knorm, batch=batch, seq=seq)
    out = _out_proj(x2d, o0, o1, row(lq1), row(lk1), row(lq2), row(lk2), row(subln_gain),
                    row(g_pre), w_in_bf16, conv_w.astype(F32), wa_bf16, wc_bf16, row(b_merge),
                    wo_bf16, row(g_post), seq=seq, lam_init=_lambda_init(layer_idx))
    return out.reshape(batch, seq, d)


def kernel(x, w_in, lambda_q1, lambda_k1, lambda_q2, lambda_k2, subln_gain, conv_w, w_attn_o,
           w_conv_o, b_merge, w_out, g_pre, g_post):
    for l in range(w_in.shape[0]):
        x = _layer(x, l, w_in[l], lambda_q1[l], lambda_k1[l], lambda_q2[l], lambda_k2[l],
                   subln_gain[l], conv_w[l], w_attn_o[l], w_conv_o[l], b_merge[l], w_out[l],
                   g_pre[l], g_post[l])
    return x
```

```python
import functools
import math

import jax
import jax.numpy as jnp
from jax import lax
from jax.experimental import pallas as pl
from jax.experimental.pallas import tpu as pltpu

D_MODEL = 1024
ATTN_HEADS = 4
HEAD_DIM = 64
HEAD_COLS = 2 * HEAD_DIM
ATTN_WIDTH = ATTN_HEADS * HEAD_COLS
CONV_WIDTH = D_MODEL // 2
CONV_K = 3
NORM_EPS = 1e-6
QKV_COLS = 3 * ATTN_WIDTH
CONV_COLS = 4 * CONV_WIDTH
GATE_COLS = 2 * D_MODEL
IN_COLS = QKV_COLS + ATTN_WIDTH + CONV_COLS + GATE_COLS
MASK_VALUE = -1e30
LOG2_E = math.log2(math.e)
Q_SCALE_LOG2 = HEAD_DIM ** -0.5 * LOG2_E
ALIBI_SKIP_BITS = 160.0
NORM_SLACK = 1.01

PROJ_ROWS = 1024
PROJ_SUB_ROWS = 512
ATTN_Q_ROWS = 1024
ATTN_K_ROWS = 1024
TK_SHIFT = ATTN_K_ROWS.bit_length() - 1
ATTN_SUB_ROWS = 256
ATTN_LOOKAHEAD = 3
assert ATTN_K_ROWS == 1 << TK_SHIFT and ATTN_K_ROWS % ATTN_Q_ROWS == 0
assert PROJ_ROWS == ATTN_K_ROWS
assert ATTN_Q_ROWS % PROJ_SUB_ROWS == 0 and PROJ_SUB_ROWS % ATTN_SUB_ROWS == 0
OUT_ROWS = 1024
OUT_SUB_ROWS = 512
BF16_SUBLANES = 16
HALO_ROWS = BF16_SUBLANES
assert HALO_ROWS >= CONV_K - 1
V7X_VMEM_BYTES = 64 * 1024 * 1024
VMEM_LIMIT_BYTES = V7X_VMEM_BYTES * 7 // 8

BF16 = jnp.bfloat16
F32 = jnp.float32


def _lambda_init(layer_idx):
    return 0.8 - 0.6 * math.exp(-0.3 * layer_idx)


def _rms_norm_bf16(x, gain):
    ms = jnp.mean(x * x, axis=-1, keepdims=True)
    return (x * lax.rsqrt(ms + NORM_EPS) * gain).astype(BF16)


def _max_map_norm(x, lane):
    sq = x * x
    n0 = jnp.sum(jnp.where(lane < HEAD_DIM, sq, 0.0), axis=-1, keepdims=True)
    n1 = jnp.sum(jnp.where(lane >= HEAD_DIM, sq, 0.0), axis=-1, keepdims=True)
    return jnp.sqrt(jnp.max(jnp.maximum(n0, n1), axis=0, keepdims=True))[0, 0] * NORM_SLACK


def _qkv_proj_kernel(slopes_ref, x_ref, g_ref, wqkv_ref, win_ref, wa_ref, wc_ref, wo_ref,
                     q_ref, ka_ref, kb_ref, v_ref, qnorm_ref, knorm_ref, win16_ref, wa16_ref,
                     wc16_ref, wo16_ref, h_ref, biasa_ref, biasb_ref, w_ref):
    for w32_ref, w16_ref in ((win_ref, win16_ref), (wa_ref, wa16_ref), (wc_ref, wc16_ref),
                             (wo_ref, wo16_ref)):
        w16_ref[...] = w32_ref[...].astype(BF16)

    @pl.when(pl.program_id(0) == 0)
    def _first_step():
        w_ref[...] = wqkv_ref[...].astype(BF16)
        lane = lax.broadcasted_iota(jnp.int32, (PROJ_ROWS, HEAD_COLS), 1)
        row = lax.broadcasted_iota(jnp.int32, (PROJ_ROWS, HEAD_COLS), 0).astype(F32)
        for hd in range(ATTN_HEADS):
            b = row * (slopes_ref[hd] * LOG2_E)
            b1 = b.astype(BF16).astype(F32)
            b2 = (b - b1).astype(BF16).astype(F32)
            b3 = (b - b1 - b2).astype(BF16).astype(F32)
            for bias_ref, first in ((biasa_ref, HEAD_DIM), (biasb_ref, 0)):
                bias_ref[hd] = jnp.where(lane == first, b1, jnp.where(
                    lane == first + 1, b2, jnp.where(lane == first + 2, b3, 0.0)))

    lane = lax.broadcasted_iota(jnp.int32, (PROJ_SUB_ROWS, HEAD_COLS), 1)
    for r in range(0, PROJ_ROWS, PROJ_SUB_ROWS):
        rows = slice(r, r + PROJ_SUB_ROWS)
        norm_row = pl.program_id(0) * (PROJ_ROWS // PROJ_SUB_ROWS) + r // PROJ_SUB_ROWS
        h_ref[rows, :] = _rms_norm_bf16(x_ref[rows, :], g_ref[...])

        def proj(col, rows=rows):
            return jnp.dot(h_ref[rows, :], w_ref[:, col:col + ATTN_WIDTH],
                           preferred_element_type=F32)

        q = proj(0) * Q_SCALE_LOG2
        q_ref[rows, :] = q.astype(BF16)
        k = proj(ATTN_WIDTH)
        for hd in range(ATTN_HEADS):
            cols = slice(hd * HEAD_COLS, (hd + 1) * HEAD_COLS)
            kh = k[:, cols]
            ka_ref[rows, cols] = jnp.where(lane < HEAD_DIM, kh, biasa_ref[hd, rows, :]).astype(BF16)
            kb_ref[rows, cols] = jnp.where(lane >= HEAD_DIM, kh, biasb_ref[hd, rows, :]).astype(BF16)
            qnorm_ref[norm_row, hd] = _max_map_norm(q[:, cols], lane)
            knorm_ref[norm_row, hd] = _max_map_norm(kh, lane)
        v_ref[rows, :] = proj(2 * ATTN_WIDTH).astype(BF16)


def _qkv_proj(slopes, x2d, g_pre, w_in, w_attn_o, w_conv_o, w_out):
    n = x2d.shape[0]
    steps = n // PROJ_ROWS
    row_block = lambda cols: pl.BlockSpec((PROJ_ROWS, cols), lambda i: (i, 0))
    act = jax.ShapeDtypeStruct((n, ATTN_WIDTH), BF16)
    norms = jax.ShapeDtypeStruct((n // PROJ_SUB_ROWS, ATTN_HEADS), F32)
    weights = (w_in, w_attn_o, w_conv_o, w_out)
    assert all(w.shape[0] % (steps * BF16_SUBLANES) == 0 for w in weights)
    slab = lambda w: pl.BlockSpec((w.shape[0] // steps, w.shape[1]), lambda i: (i, 0))
    return pl.pallas_call(
        _qkv_proj_kernel,
        grid=(steps,),
        in_specs=[
            pl.BlockSpec(memory_space=pltpu.SMEM),
            row_block(D_MODEL),
            pl.BlockSpec((1, D_MODEL), lambda i: (0, 0)),
            pl.BlockSpec((D_MODEL, QKV_COLS), lambda i: (0, 0), pipeline_mode=pl.Buffered(1)),
        ] + [slab(w) for w in weights],
        out_specs=[row_block(ATTN_WIDTH), row_block(ATTN_WIDTH), row_block(ATTN_WIDTH),
                   row_block(ATTN_WIDTH), pl.BlockSpec(memory_space=pltpu.SMEM),
                   pl.BlockSpec(memory_space=pltpu.SMEM)]
        + [slab(w) for w in weights],
        out_shape=[act, act, act, act, norms, norms]
        + [jax.ShapeDtypeStruct(w.shape, BF16) for w in weights],
        scratch_shapes=[pltpu.VMEM((PROJ_ROWS, D_MODEL), BF16),
                        pltpu.VMEM((ATTN_HEADS, PROJ_ROWS, HEAD_COLS), F32),
                        pltpu.VMEM((ATTN_HEADS, PROJ_ROWS, HEAD_COLS), F32),
                        pltpu.VMEM((D_MODEL, QKV_COLS), BF16)],
        compiler_params=pltpu.CompilerParams(
            dimension_semantics=("arbitrary",), vmem_limit_bytes=VMEM_LIMIT_BYTES),
        name="qkv_proj",
    )(slopes, x2d, g_pre, w_in, w_in, w_attn_o, w_conv_o, w_out)


def _diff_attn_kernel(slopes_ref, qnorm_ref, knorm_ref, q_ref, ka_ref, kb_ref, v_ref,
                      o0_ref, o1_ref, m_ref, acc_ref):
    tq, tk, ts = ATTN_Q_ROWS, ATTN_K_ROWS, ATTN_SUB_ROWS
    th = PROJ_SUB_ROWS
    batch = pl.program_id(0)
    head = pl.program_id(1)
    qi = pl.program_id(2)
    slope = slopes_ref[head] * LOG2_E
    seq_norm_row = batch * (ka_ref.shape[0] // th)

    q = q_ref[...].astype(F32)
    qlane = lax.broadcasted_iota(jnp.int32, q.shape, 1)
    q_maps = (
        jnp.where(qlane < HEAD_DIM, q, jnp.where(qlane < HEAD_DIM + 3, 1.0, 0.0)).astype(BF16),
        jnp.where(qlane >= HEAD_DIM, q, jnp.where(qlane < 3, 1.0, 0.0)).astype(BF16),
    )
    key_refs = (ka_ref, kb_ref)

    tri_keep = (lax.broadcasted_iota(jnp.int32, (ts, ts), 1)
                <= lax.broadcasted_iota(jnp.int32, (ts, ts), 0))

    def step(key_start, diag_cols, row_stop=tq, key_lo=0):
        span_bias = (key_start - qi * tq).astype(F32) * slope
        first = diag_cols is not None
        streams = [(r, mp)
                   for r in (range(tq - ts, -ts, -ts) if first else range(0, row_stop, ts))
                   for mp in range(2)]

        def width(r):
            return tk - key_lo if diag_cols is None else diag_cols + r + ts

        def scores(r, mp):
            keys = key_refs[mp][pl.ds(key_start + key_lo, width(r)), :]
            s = lax.dot_general(q_maps[mp][r:r + ts], keys, (((1,), (1,)), ((), ())),
                                preferred_element_type=F32)
            if diag_cols is not None:
                below = width(r) - ts
                blocks = [s[:, :below]] if below else []
                s = jnp.concatenate(blocks + [jnp.where(tri_keep, s[:, below:], MASK_VALUE)],
                                    axis=1)
            return s

        probs = {}

        def update(r, mp, s):
            m_cur = jnp.max(s, axis=-1, keepdims=True) + span_bias
            if first:
                m_new = jnp.broadcast_to(m_cur, (ts, HEAD_COLS))
                alpha = None
            else:
                m_prev = m_ref[mp, r:r + ts]
                m_new = jnp.maximum(m_prev, m_cur)
                alpha = jnp.exp2(m_prev - m_new)
            shift = m_new - span_bias
            p = jnp.exp2(s - jnp.concatenate([shift] * (width(r) // HEAD_COLS), axis=1))
            m_ref[mp, r:r + ts] = m_new
            probs[mp] = (p.astype(BF16), alpha)
            if mp == 0:
                return
            v_ones = jnp.concatenate([v_ref[pl.ds(key_start + key_lo, width(r)), :],
                                      jnp.ones((width(r), HEAD_COLS), BF16)], axis=1)
            pv_pair = jnp.dot(jnp.concatenate([probs[0][0], probs[1][0]], axis=0), v_ones,
                              preferred_element_type=F32)
            for mq in range(2):
                pv = pv_pair[mq * ts:(mq + 1) * ts]
                alpha = probs[mq][1]
                if alpha is not None:
                    pv = jnp.concatenate([alpha, alpha], axis=1) * acc_ref[mq, r:r + ts] + pv
                acc_ref[mq, r:r + ts] = pv

        pending = [scores(*st) for st in streams[:ATTN_LOOKAHEAD]]
        for i, st in enumerate(streams):
            if i + ATTN_LOOKAHEAD < len(streams):
                pending.append(scores(*streams[i + ATTN_LOOKAHEAD]))
            update(*st, pending.pop(0))

    n_spans = lax.shift_right_logical(qi * tq, TK_SHIFT)
    tail_start = pl.multiple_of(n_spans * tk, tk)
    for extra in range(0, tk, tq):
        @pl.when(qi * tq - tail_start == extra)
        def _diag_step(extra=extra):
            step(tail_start, extra)

    def row_block_bound(hf):
        m_low = jnp.minimum(m_ref[0, hf * th:(hf + 1) * th], m_ref[1, hf * th:(hf + 1) * th])
        m_low = jnp.min(jnp.min(m_low, axis=0, keepdims=True), axis=1, keepdims=True)[0, 0]
        q_norm = qnorm_ref[seq_norm_row + qi * (tq // th) + hf, head]
        return q_norm, m_low - ALIBI_SKIP_BITS

    row_blocks = [row_block_bound(hf) for hf in range(tq // th)]

    def full_span(kj, carry):
        key_start = pl.multiple_of(kj * tk, tk)
        rows_needed = jnp.int32(0)
        keys_from = jnp.int32(tk)
        for kh in reversed(range(tk // th)):
            k_norm = knorm_ref[seq_norm_row + kj * (tk // th) + kh, head]
            alibi_max = (key_start + ((kh + 1) * th - 1) - qi * tq).astype(F32) * slope
            for hf, (q_norm, floor) in enumerate(row_blocks):
                live = q_norm * k_norm + alibi_max > floor
                rows_needed = jnp.where(live, jnp.maximum(rows_needed, (hf + 1) * th),
                                        rows_needed)
                keys_from = jnp.where(live, kh * th, keys_from)
        for hf in range(tq // th):
            for kh in range(tk // th):
                @pl.when(jnp.logical_and(rows_needed == (hf + 1) * th, keys_from == kh * th))
                def _fold_span(hf=hf, kh=kh):
                    step(key_start, None, row_stop=(hf + 1) * th, key_lo=kh * th)

        return carry

    lax.fori_loop(0, n_spans, full_span, 0)

    for mp, o_ref in enumerate((o0_ref, o1_ref)):
        o_ref[...] = (acc_ref[mp, :, 0:HEAD_COLS]
                      / acc_ref[mp, :, HEAD_COLS:2 * HEAD_COLS]).astype(BF16)


def _diff_attn(q, ka, kb, v, slopes, qnorm, knorm, *, batch, seq):
    nq = seq // ATTN_Q_ROWS
    seq_block = pl.BlockSpec((seq, HEAD_COLS), lambda b, h, i: (b, h))
    q_block = pl.BlockSpec((ATTN_Q_ROWS, HEAD_COLS), lambda b, h, i: (b * nq + i, h))
    return pl.pallas_call(
        _diff_attn_kernel,
        grid=(batch, ATTN_HEADS, nq),
        in_specs=[
            pl.BlockSpec(memory_space=pltpu.SMEM), pl.BlockSpec(memory_space=pltpu.SMEM),
            pl.BlockSpec(memory_space=pltpu.SMEM),
            q_block, seq_block, seq_block, seq_block,
        ],
        out_specs=[q_block, q_block],
        out_shape=[jax.ShapeDtypeStruct((batch * seq, ATTN_WIDTH), BF16)] * 2,
        scratch_shapes=[pltpu.VMEM((2, ATTN_Q_ROWS, HEAD_COLS), F32),
                        pltpu.VMEM((2, ATTN_Q_ROWS, 2 * HEAD_COLS), F32)],
        compiler_params=pltpu.CompilerParams(
            dimension_semantics=("arbitrary", "arbitrary", "arbitrary"),
            vmem_limit_bytes=VMEM_LIMIT_BYTES),
        name="diff_attn",
    )(slopes, qnorm, knorm, q, ka, kb, v)


def _sigmoid(z):
    return 0.5 * jnp.tanh(0.5 * z) + 0.5


def _out_proj_kernel(x_ref, o0_ref, o1_ref, lq1_ref, lk1_ref, lq2_ref, lk2_ref, subln_ref,
                     gpre_ref, wmid_ref, whi_ref, cw_ref, wa_ref, wc_ref, bm_ref, wo_ref, gp_ref,
                     out_ref, h_ref, u_ref, *, tiles_per_seq, lam_init):
    tm, ts, cwid = OUT_ROWS, OUT_SUB_ROWS, CONV_WIDTH

    @pl.when(pl.program_id(0) % tiles_per_seq == 0)
    def _sequence_start():
        u_ref[0:HALO_ROWS, :] = jnp.zeros((HALO_ROWS, cwid), F32)

    blocks = range(0, tm, ts)

    def proj(r, w_ref, col, cols):
        return jnp.dot(h_ref[r:r + ts, :], w_ref[:, col:col + cols], preferred_element_type=F32)

    for r in blocks:
        h_ref[r:r + ts, :] = _rms_norm_bf16(x_ref[r:r + ts, :], gpre_ref[...])

    for r in blocks:
        u = proj(r, wmid_ref, ATTN_WIDTH + cwid, cwid) * proj(r, whi_ref, 0, cwid)
        u_ref[HALO_ROWS + r:HALO_ROWS + r + ts, :] = u
    y_conv = {}
    for r in blocks:
        conv = cw_ref[CONV_K - 1:CONV_K, :] * u_ref[HALO_ROWS + r:HALO_ROWS + r + ts, :]
        for k in range(CONV_K - 1):
            first_row = HALO_ROWS + r - (CONV_K - 1 - k)
            conv = conv + cw_ref[k:k + 1, :] * u_ref[first_row:first_row + ts, :]
        z_c = proj(r, whi_ref, cwid, cwid)
        yc_in = (proj(r, wmid_ref, ATTN_WIDTH, cwid) * conv) * (z_c * _sigmoid(z_c))
        y_conv[r] = jnp.dot(yc_in.astype(BF16), wc_ref[...], preferred_element_type=F32)
    u_ref[0:HALO_ROWS, :] = u_ref[tm:tm + HALO_ROWS, :]

    g_a = {}
    for r in blocks:
        g_a[r] = _sigmoid(proj(r, whi_ref, 2 * cwid, D_MODEL) + bm_ref[:, 0:D_MODEL])

    lam = (jnp.exp(jnp.sum(lq1_ref[...] * lk1_ref[...], axis=-1, keepdims=True))
           - jnp.exp(jnp.sum(lq2_ref[...] * lk2_ref[...], axis=-1, keepdims=True))
           + lam_init)
    subln_gain = subln_ref[...] * (1.0 - lam_init)
    y_merged = {}
    for r in blocks:
        o = o0_ref[r:r + ts, :].astype(F32) - lam * o1_ref[r:r + ts, :].astype(F32)
        heads = []
        for hd in range(ATTN_HEADS):
            oh = o[:, hd * HEAD_COLS:(hd + 1) * HEAD_COLS]
            ms = jnp.mean(oh * oh, axis=-1, keepdims=True)
            heads.append(oh * lax.rsqrt(ms + NORM_EPS) * subln_gain)
        z_a = proj(r, wmid_ref, 0, ATTN_WIDTH)
        ya_in = jnp.concatenate(heads, axis=1) * (z_a * _sigmoid(z_a))
        y_attn = jnp.dot(ya_in.astype(BF16), wa_ref[...], preferred_element_type=F32)
        y_merged[r] = g_a[r] * y_attn

    for r in blocks:
        g_c = _sigmoid(proj(r, whi_ref, 2 * cwid + D_MODEL, D_MODEL)
                       + bm_ref[:, D_MODEL:2 * D_MODEL])
        y_merged[r] = (y_merged[r] + g_c * y_conv[r]).astype(BF16)
    for r in blocks:
        out = jnp.dot(y_merged[r], wo_ref[...], preferred_element_type=F32)
        ms = jnp.mean(out * out, axis=-1, keepdims=True)
        out_ref[r:r + ts, :] = x_ref[r:r + ts, :] + out * lax.rsqrt(ms + NORM_EPS) * gp_ref[...]


def _out_proj(x2d, o0, o1, lq1, lk1, lq2, lk2, subln_gain, g_pre, w_in_bf16, conv_w, wa, wc,
              b_merge, wo, g_post, *, seq, lam_init):
    n = x2d.shape[0]
    mid_cols = ATTN_WIDTH + 2 * CONV_WIDTH
    hi_cols = 2 * CONV_WIDTH + GATE_COLS
    assert QKV_COLS == mid_cols and QKV_COLS + mid_cols == hi_cols == IN_COLS - hi_cols
    row_block = lambda cols: pl.BlockSpec((OUT_ROWS, cols), lambda i: (i, 0))
    const = lambda shape, col_block=0: pl.BlockSpec(shape, lambda i: (0, col_block),
                                                   pipeline_mode=pl.Buffered(1))
    return pl.pallas_call(
        functools.partial(_out_proj_kernel, tiles_per_seq=seq // OUT_ROWS, lam_init=lam_init),
        grid=(n // OUT_ROWS,),
        in_specs=[
            row_block(D_MODEL), row_block(ATTN_WIDTH), row_block(ATTN_WIDTH),
            const((1, HEAD_DIM)), const((1, HEAD_DIM)), const((1, HEAD_DIM)), const((1, HEAD_DIM)),
            const((1, HEAD_COLS)), const((1, D_MODEL)),
            const((D_MODEL, mid_cols), 1), const((D_MODEL, hi_cols), 1),
            const((CONV_K, CONV_WIDTH)), const((ATTN_WIDTH, D_MODEL)),
            const((CONV_WIDTH, D_MODEL)), const((1, 2 * D_MODEL)), const((D_MODEL, D_MODEL)),
            const((1, D_MODEL)),
        ],
        out_specs=row_block(D_MODEL),
        out_shape=jax.ShapeDtypeStruct((n, D_MODEL), F32),
        scratch_shapes=[pltpu.VMEM((OUT_ROWS, D_MODEL), BF16),
                        pltpu.VMEM((HALO_ROWS + OUT_ROWS, CONV_WIDTH), F32)],
        compiler_params=pltpu.CompilerParams(
            dimension_semantics=("arbitrary",), vmem_limit_bytes=VMEM_LIMIT_BYTES),
        name="out_proj",
    )(x2d, o0, o1, lq1, lk1, lq2, lk2, subln_gain, g_pre, w_in_bf16, w_in_bf16, conv_w, wa, wc,
      b_merge, wo, g_post)


def _layer(x, layer_idx, w_in, lq1, lk1, lq2, lk2, subln_gain, conv_w, w_attn_o, w_conv_o,
           b_merge, w_out, g_pre, g_post):
    batch, seq, d = x.shape
    x2d = x.reshape(batch * seq, d)
    row = lambda a: a.reshape(1, -1).astype(F32)
    slopes = jnp.asarray([2.0 ** (-8.0 * (i + 1) / ATTN_HEADS) for i in range(ATTN_HEADS)], F32)

    q, ka, kb, v, qnorm, knorm, w_in_bf16, wa_bf16, wc_bf16, wo_bf16 = _qkv_proj(
        slopes, x2d, row(g_pre), w_in, w_attn_o, w_conv_o, w_out)
    o0, o1 = _diff_attn(q, ka, kb, v, slopes, qnorm, knorm, batch=batch, seq=seq)
    out = _out_proj(x2d, o0, o1, row(lq1), row(lk1), row(lq2), row(lk2), row(subln_gain),
                    row(g_pre), w_in_bf16, conv_w.astype(F32), wa_bf16, wc_bf16, row(b_merge),
                    wo_bf16, row(g_post), seq=seq, lam_init=_lambda_init(layer_idx))
    return out.reshape(batch, seq, d)


def kernel(x, w_in, lambda_q1, lambda_k1, lambda_q2, lambda_k2, subln_gain, conv_w, w_attn_o,
           w_conv_o, b_merge, w_out, g_pre, g_post):
    for l in range(w_in.shape[0]):
        x = _layer(x, l, w_in[l], lambda_q1[l], lambda_k1[l], lambda_q2[l], lambda_k2[l],
                   subln_gain[l], conv_w[l], w_attn_o[l], w_conv_o[l], b_merge[l], w_out[l],
                   g_pre[l], g_post[l])
    return x
```

```python
import functools
import math

import jax
import jax.numpy as jnp
from jax import lax
from jax.experimental import pallas as pl
from jax.experimental.pallas import tpu as pltpu

D_MODEL = 1024
ATTN_HEADS = 4
HEAD_DIM = 64
HEAD_COLS = 2 * HEAD_DIM
ATTN_WIDTH = ATTN_HEADS * HEAD_COLS
CONV_WIDTH = D_MODEL // 2
CONV_K = 3
NORM_EPS = 1e-6
QKV_COLS = 3 * ATTN_WIDTH
CONV_COLS = 4 * CONV_WIDTH
GATE_COLS = 2 * D_MODEL
IN_COLS = QKV_COLS + ATTN_WIDTH + CONV_COLS + GATE_COLS
MASK_VALUE = -1e30
LOG2_E = math.log2(math.e)
Q_SCALE_LOG2 = HEAD_DIM ** -0.5 * LOG2_E
ALIBI_SKIP_BITS = 160.0
NORM_SLACK = 1.01

PROJ_ROWS = 1024
PROJ_SUB_ROWS = 512
ATTN_Q_ROWS = 1024
ATTN_K_ROWS = 1024
TK_SHIFT = ATTN_K_ROWS.bit_length() - 1
ATTN_SUB_ROWS = 256
ATTN_LOOKAHEAD = 3
assert ATTN_K_ROWS == 1 << TK_SHIFT and ATTN_K_ROWS % ATTN_Q_ROWS == 0
assert PROJ_ROWS == ATTN_K_ROWS
assert ATTN_Q_ROWS % PROJ_SUB_ROWS == 0 and PROJ_SUB_ROWS % ATTN_SUB_ROWS == 0
OUT_ROWS = 1024
OUT_SUB_ROWS = 512
BF16_SUBLANES = 16
HALO_ROWS = BF16_SUBLANES
assert HALO_ROWS >= CONV_K - 1
V7X_VMEM_BYTES = 64 * 1024 * 1024
VMEM_LIMIT_BYTES = V7X_VMEM_BYTES * 7 // 8

BF16 = jnp.bfloat16
F32 = jnp.float32


def _lambda_init(layer_idx):
    return 0.8 - 0.6 * math.exp(-0.3 * layer_idx)


def _rms_norm_bf16(x, gain):
    ms = jnp.mean(x * x, axis=-1, keepdims=True)
    return (x * lax.rsqrt(ms + NORM_EPS) * gain).astype(BF16)


def _max_map_norm(x, lane):
    sq = x * x
    n0 = jnp.sum(jnp.where(lane < HEAD_DIM, sq, 0.0), axis=-1, keepdims=True)
    n1 = jnp.sum(jnp.where(lane >= HEAD_DIM, sq, 0.0), axis=-1, keepdims=True)
    return jnp.sqrt(jnp.max(jnp.maximum(n0, n1), axis=0, keepdims=True))[0, 0] * NORM_SLACK


def _qkv_proj_kernel(slopes_ref, x_ref, g_ref, wqkv_ref, q_ref, ka_ref, kb_ref, v_ref,
                     qnorm_ref, knorm_ref, h_ref, biasa_ref, biasb_ref, w_ref):
    @pl.when(pl.program_id(0) == 0)
    def _first_step():
        w_ref[...] = wqkv_ref[...].astype(BF16)
        lane = lax.broadcasted_iota(jnp.int32, (PROJ_ROWS, HEAD_COLS), 1)
        row = lax.broadcasted_iota(jnp.int32, (PROJ_ROWS, HEAD_COLS), 0).astype(F32)
        for hd in range(ATTN_HEADS):
            b = row * (slopes_ref[hd] * LOG2_E)
            b1 = b.astype(BF16).astype(F32)
            b2 = (b - b1).astype(BF16).astype(F32)
            b3 = (b - b1 - b2).astype(BF16).astype(F32)
            for bias_ref, first in ((biasa_ref, HEAD_DIM), (biasb_ref, 0)):
                bias_ref[hd] = jnp.where(lane == first, b1, jnp.where(
                    lane == first + 1, b2, jnp.where(lane == first + 2, b3, 0.0)))

    lane = lax.broadcasted_iota(jnp.int32, (PROJ_SUB_ROWS, HEAD_COLS), 1)
    for r in range(0, PROJ_ROWS, PROJ_SUB_ROWS):
        rows = slice(r, r + PROJ_SUB_ROWS)
        norm_row = pl.program_id(0) * (PROJ_ROWS // PROJ_SUB_ROWS) + r // PROJ_SUB_ROWS
        h_ref[rows, :] = _rms_norm_bf16(x_ref[rows, :], g_ref[...])

        def proj(col, rows=rows):
            return jnp.dot(h_ref[rows, :], w_ref[:, col:col + ATTN_WIDTH],
                           preferred_element_type=F32)

        q = proj(0) * Q_SCALE_LOG2
        q_ref[rows, :] = q.astype(BF16)
        k = proj(ATTN_WIDTH)
        for hd in range(ATTN_HEADS):
            cols = slice(hd * HEAD_COLS, (hd + 1) * HEAD_COLS)
            kh = k[:, cols]
            ka_ref[rows, cols] = jnp.where(lane < HEAD_DIM, kh, biasa_ref[hd, rows, :]).astype(BF16)
            kb_ref[rows, cols] = jnp.where(lane >= HEAD_DIM, kh, biasb_ref[hd, rows, :]).astype(BF16)
            qnorm_ref[norm_row, hd] = _max_map_norm(q[:, cols], lane)
            knorm_ref[norm_row, hd] = _max_map_norm(kh, lane)
        v_ref[rows, :] = proj(2 * ATTN_WIDTH).astype(BF16)


def _qkv_proj(slopes, x2d, g_pre, w_in):
    n = x2d.shape[0]
    row_block = lambda cols: pl.BlockSpec((PROJ_ROWS, cols), lambda i: (i, 0))
    act = jax.ShapeDtypeStruct((n, ATTN_WIDTH), BF16)
    norms = jax.ShapeDtypeStruct((n // PROJ_SUB_ROWS, ATTN_HEADS), F32)
    return pl.pallas_call(
        _qkv_proj_kernel,
        grid=(n // PROJ_ROWS,),
        in_specs=[
            pl.BlockSpec(memory_space=pltpu.SMEM),
            row_block(D_MODEL),
            pl.BlockSpec((1, D_MODEL), lambda i: (0, 0)),
            pl.BlockSpec((D_MODEL, QKV_COLS), lambda i: (0, 0), pipeline_mode=pl.Buffered(1)),
        ],
        out_specs=[row_block(ATTN_WIDTH), row_block(ATTN_WIDTH), row_block(ATTN_WIDTH),
                   row_block(ATTN_WIDTH), pl.BlockSpec(memory_space=pltpu.SMEM),
                   pl.BlockSpec(memory_space=pltpu.SMEM)],
        out_shape=[act, act, act, act, norms, norms],
        scratch_shapes=[pltpu.VMEM((PROJ_ROWS, D_MODEL), BF16),
                        pltpu.VMEM((ATTN_HEADS, PROJ_ROWS, HEAD_COLS), F32),
                        pltpu.VMEM((ATTN_HEADS, PROJ_ROWS, HEAD_COLS), F32),
                        pltpu.VMEM((D_MODEL, QKV_COLS), BF16)],
        compiler_params=pltpu.CompilerParams(
            dimension_semantics=("arbitrary",), vmem_limit_bytes=VMEM_LIMIT_BYTES),
        name="qkv_proj",
    )(slopes, x2d, g_pre, w_in)


def _diff_attn_kernel(slopes_ref, qnorm_ref, knorm_ref, q_ref, ka_ref, kb_ref, v_ref,
                      win_ref, wa_ref, wc_ref, wo_ref, o0_ref, o1_ref, win16_ref, wa16_ref,
                      wc16_ref, wo16_ref, m_ref, acc_ref):
    tq, tk, ts = ATTN_Q_ROWS, ATTN_K_ROWS, ATTN_SUB_ROWS
    th = PROJ_SUB_ROWS
    batch = pl.program_id(0)
    head = pl.program_id(1)
    qi = pl.program_id(2)

    @pl.when(qi == 0)
    def _round_weight_slabs():
        for w32_ref, w16_ref in ((win_ref, win16_ref), (wa_ref, wa16_ref), (wc_ref, wc16_ref),
                                 (wo_ref, wo16_ref)):
            w16_ref[...] = w32_ref[...].astype(BF16)

    slope = slopes_ref[head] * LOG2_E
    seq_norm_row = batch * (ka_ref.shape[0] // th)

    q = q_ref[...].astype(F32)
    qlane = lax.broadcasted_iota(jnp.int32, q.shape, 1)
    q_maps = (
        jnp.where(qlane < HEAD_DIM, q, jnp.where(qlane < HEAD_DIM + 3, 1.0, 0.0)).astype(BF16),
        jnp.where(qlane >= HEAD_DIM, q, jnp.where(qlane < 3, 1.0, 0.0)).astype(BF16),
    )
    key_refs = (ka_ref, kb_ref)

    tri_keep = (lax.broadcasted_iota(jnp.int32, (ts, ts), 1)
                <= lax.broadcasted_iota(jnp.int32, (ts, ts), 0))

    def step(key_start, diag_cols, row_stop=tq, key_lo=0):
        span_bias = (key_start - qi * tq).astype(F32) * slope
        first = diag_cols is not None
        streams = [(r, mp)
                   for r in (range(tq - ts, -ts, -ts) if first else range(0, row_stop, ts))
                   for mp in range(2)]

        def width(r):
            return tk - key_lo if diag_cols is None else diag_cols + r + ts

        def scores(r, mp):
            keys = key_refs[mp][pl.ds(key_start + key_lo, width(r)), :]
            s = lax.dot_general(q_maps[mp][r:r + ts], keys, (((1,), (1,)), ((), ())),
                                preferred_element_type=F32)
            if diag_cols is not None:
                below = width(r) - ts
                blocks = [s[:, :below]] if below else []
                s = jnp.concatenate(blocks + [jnp.where(tri_keep, s[:, below:], MASK_VALUE)],
                                    axis=1)
            return s

        probs = {}

        def update(r, mp, s):
            m_cur = jnp.max(s, axis=-1, keepdims=True) + span_bias
            if first:
                m_new = jnp.broadcast_to(m_cur, (ts, HEAD_COLS))
                alpha = None
            else:
                m_prev = m_ref[mp, r:r + ts]
                m_new = jnp.maximum(m_prev, m_cur)
                alpha = jnp.exp2(m_prev - m_new)
            shift = m_new - span_bias
            p = jnp.exp2(s - jnp.concatenate([shift] * (width(r) // HEAD_COLS), axis=1))
            m_ref[mp, r:r + ts] = m_new
            probs[mp] = (p.astype(BF16), alpha)
            if mp == 0:
                return
            v_ones = jnp.concatenate([v_ref[pl.ds(key_start + key_lo, width(r)), :],
                                      jnp.ones((width(r), HEAD_COLS), BF16)], axis=1)
            pv_pair = jnp.dot(jnp.concatenate([probs[0][0], probs[1][0]], axis=0), v_ones,
                              preferred_element_type=F32)
            for mq in range(2):
                pv = pv_pair[mq * ts:(mq + 1) * ts]
                alpha = probs[mq][1]
                if alpha is not None:
                    pv = jnp.concatenate([alpha, alpha], axis=1) * acc_ref[mq, r:r + ts] + pv
                acc_ref[mq, r:r + ts] = pv

        pending = [scores(*st) for st in streams[:ATTN_LOOKAHEAD]]
        for i, st in enumerate(streams):
            if i + ATTN_LOOKAHEAD < len(streams):
                pending.append(scores(*streams[i + ATTN_LOOKAHEAD]))
            update(*st, pending.pop(0))

    n_spans = lax.shift_right_logical(qi * tq, TK_SHIFT)
    tail_start = pl.multiple_of(n_spans * tk, tk)
    for extra in range(0, tk, tq):
        @pl.when(qi * tq - tail_start == extra)
        def _diag_step(extra=extra):
            step(tail_start, extra)

    def row_block_bound(hf):
        m_low = jnp.minimum(m_ref[0, hf * th:(hf + 1) * th], m_ref[1, hf * th:(hf + 1) * th])
        m_low = jnp.min(jnp.min(m_low, axis=0, keepdims=True), axis=1, keepdims=True)[0, 0]
        q_norm = qnorm_ref[seq_norm_row + qi * (tq // th) + hf, head]
        return q_norm, m_low - ALIBI_SKIP_BITS

    row_blocks = [row_block_bound(hf) for hf in range(tq // th)]

    def full_span(kj, carry):
        key_start = pl.multiple_of(kj * tk, tk)
        rows_needed = jnp.int32(0)
        keys_from = jnp.int32(tk)
        for kh in reversed(range(tk // th)):
            k_norm = knorm_ref[seq_norm_row + kj * (tk // th) + kh, head]
            alibi_max = (key_start + ((kh + 1) * th - 1) - qi * tq).astype(F32) * slope
            for hf, (q_norm, floor) in enumerate(row_blocks):
                live = q_norm * k_norm + alibi_max > floor
                rows_needed = jnp.where(live, jnp.maximum(rows_needed, (hf + 1) * th),
                                        rows_needed)
                keys_from = jnp.where(live, kh * th, keys_from)
        for hf in range(tq // th):
            for kh in range(tk // th):
                @pl.when(jnp.logical_and(rows_needed == (hf + 1) * th, keys_from == kh * th))
                def _fold_span(hf=hf, kh=kh):
                    step(key_start, None, row_stop=(hf + 1) * th, key_lo=kh * th)

        return carry

    lax.fori_loop(0, n_spans, full_span, 0)

    for mp, o_ref in enumerate((o0_ref, o1_ref)):
        o_ref[...] = (acc_ref[mp, :, 0:HEAD_COLS]
                      / acc_ref[mp, :, HEAD_COLS:2 * HEAD_COLS]).astype(BF16)


def _diff_attn(q, ka, kb, v, slopes, qnorm, knorm, weights, *, batch, seq):
    nq = seq // ATTN_Q_ROWS
    slabs = batch * ATTN_HEADS
    assert all(w.shape[0] % (slabs * BF16_SUBLANES) == 0 for w in weights)
    slab = lambda w: pl.BlockSpec((w.shape[0] // slabs, w.shape[1]),
                                  lambda b, h, i: (b * ATTN_HEADS + h, 0))
    seq_block = pl.BlockSpec((seq, HEAD_COLS), lambda b, h, i: (b, h))
    q_block = pl.BlockSpec((ATTN_Q_ROWS, HEAD_COLS), lambda b, h, i: (b * nq + i, h))
    return pl.pallas_call(
        _diff_attn_kernel,
        grid=(batch, ATTN_HEADS, nq),
        in_specs=[
            pl.BlockSpec(memory_space=pltpu.SMEM), pl.BlockSpec(memory_space=pltpu.SMEM),
            pl.BlockSpec(memory_space=pltpu.SMEM),
            q_block, seq_block, seq_block, seq_block,
        ] + [slab(w) for w in weights],
        out_specs=[q_block, q_block] + [slab(w) for w in weights],
        out_shape=[jax.ShapeDtypeStruct((batch * seq, ATTN_WIDTH), BF16)] * 2
        + [jax.ShapeDtypeStruct(w.shape, BF16) for w in weights],
        scratch_shapes=[pltpu.VMEM((2, ATTN_Q_ROWS, HEAD_COLS), F32),
                        pltpu.VMEM((2, ATTN_Q_ROWS, 2 * HEAD_COLS), F32)],
        compiler_params=pltpu.CompilerParams(
            dimension_semantics=("arbitrary", "arbitrary", "arbitrary"),
            vmem_limit_bytes=VMEM_LIMIT_BYTES),
        name="diff_attn",
    )(slopes, qnorm, knorm, q, ka, kb, v, *weights)


def _sigmoid(z):
    return 0.5 * jnp.tanh(0.5 * z) + 0.5


def _out_proj_kernel(x_ref, o0_ref, o1_ref, lq1_ref, lk1_ref, lq2_ref, lk2_ref, subln_ref,
                     gpre_ref, wmid_ref, whi_ref, cw_ref, wa_ref, wc_ref, bm_ref, wo_ref, gp_ref,
                     out_ref, h_ref, u_ref, *, tiles_per_seq, lam_init):
    tm, ts, cwid = OUT_ROWS, OUT_SUB_ROWS, CONV_WIDTH

    @pl.when(pl.program_id(0) % tiles_per_seq == 0)
    def _sequence_start():
        u_ref[0:HALO_ROWS, :] = jnp.zeros((HALO_ROWS, cwid), F32)

    blocks = range(0, tm, ts)

    def proj(r, w_ref, col, cols):
        return jnp.dot(h_ref[r:r + ts, :], w_ref[:, col:col + cols], preferred_element_type=F32)

    for r in blocks:
        h_ref[r:r + ts, :] = _rms_norm_bf16(x_ref[r:r + ts, :], gpre_ref[...])

    for r in blocks:
        u = proj(r, wmid_ref, ATTN_WIDTH + cwid, cwid) * proj(r, whi_ref, 0, cwid)
        u_ref[HALO_ROWS + r:HALO_ROWS + r + ts, :] = u
    y_conv = {}
    for r in blocks:
        conv = cw_ref[CONV_K - 1:CONV_K, :] * u_ref[HALO_ROWS + r:HALO_ROWS + r + ts, :]
        for k in range(CONV_K - 1):
            first_row = HALO_ROWS + r - (CONV_K - 1 - k)
            conv = conv + cw_ref[k:k + 1, :] * u_ref[first_row:first_row + ts, :]
        z_c = proj(r, whi_ref, cwid, cwid)
        yc_in = (proj(r, wmid_ref, ATTN_WIDTH, cwid) * conv) * (z_c * _sigmoid(z_c))
        y_conv[r] = jnp.dot(yc_in.astype(BF16), wc_ref[...], preferred_element_type=F32)
    u_ref[0:HALO_ROWS, :] = u_ref[tm:tm + HALO_ROWS, :]

    g_a = {}
    for r in blocks:
        g_a[r] = _sigmoid(proj(r, whi_ref, 2 * cwid, D_MODEL) + bm_ref[:, 0:D_MODEL])

    lam = (jnp.exp(jnp.sum(lq1_ref[...] * lk1_ref[...], axis=-1, keepdims=True))
           - jnp.exp(jnp.sum(lq2_ref[...] * lk2_ref[...], axis=-1, keepdims=True))
           + lam_init)
    subln_gain = subln_ref[...] * (1.0 - lam_init)
    y_merged = {}
    for r in blocks:
        o = o0_ref[r:r + ts, :].astype(F32) - lam * o1_ref[r:r + ts, :].astype(F32)
        heads = []
        for hd in range(ATTN_HEADS):
            oh = o[:, hd * HEAD_COLS:(hd + 1) * HEAD_COLS]
            ms = jnp.mean(oh * oh, axis=-1, keepdims=True)
            heads.append(oh * lax.rsqrt(ms + NORM_EPS) * subln_gain)
        z_a = proj(r, wmid_ref, 0, ATTN_WIDTH)
        ya_in = jnp.concatenate(heads, axis=1) * (z_a * _sigmoid(z_a))
        y_attn = jnp.dot(ya_in.astype(BF16), wa_ref[...], preferred_element_type=F32)
        y_merged[r] = g_a[r] * y_attn

    for r in blocks:
        g_c = _sigmoid(proj(r, whi_ref, 2 * cwid + D_MODEL, D_MODEL)
                       + bm_ref[:, D_MODEL:2 * D_MODEL])
        y_merged[r] = (y_merged[r] + g_c * y_conv[r]).astype(BF16)
    for r in blocks:
        out = jnp.dot(y_merged[r], wo_ref[...], preferred_element_type=F32)
        ms = jnp.mean(out * out, axis=-1, keepdims=True)
        out_ref[r:r + ts, :] = x_ref[r:r + ts, :] + out * lax.rsqrt(ms + NORM_EPS) * gp_ref[...]


def _out_proj(x2d, o0, o1, lq1, lk1, lq2, lk2, subln_gain, g_pre, w_in_bf16, conv_w, wa, wc,
              b_merge, wo, g_post, *, seq, lam_init):
    n = x2d.shape[0]
    mid_cols = ATTN_WIDTH + 2 * CONV_WIDTH
    hi_cols = 2 * CONV_WIDTH + GATE_COLS
    assert QKV_COLS == mid_cols and QKV_COLS + mid_cols == hi_cols == IN_COLS - hi_cols
    row_block = lambda cols: pl.BlockSpec((OUT_ROWS, cols), lambda i: (i, 0))
    const = lambda shape, col_block=0: pl.BlockSpec(shape, lambda i: (0, col_block),
                                                   pipeline_mode=pl.Buffered(1))
    return pl.pallas_call(
        functools.partial(_out_proj_kernel, tiles_per_seq=seq // OUT_ROWS, lam_init=lam_init),
        grid=(n // OUT_ROWS,),
        in_specs=[
            row_block(D_MODEL), row_block(ATTN_WIDTH), row_block(ATTN_WIDTH),
            const((1, HEAD_DIM)), const((1, HEAD_DIM)), const((1, HEAD_DIM)), const((1, HEAD_DIM)),
            const((1, HEAD_COLS)), const((1, D_MODEL)),
            const((D_MODEL, mid_cols), 1), const((D_MODEL, hi_cols), 1),
            const((CONV_K, CONV_WIDTH)), const((ATTN_WIDTH, D_MODEL)),
            const((CONV_WIDTH, D_MODEL)), const((1, 2 * D_MODEL)), const((D_MODEL, D_MODEL)),
            const((1, D_MODEL)),
        ],
        out_specs=row_block(D_MODEL),
        out_shape=jax.ShapeDtypeStruct((n, D_MODEL), F32),
        scratch_shapes=[pltpu.VMEM((OUT_ROWS, D_MODEL), BF16),
                        pltpu.VMEM((HALO_ROWS + OUT_ROWS, CONV_WIDTH), F32)],
        compiler_params=pltpu.CompilerParams(
            dimension_semantics=("arbitrary",), vmem_limit_bytes=VMEM_LIMIT_BYTES),
        name="out_proj",
    )(x2d, o0, o1, lq1, lk1, lq2, lk2, subln_gain, g_pre, w_in_bf16, w_in_bf16, conv_w, wa, wc,
      b_merge, wo, g_post)


def _layer(x, layer_idx, w_in, lq1, lk1, lq2, lk2, subln_gain, conv_w, w_attn_o, w_conv_o,
           b_merge, w_out, g_pre, g_post):
    batch, seq, d = x.shape
    x2d = x.reshape(batch * seq, d)
    row = lambda a: a.reshape(1, -1).astype(F32)
    slopes = jnp.asarray([2.0 ** (-8.0 * (i + 1) / ATTN_HEADS) for i in range(ATTN_HEADS)], F32)

    q, ka, kb, v, qnorm, knorm = _qkv_proj(slopes, x2d, row(g_pre), w_in)
    o0, o1, w_in_bf16, wa_bf16, wc_bf16, wo_bf16 = _diff_attn(
        q, ka, kb, v, slopes, qnorm, knorm, (w_in, w_attn_o, w_conv_o, w_out),
        batch=batch, seq=seq)
    out = _out_proj(x2d, o0, o1, row(lq1), row(lk1), row(lq2), row(lk2), row(subln_gain),
                    row(g_pre), w_in_bf16, conv_w.astype(F32), wa_bf16, wc_bf16, row(b_merge),
                    wo_bf16, row(g_post), seq=seq, lam_init=_lambda_init(layer_idx))
    return out.reshape(batch, seq, d)


def kernel(x, w_in, lambda_q1, lambda_k1, lambda_q2, lambda_k2, subln_gain, conv_w, w_attn_o,
           w_conv_o, b_merge, w_out, g_pre, g_post):
    for l in range(w_in.shape[0]):
        x = _layer(x, l, w_in[l], lambda_q1[l], lambda_k1[l], lambda_q2[l], lambda_k2[l],
                   subln_gain[l], conv_w[l], w_attn_o[l], w_conv_o[l], b_merge[l], w_out[l],
                   g_pre[l], g_post[l])
    return x
```

```python
import functools
import math

import jax
import jax.numpy as jnp
from jax import lax
from jax.experimental import pallas as pl
from jax.experimental.pallas import tpu as pltpu

D_MODEL = 1024
ATTN_HEADS = 4
HEAD_DIM = 64
HEAD_COLS = 2 * HEAD_DIM
ATTN_WIDTH = ATTN_HEADS * HEAD_COLS
CONV_WIDTH = D_MODEL // 2
CONV_K = 3
NORM_EPS = 1e-6
QKV_COLS = 3 * ATTN_WIDTH
CONV_COLS = 4 * CONV_WIDTH
GATE_COLS = 2 * D_MODEL
IN_COLS = QKV_COLS + ATTN_WIDTH + CONV_COLS + GATE_COLS
MASK_VALUE = -1e30
LOG2_E = math.log2(math.e)
Q_SCALE_LOG2 = HEAD_DIM ** -0.5 * LOG2_E
ALIBI_SKIP_BITS = 160.0
NORM_SLACK = 1.01

PROJ_ROWS = 1024
PROJ_SUB_ROWS = 512
ATTN_Q_ROWS = 1024
ATTN_K_ROWS = 1024
TK_SHIFT = ATTN_K_ROWS.bit_length() - 1
ATTN_SUB_ROWS = 256
ATTN_LOOKAHEAD = 3
assert ATTN_K_ROWS == 1 << TK_SHIFT and ATTN_K_ROWS % ATTN_Q_ROWS == 0
assert PROJ_ROWS == ATTN_K_ROWS
assert ATTN_Q_ROWS % PROJ_SUB_ROWS == 0 and PROJ_SUB_ROWS % ATTN_SUB_ROWS == 0
OUT_ROWS = 1024
OUT_SUB_ROWS = 512
BF16_SUBLANES = 16
HALO_ROWS = BF16_SUBLANES
assert HALO_ROWS >= CONV_K - 1
V7X_VMEM_BYTES = 64 * 1024 * 1024
VMEM_LIMIT_BYTES = V7X_VMEM_BYTES * 7 // 8

BF16 = jnp.bfloat16
F32 = jnp.float32


def _lambda_init(layer_idx):
    return 0.8 - 0.6 * math.exp(-0.3 * layer_idx)


def _rms_norm_bf16(x, gain):
    ms = jnp.mean(x * x, axis=-1, keepdims=True)
    return (x * lax.rsqrt(ms + NORM_EPS) * gain).astype(BF16)


def _max_map_norm(x, lane):
    sq = x * x
    n0 = jnp.sum(jnp.where(lane < HEAD_DIM, sq, 0.0), axis=-1, keepdims=True)
    n1 = jnp.sum(jnp.where(lane >= HEAD_DIM, sq, 0.0), axis=-1, keepdims=True)
    return jnp.sqrt(jnp.max(jnp.maximum(n0, n1), axis=0, keepdims=True))[0, 0] * NORM_SLACK


def _qkv_proj_kernel(slopes_ref, x_ref, g_ref, wqkv_ref, win_ref, wa_ref, wc_ref, wo_ref,
                     q_ref, ka_ref, kb_ref, v_ref, qnorm_ref, knorm_ref, win16_ref, wa16_ref,
                     wc16_ref, wo16_ref, h_ref, biasa_ref, biasb_ref, w_ref):
    for w32_ref, w16_ref in ((win_ref, win16_ref), (wa_ref, wa16_ref), (wc_ref, wc16_ref),
                             (wo_ref, wo16_ref)):
        w16_ref[...] = w32_ref[...].astype(BF16)

    @pl.when(pl.program_id(0) == 0)
    def _first_step():
        w_ref[...] = wqkv_ref[...].astype(BF16)
        lane = lax.broadcasted_iota(jnp.int32, (PROJ_ROWS, HEAD_COLS), 1)
        row = lax.broadcasted_iota(jnp.int32, (PROJ_ROWS, HEAD_COLS), 0).astype(F32)
        for hd in range(ATTN_HEADS):
            b = row * (slopes_ref[hd] * LOG2_E)
            b1 = b.astype(BF16).astype(F32)
            b2 = (b - b1).astype(BF16).astype(F32)
            b3 = (b - b1 - b2).astype(BF16).astype(F32)
            for bias_ref, first in ((biasa_ref, HEAD_DIM), (biasb_ref, 0)):
                bias_ref[hd] = jnp.where(lane == first, b1, jnp.where(
                    lane == first + 1, b2, jnp.where(lane == first + 2, b3, 0.0)))

    lane = lax.broadcasted_iota(jnp.int32, (PROJ_SUB_ROWS, HEAD_COLS), 1)
    for r in range(0, PROJ_ROWS, PROJ_SUB_ROWS):
        rows = slice(r, r + PROJ_SUB_ROWS)
        norm_row = pl.program_id(0) * (PROJ_ROWS // PROJ_SUB_ROWS) + r // PROJ_SUB_ROWS
        h_ref[rows, :] = _rms_norm_bf16(x_ref[rows, :], g_ref[...])

        def proj(col, rows=rows):
            return jnp.dot(h_ref[rows, :], w_ref[:, col:col + ATTN_WIDTH],
                           preferred_element_type=F32)

        q = proj(0) * Q_SCALE_LOG2
        q_ref[rows, :] = q.astype(BF16)
        k = proj(ATTN_WIDTH)
        for hd in range(ATTN_HEADS):
            cols = slice(hd * HEAD_COLS, (hd + 1) * HEAD_COLS)
            kh = k[:, cols]
            ka_ref[rows, cols] = jnp.where(lane < HEAD_DIM, kh, biasa_ref[hd, rows, :]).astype(BF16)
            kb_ref[rows, cols] = jnp.where(lane >= HEAD_DIM, kh, biasb_ref[hd, rows, :]).astype(BF16)
            qnorm_ref[norm_row, hd] = _max_map_norm(q[:, cols], lane)
            knorm_ref[norm_row, hd] = _max_map_norm(kh, lane)
        v_ref[rows, :] = proj(2 * ATTN_WIDTH).astype(BF16)


def _qkv_proj(slopes, x2d, g_pre, w_in, w_attn_o, w_conv_o, w_out):
    n = x2d.shape[0]
    steps = n // PROJ_ROWS
    row_block = lambda cols: pl.BlockSpec((PROJ_ROWS, cols), lambda i: (i, 0))
    act = jax.ShapeDtypeStruct((n, ATTN_WIDTH), BF16)
    norms = jax.ShapeDtypeStruct((n // PROJ_SUB_ROWS, ATTN_HEADS), F32)
    weights = (w_in, w_attn_o, w_conv_o, w_out)
    assert all(w.shape[0] % (steps * BF16_SUBLANES) == 0 for w in weights)
    slab = lambda w: pl.BlockSpec((w.shape[0] // steps, w.shape[1]), lambda i: (i, 0))
    return pl.pallas_call(
        _qkv_proj_kernel,
        grid=(steps,),
        in_specs=[
            pl.BlockSpec(memory_space=pltpu.SMEM),
            row_block(D_MODEL),
            pl.BlockSpec((1, D_MODEL), lambda i: (0, 0)),
            pl.BlockSpec((D_MODEL, QKV_COLS), lambda i: (0, 0), pipeline_mode=pl.Buffered(1)),
        ] + [slab(w) for w in weights],
        out_specs=[row_block(ATTN_WIDTH), row_block(ATTN_WIDTH), row_block(ATTN_WIDTH),
                   row_block(ATTN_WIDTH), pl.BlockSpec(memory_space=pltpu.SMEM),
                   pl.BlockSpec(memory_space=pltpu.SMEM)]
        + [slab(w) for w in weights],
        out_shape=[act, act, act, act, norms, norms]
        + [jax.ShapeDtypeStruct(w.shape, BF16) for w in weights],
        scratch_shapes=[pltpu.VMEM((PROJ_ROWS, D_MODEL), BF16),
                        pltpu.VMEM((ATTN_HEADS, PROJ_ROWS, HEAD_COLS), F32),
                        pltpu.VMEM((ATTN_HEADS, PROJ_ROWS, HEAD_COLS), F32),
                        pltpu.VMEM((D_MODEL, QKV_COLS), BF16)],
        compiler_params=pltpu.CompilerParams(
            dimension_semantics=("arbitrary",), vmem_limit_bytes=VMEM_LIMIT_BYTES),
        name="qkv_proj",
    )(slopes, x2d, g_pre, w_in, w_in, w_attn_o, w_conv_o, w_out)


def _diff_attn_kernel(slopes_ref, qnorm_ref, knorm_ref, q_ref, ka_ref, kb_ref, v_ref,
                      o0_ref, o1_ref, l_ref, m_ref, acc_ref):
    tq, tk, ts = ATTN_Q_ROWS, ATTN_K_ROWS, ATTN_SUB_ROWS
    th = PROJ_SUB_ROWS
    batch = pl.program_id(0)
    head = pl.program_id(1)
    qi = pl.program_id(2)
    slope = slopes_ref[head] * LOG2_E
    seq_norm_row = batch * (ka_ref.shape[0] // th)

    q = q_ref[...].astype(F32)
    qlane = lax.broadcasted_iota(jnp.int32, q.shape, 1)
    q_maps = (
        jnp.where(qlane < HEAD_DIM, q, jnp.where(qlane < HEAD_DIM + 3, 1.0, 0.0)).astype(BF16),
        jnp.where(qlane >= HEAD_DIM, q, jnp.where(qlane < 3, 1.0, 0.0)).astype(BF16),
    )
    key_refs = (ka_ref, kb_ref)

    tri_keep = (lax.broadcasted_iota(jnp.int32, (ts, ts), 1)
                <= lax.broadcasted_iota(jnp.int32, (ts, ts), 0))

    def step(key_start, diag_cols, row_stop=tq, key_lo=0):
        span_bias = (key_start - qi * tq).astype(F32) * slope
        first = diag_cols is not None
        streams = [(r, mp)
                   for r in (range(tq - ts, -ts, -ts) if first else range(0, row_stop, ts))
                   for mp in range(2)]

        def width(r):
            return tk - key_lo if diag_cols is None else diag_cols + r + ts

        def scores(r, mp):
            keys = key_refs[mp][pl.ds(key_start + key_lo, width(r)), :]
            s = lax.dot_general(q_maps[mp][r:r + ts], keys, (((1,), (1,)), ((), ())),
                                preferred_element_type=F32)
            if diag_cols is not None:
                below = width(r) - ts
                blocks = [s[:, :below]] if below else []
                s = jnp.concatenate(blocks + [jnp.where(tri_keep, s[:, below:], MASK_VALUE)],
                                    axis=1)
            return s

        probs = {}

        def update(r, mp, s):
            m_cur = jnp.max(s, axis=-1, keepdims=True) + span_bias
            if first:
                m_new = jnp.broadcast_to(m_cur, (ts, HEAD_COLS))
                alpha = None
            else:
                m_prev = m_ref[mp, r:r + ts]
                m_new = jnp.maximum(m_prev, m_cur)
                alpha = jnp.exp2(m_prev - m_new)
            shift = m_new - span_bias
            p = jnp.exp2(s - jnp.concatenate([shift] * (width(r) // HEAD_COLS), axis=1))
            m_ref[mp, r:r + ts] = m_new
            probs[mp] = (p.astype(BF16), alpha)
            if mp == 0:
                return
            v_ones = jnp.concatenate([v_ref[pl.ds(key_start + key_lo, width(r)), :],
                                      jnp.ones((width(r), HEAD_COLS), BF16)], axis=1)
            pv_pair = jnp.dot(jnp.concatenate([probs[0][0], probs[1][0]], axis=0), v_ones,
                              preferred_element_type=F32)
            for mq in range(2):
                pv = pv_pair[mq * ts:(mq + 1) * ts]
                alpha = probs[mq][1]
                if alpha is not None:
                    pv = jnp.concatenate([alpha, alpha], axis=1) * acc_ref[mq, r:r + ts] + pv
                acc_ref[mq, r:r + ts] = pv

        pending = [scores(*st) for st in streams[:ATTN_LOOKAHEAD]]
        for i, st in enumerate(streams):
            if i + ATTN_LOOKAHEAD < len(streams):
                pending.append(scores(*streams[i + ATTN_LOOKAHEAD]))
            update(*st, pending.pop(0))

    n_spans = lax.shift_right_logical(qi * tq, TK_SHIFT)
    tail_start = pl.multiple_of(n_spans * tk, tk)
    for extra in range(0, tk, tq):
        @pl.when(qi * tq - tail_start == extra)
        def _diag_step(extra=extra):
            step(tail_start, extra)

    def row_block_bound(hf):
        m_low = jnp.minimum(m_ref[0, hf * th:(hf + 1) * th], m_ref[1, hf * th:(hf + 1) * th])
        m_low = jnp.min(jnp.min(m_low, axis=0, keepdims=True), axis=1, keepdims=True)[0, 0]
        q_norm = qnorm_ref[seq_norm_row + qi * (tq // th) + hf, head]
        return q_norm, m_low - ALIBI_SKIP_BITS

    row_blocks = [row_block_bound(hf) for hf in range(tq // th)]

    def full_span(kj, carry):
        key_start = pl.multiple_of(kj * tk, tk)
        rows_needed = jnp.int32(0)
        keys_from = jnp.int32(tk)
        for kh in reversed(range(tk // th)):
            k_norm = knorm_ref[seq_norm_row + kj * (tk // th) + kh, head]
            alibi_max = (key_start + ((kh + 1) * th - 1) - qi * tq).astype(F32) * slope
            for hf, (q_norm, floor) in enumerate(row_blocks):
                live = q_norm * k_norm + alibi_max > floor
                rows_needed = jnp.where(live, jnp.maximum(rows_needed, (hf + 1) * th),
                                        rows_needed)
                keys_from = jnp.where(live, kh * th, keys_from)
        for hf in range(tq // th):
            for kh in range(tk // th):
                @pl.when(jnp.logical_and(rows_needed == (hf + 1) * th, keys_from == kh * th))
                def _fold_span(hf=hf, kh=kh):
                    step(key_start, None, row_stop=(hf + 1) * th, key_lo=kh * th)

        return carry

    lax.fori_loop(0, n_spans, full_span, 0)

    for mp, o_ref in enumerate((o0_ref, o1_ref)):
        o_ref[...] = acc_ref[mp, :, 0:HEAD_COLS].astype(BF16)
    lane = lax.broadcasted_iota(jnp.int32, (tq, HEAD_COLS), 1)
    row_sums = jnp.where(lane == 0, acc_ref[0, :, HEAD_COLS:2 * HEAD_COLS],
                         acc_ref[1, :, HEAD_COLS:2 * HEAD_COLS])
    l_ref[0] = row_sums[:, 0:2]


def _diff_attn(q, ka, kb, v, slopes, qnorm, knorm, *, batch, seq):
    nq = seq // ATTN_Q_ROWS
    seq_block = pl.BlockSpec((seq, HEAD_COLS), lambda b, h, i: (b, h))
    q_block = pl.BlockSpec((ATTN_Q_ROWS, HEAD_COLS), lambda b, h, i: (b * nq + i, h))
    return pl.pallas_call(
        _diff_attn_kernel,
        grid=(batch, ATTN_HEADS, nq),
        in_specs=[
            pl.BlockSpec(memory_space=pltpu.SMEM), pl.BlockSpec(memory_space=pltpu.SMEM),
            pl.BlockSpec(memory_space=pltpu.SMEM),
            q_block, seq_block, seq_block, seq_block,
        ],
        out_specs=[q_block, q_block,
                   pl.BlockSpec((1, ATTN_Q_ROWS, 2), lambda b, h, i: (h, b * nq + i, 0))],
        out_shape=[jax.ShapeDtypeStruct((batch * seq, ATTN_WIDTH), BF16)] * 2
        + [jax.ShapeDtypeStruct((ATTN_HEADS, batch * seq, 2), F32)],
        scratch_shapes=[pltpu.VMEM((2, ATTN_Q_ROWS, HEAD_COLS), F32),
                        pltpu.VMEM((2, ATTN_Q_ROWS, 2 * HEAD_COLS), F32)],
        compiler_params=pltpu.CompilerParams(
            dimension_semantics=("arbitrary", "arbitrary", "arbitrary"),
            vmem_limit_bytes=VMEM_LIMIT_BYTES),
        name="diff_attn",
    )(slopes, qnorm, knorm, q, ka, kb, v)


def _sigmoid(z):
    return 0.5 * jnp.tanh(0.5 * z) + 0.5


def _out_proj_kernel(x_ref, o0_ref, o1_ref, l_ref, lq1_ref, lk1_ref, lq2_ref, lk2_ref, subln_ref,
                     gpre_ref, wmid_ref, whi_ref, cw_ref, wa_ref, wc_ref, bm_ref, wo_ref, gp_ref,
                     out_ref, h_ref, u_ref, *, tiles_per_seq, lam_init):
    tm, ts, cwid = OUT_ROWS, OUT_SUB_ROWS, CONV_WIDTH

    @pl.when(pl.program_id(0) % tiles_per_seq == 0)
    def _sequence_start():
        u_ref[0:HALO_ROWS, :] = jnp.zeros((HALO_ROWS, cwid), F32)

    blocks = range(0, tm, ts)

    def proj(r, w_ref, col, cols):
        return jnp.dot(h_ref[r:r + ts, :], w_ref[:, col:col + cols], preferred_element_type=F32)

    for r in blocks:
        h_ref[r:r + ts, :] = _rms_norm_bf16(x_ref[r:r + ts, :], gpre_ref[...])

    for r in blocks:
        u = proj(r, wmid_ref, ATTN_WIDTH + cwid, cwid) * proj(r, whi_ref, 0, cwid)
        u_ref[HALO_ROWS + r:HALO_ROWS + r + ts, :] = u
    y_conv = {}
    for r in blocks:
        conv = cw_ref[CONV_K - 1:CONV_K, :] * u_ref[HALO_ROWS + r:HALO_ROWS + r + ts, :]
        for k in range(CONV_K - 1):
            first_row = HALO_ROWS + r - (CONV_K - 1 - k)
            conv = conv + cw_ref[k:k + 1, :] * u_ref[first_row:first_row + ts, :]
        z_c = proj(r, whi_ref, cwid, cwid)
        yc_in = (proj(r, wmid_ref, ATTN_WIDTH, cwid) * conv) * (z_c * _sigmoid(z_c))
        y_conv[r] = jnp.dot(yc_in.astype(BF16), wc_ref[...], preferred_element_type=F32)
    u_ref[0:HALO_ROWS, :] = u_ref[tm:tm + HALO_ROWS, :]

    g_a = {}
    for r in blocks:
        g_a[r] = _sigmoid(proj(r, whi_ref, 2 * cwid, D_MODEL) + bm_ref[:, 0:D_MODEL])

    lam = (jnp.exp(jnp.sum(lq1_ref[...] * lk1_ref[...], axis=-1, keepdims=True))
           - jnp.exp(jnp.sum(lq2_ref[...] * lk2_ref[...], axis=-1, keepdims=True))
           + lam_init)
    subln_gain = subln_ref[...] * (1.0 - lam_init)
    y_merged = {}
    for r in blocks:
        heads = []
        for hd in range(ATTN_HEADS):
            cols = slice(hd * HEAD_COLS, (hd + 1) * HEAD_COLS)
            inv = 1.0 / l_ref[hd, r:r + ts, :]
            oh = (o0_ref[r:r + ts, cols].astype(F32) * inv[:, 0:1]
                  - lam * (o1_ref[r:r + ts, cols].astype(F32) * inv[:, 1:2]))
            ms = jnp.mean(oh * oh, axis=-1, keepdims=True)
            heads.append(oh * lax.rsqrt(ms + NORM_EPS) * subln_gain)
        z_a = proj(r, wmid_ref, 0, ATTN_WIDTH)
        ya_in = jnp.concatenate(heads, axis=1) * (z_a * _sigmoid(z_a))
        y_attn = jnp.dot(ya_in.astype(BF16), wa_ref[...], preferred_element_type=F32)
        y_merged[r] = g_a[r] * y_attn

    for r in blocks:
        g_c = _sigmoid(proj(r, whi_ref, 2 * cwid + D_MODEL, D_MODEL)
                       + bm_ref[:, D_MODEL:2 * D_MODEL])
        y_merged[r] = (y_merged[r] + g_c * y_conv[r]).astype(BF16)
    for r in blocks:
        out = jnp.dot(y_merged[r], wo_ref[...], preferred_element_type=F32)
        ms = jnp.mean(out * out, axis=-1, keepdims=True)
        out_ref[r:r + ts, :] = x_ref[r:r + ts, :] + out * lax.rsqrt(ms + NORM_EPS) * gp_ref[...]


def _out_proj(x2d, o0, o1, row_sums, lq1, lk1, lq2, lk2, subln_gain, g_pre, w_in_bf16, conv_w,
              wa, wc, b_merge, wo, g_post, *, seq, lam_init):
    n = x2d.shape[0]
    mid_cols = ATTN_WIDTH + 2 * CONV_WIDTH
    hi_cols = 2 * CONV_WIDTH + GATE_COLS
    assert QKV_COLS == mid_cols and QKV_COLS + mid_cols == hi_cols == IN_COLS - hi_cols
    row_block = lambda cols: pl.BlockSpec((OUT_ROWS, cols), lambda i: (i, 0))
    const = lambda shape, col_block=0: pl.BlockSpec(shape, lambda i: (0, col_block),
                                                   pipeline_mode=pl.Buffered(1))
    return pl.pallas_call(
        functools.partial(_out_proj_kernel, tiles_per_seq=seq // OUT_ROWS, lam_init=lam_init),
        grid=(n // OUT_ROWS,),
        in_specs=[
            row_block(D_MODEL), row_block(ATTN_WIDTH), row_block(ATTN_WIDTH),
            pl.BlockSpec((ATTN_HEADS, OUT_ROWS, 2), lambda i: (0, i, 0)),
            const((1, HEAD_DIM)), const((1, HEAD_DIM)), const((1, HEAD_DIM)), const((1, HEAD_DIM)),
            const((1, HEAD_COLS)), const((1, D_MODEL)),
            const((D_MODEL, mid_cols), 1), const((D_MODEL, hi_cols), 1),
            const((CONV_K, CONV_WIDTH)), const((ATTN_WIDTH, D_MODEL)),
            const((CONV_WIDTH, D_MODEL)), const((1, 2 * D_MODEL)), const((D_MODEL, D_MODEL)),
            const((1, D_MODEL)),
        ],
        out_specs=row_block(D_MODEL),
        out_shape=jax.ShapeDtypeStruct((n, D_MODEL), F32),
        scratch_shapes=[pltpu.VMEM((OUT_ROWS, D_MODEL), BF16),
                        pltpu.VMEM((HALO_ROWS + OUT_ROWS, CONV_WIDTH), F32)],
        compiler_params=pltpu.CompilerParams(
            dimension_semantics=("arbitrary",), vmem_limit_bytes=VMEM_LIMIT_BYTES),
        name="out_proj",
    )(x2d, o0, o1, row_sums, lq1, lk1, lq2, lk2, subln_gain, g_pre, w_in_bf16, w_in_bf16, conv_w,
      wa, wc, b_merge, wo, g_post)


def _layer(x, layer_idx, w_in, lq1, lk1, lq2, lk2, subln_gain, conv_w, w_attn_o, w_conv_o,
           b_merge, w_out, g_pre, g_post):
    batch, seq, d = x.shape
    x2d = x.reshape(batch * seq, d)
    row = lambda a: a.reshape(1, -1).astype(F32)
    slopes = jnp.asarray([2.0 ** (-8.0 * (i + 1) / ATTN_HEADS) for i in range(ATTN_HEADS)], F32)

    q, ka, kb, v, qnorm, knorm, w_in_bf16, wa_bf16, wc_bf16, wo_bf16 = _qkv_proj(
        slopes, x2d, row(g_pre), w_in, w_attn_o, w_conv_o, w_out)
    o0, o1, row_sums = _diff_attn(q, ka, kb, v, slopes, qnorm, knorm, batch=batch, seq=seq)
    out = _out_proj(x2d, o0, o1, row_sums, row(lq1), row(lk1), row(lq2), row(lk2), row(subln_gain),
                    row(g_pre), w_in_bf16, conv_w.astype(F32), wa_bf16, wc_bf16, row(b_merge),
                    wo_bf16, row(g_post), seq=seq, lam_init=_lambda_init(layer_idx))
    return out.reshape(batch, seq, d)


def kernel(x, w_in, lambda_q1, lambda_k1, lambda_q2, lambda_k2, subln_gain, conv_w, w_attn_o,
           w_conv_o, b_merge, w_out, g_pre, g_post):
    for l in range(w_in.shape[0]):
        x = _layer(x, l, w_in[l], lambda_q1[l], lambda_k1[l], lambda_q2[l], lambda_k2[l],
                   subln_gain[l], conv_w[l], w_attn_o[l], w_conv_o[l], b_merge[l], w_out[l],
                   g_pre[l], g_post[l])
    return x
```

```python
import functools
import math

import jax
import jax.numpy as jnp
from jax import lax
from jax.experimental import pallas as pl
from jax.experimental.pallas import tpu as pltpu

D_MODEL = 1024
ATTN_HEADS = 4
HEAD_DIM = 64
HEAD_COLS = 2 * HEAD_DIM
ATTN_WIDTH = ATTN_HEADS * HEAD_COLS
CONV_WIDTH = D_MODEL // 2
CONV_K = 3
NORM_EPS = 1e-6
QKV_COLS = 3 * ATTN_WIDTH
CONV_COLS = 4 * CONV_WIDTH
GATE_COLS = 2 * D_MODEL
IN_COLS = QKV_COLS + ATTN_WIDTH + CONV_COLS + GATE_COLS
MASK_VALUE = -1e30
LOG2_E = math.log2(math.e)
Q_SCALE_LOG2 = HEAD_DIM ** -0.5 * LOG2_E
ALIBI_SKIP_BITS = 160.0
NORM_SLACK = 1.01

PROJ_ROWS = 1024
PROJ_SUB_ROWS = 512
X_BUFFERS = 3
ATTN_Q_ROWS = 1024
ATTN_K_ROWS = 1024
TK_SHIFT = ATTN_K_ROWS.bit_length() - 1
ATTN_SUB_ROWS = 256
ATTN_LOOKAHEAD = 3
assert ATTN_K_ROWS == 1 << TK_SHIFT and ATTN_K_ROWS % ATTN_Q_ROWS == 0
assert PROJ_ROWS == ATTN_K_ROWS
assert ATTN_Q_ROWS % PROJ_SUB_ROWS == 0 and PROJ_SUB_ROWS % ATTN_SUB_ROWS == 0
OUT_ROWS = 1024
OUT_SUB_ROWS = 512
BF16_SUBLANES = 16
HALO_ROWS = BF16_SUBLANES
assert HALO_ROWS >= CONV_K - 1
V7X_VMEM_BYTES = 64 * 1024 * 1024
VMEM_LIMIT_BYTES = V7X_VMEM_BYTES * 7 // 8

BF16 = jnp.bfloat16
F32 = jnp.float32


def _lambda_init(layer_idx):
    return 0.8 - 0.6 * math.exp(-0.3 * layer_idx)


def _rms_norm_bf16(x, gain):
    ms = jnp.mean(x * x, axis=-1, keepdims=True)
    return (x * lax.rsqrt(ms + NORM_EPS) * gain).astype(BF16)


def _max_map_norm(x, lane):
    sq = x * x
    n0 = jnp.sum(jnp.where(lane < HEAD_DIM, sq, 0.0), axis=-1, keepdims=True)
    n1 = jnp.sum(jnp.where(lane >= HEAD_DIM, sq, 0.0), axis=-1, keepdims=True)
    return jnp.sqrt(jnp.max(jnp.maximum(n0, n1), axis=0, keepdims=True))[0, 0] * NORM_SLACK


def _qkv_proj_kernel(slopes_ref, x_hbm_ref, g_ref, wqkv_ref, win_ref, wa_ref, wc_ref, wo_ref,
                     q_ref, ka_ref, kb_ref, v_ref, qnorm_ref, knorm_ref, win16_ref, wa16_ref,
                     wc16_ref, wo16_ref, h_ref, biasa_ref, biasb_ref, w_ref, xbuf_ref, xsem_ref):
    step = pl.program_id(0)

    def x_copy(tile):
        tile = jnp.asarray(tile, jnp.int32)
        slot = lax.rem(tile, X_BUFFERS)
        return pltpu.make_async_copy(
            x_hbm_ref.at[pl.ds(pl.multiple_of(tile * PROJ_ROWS, PROJ_ROWS), PROJ_ROWS), :],
            xbuf_ref.at[slot], xsem_ref.at[slot])

    @pl.when(step == 0)
    def _prime_ring():
        for tile in range(X_BUFFERS - 1):
            x_copy(tile).start()

    @pl.when(step + (X_BUFFERS - 1) < pl.num_programs(0))
    def _prefetch():
        x_copy(step + (X_BUFFERS - 1)).start()

    x_copy(step).wait()
    x_ref = xbuf_ref.at[lax.rem(step, X_BUFFERS)]

    for w32_ref, w16_ref in ((win_ref, win16_ref), (wa_ref, wa16_ref), (wc_ref, wc16_ref),
                             (wo_ref, wo16_ref)):
        w16_ref[...] = w32_ref[...].astype(BF16)

    @pl.when(pl.program_id(0) == 0)
    def _first_step():
        w_ref[...] = wqkv_ref[...].astype(BF16)
        lane = lax.broadcasted_iota(jnp.int32, (PROJ_ROWS, HEAD_COLS), 1)
        row = lax.broadcasted_iota(jnp.int32, (PROJ_ROWS, HEAD_COLS), 0).astype(F32)
        for hd in range(ATTN_HEADS):
            b = row * (slopes_ref[hd] * LOG2_E)
            b1 = b.astype(BF16).astype(F32)
            b2 = (b - b1).astype(BF16).astype(F32)
            b3 = (b - b1 - b2).astype(BF16).astype(F32)
            for bias_ref, first in ((biasa_ref, HEAD_DIM), (biasb_ref, 0)):
                bias_ref[hd] = jnp.where(lane == first, b1, jnp.where(
                    lane == first + 1, b2, jnp.where(lane == first + 2, b3, 0.0)))

    lane = lax.broadcasted_iota(jnp.int32, (PROJ_SUB_ROWS, HEAD_COLS), 1)
    for r in range(0, PROJ_ROWS, PROJ_SUB_ROWS):
        rows = slice(r, r + PROJ_SUB_ROWS)
        norm_row = pl.program_id(0) * (PROJ_ROWS // PROJ_SUB_ROWS) + r // PROJ_SUB_ROWS
        h_ref[rows, :] = _rms_norm_bf16(x_ref[rows, :], g_ref[...])

        def proj(col, rows=rows):
            return jnp.dot(h_ref[rows, :], w_ref[:, col:col + ATTN_WIDTH],
                           preferred_element_type=F32)

        q = proj(0) * Q_SCALE_LOG2
        q_ref[rows, :] = q.astype(BF16)
        k = proj(ATTN_WIDTH)
        for hd in range(ATTN_HEADS):
            cols = slice(hd * HEAD_COLS, (hd + 1) * HEAD_COLS)
            kh = k[:, cols]
            ka_ref[rows, cols] = jnp.where(lane < HEAD_DIM, kh, biasa_ref[hd, rows, :]).astype(BF16)
            kb_ref[rows, cols] = jnp.where(lane >= HEAD_DIM, kh, biasb_ref[hd, rows, :]).astype(BF16)
            qnorm_ref[norm_row, hd] = _max_map_norm(q[:, cols], lane)
            knorm_ref[norm_row, hd] = _max_map_norm(kh, lane)
        v_ref[rows, :] = proj(2 * ATTN_WIDTH).astype(BF16)


def _qkv_proj(slopes, x2d, g_pre, w_in, w_attn_o, w_conv_o, w_out):
    n = x2d.shape[0]
    steps = n // PROJ_ROWS
    row_block = lambda cols: pl.BlockSpec((PROJ_ROWS, cols), lambda i: (i, 0))
    act = jax.ShapeDtypeStruct((n, ATTN_WIDTH), BF16)
    norms = jax.ShapeDtypeStruct((n // PROJ_SUB_ROWS, ATTN_HEADS), F32)
    weights = (w_in, w_attn_o, w_conv_o, w_out)
    assert all(w.shape[0] % (steps * BF16_SUBLANES) == 0 for w in weights)
    assert steps >= X_BUFFERS - 1
    slab = lambda w: pl.BlockSpec((w.shape[0] // steps, w.shape[1]), lambda i: (i, 0))
    return pl.pallas_call(
        _qkv_proj_kernel,
        grid=(steps,),
        in_specs=[
            pl.BlockSpec(memory_space=pltpu.SMEM),
            pl.BlockSpec(memory_space=pl.ANY),
            pl.BlockSpec((1, D_MODEL), lambda i: (0, 0)),
            pl.BlockSpec((D_MODEL, QKV_COLS), lambda i: (0, 0), pipeline_mode=pl.Buffered(1)),
        ] + [slab(w) for w in weights],
        out_specs=[row_block(ATTN_WIDTH), row_block(ATTN_WIDTH), row_block(ATTN_WIDTH),
                   row_block(ATTN_WIDTH), pl.BlockSpec(memory_space=pltpu.SMEM),
                   pl.BlockSpec(memory_space=pltpu.SMEM)]
        + [slab(w) for w in weights],
        out_shape=[act, act, act, act, norms, norms]
        + [jax.ShapeDtypeStruct(w.shape, BF16) for w in weights],
        scratch_shapes=[pltpu.VMEM((PROJ_ROWS, D_MODEL), BF16),
                        pltpu.VMEM((ATTN_HEADS, PROJ_ROWS, HEAD_COLS), F32),
                        pltpu.VMEM((ATTN_HEADS, PROJ_ROWS, HEAD_COLS), F32),
                        pltpu.VMEM((D_MODEL, QKV_COLS), BF16),
                        pltpu.VMEM((X_BUFFERS, PROJ_ROWS, D_MODEL), F32),
                        pltpu.SemaphoreType.DMA((X_BUFFERS,))],
        compiler_params=pltpu.CompilerParams(
            dimension_semantics=("arbitrary",), vmem_limit_bytes=VMEM_LIMIT_BYTES),
        name="qkv_proj",
    )(slopes, x2d, g_pre, w_in, w_in, w_attn_o, w_conv_o, w_out)


def _diff_attn_kernel(slopes_ref, qnorm_ref, knorm_ref, q_ref, ka_ref, kb_ref, v_ref,
                      o0_ref, o1_ref, m_ref, acc_ref):
    tq, tk, ts = ATTN_Q_ROWS, ATTN_K_ROWS, ATTN_SUB_ROWS
    th = PROJ_SUB_ROWS
    batch = pl.program_id(0)
    head = pl.program_id(1)
    qi = pl.program_id(2)
    slope = slopes_ref[head] * LOG2_E
    seq_norm_row = batch * (ka_ref.shape[0] // th)

    q = q_ref[...].astype(F32)
    qlane = lax.broadcasted_iota(jnp.int32, q.shape, 1)
    q_maps = (
        jnp.where(qlane < HEAD_DIM, q, jnp.where(qlane < HEAD_DIM + 3, 1.0, 0.0)).astype(BF16),
        jnp.where(qlane >= HEAD_DIM, q, jnp.where(qlane < 3, 1.0, 0.0)).astype(BF16),
    )
    key_refs = (ka_ref, kb_ref)

    tri_keep = (lax.broadcasted_iota(jnp.int32, (ts, ts), 1)
                <= lax.broadcasted_iota(jnp.int32, (ts, ts), 0))

    def step(key_start, diag_cols, row_stop=tq, key_lo=0):
        span_bias = (key_start - qi * tq).astype(F32) * slope
        first = diag_cols is not None
        streams = [(r, mp)
                   for r in (range(tq - ts, -ts, -ts) if first else range(0, row_stop, ts))
                   for mp in range(2)]

        def width(r):
            return tk - key_lo if diag_cols is None else diag_cols + r + ts

        def scores(r, mp):
            keys = key_refs[mp][pl.ds(key_start + key_lo, width(r)), :]
            s = lax.dot_general(q_maps[mp][r:r + ts], keys, (((1,), (1,)), ((), ())),
                                preferred_element_type=F32)
            if diag_cols is not None:
                below = width(r) - ts
                blocks = [s[:, :below]] if below else []
                s = jnp.concatenate(blocks + [jnp.where(tri_keep, s[:, below:], MASK_VALUE)],
                                    axis=1)
            return s

        probs = {}

        def update(r, mp, s):
            m_cur = jnp.max(s, axis=-1, keepdims=True) + span_bias
            if first:
                m_new = jnp.broadcast_to(m_cur, (ts, HEAD_COLS))
                alpha = None
            else:
                m_prev = m_ref[mp, r:r + ts]
                m_new = jnp.maximum(m_prev, m_cur)
                alpha = jnp.exp2(m_prev - m_new)
            shift = m_new - span_bias
            p = jnp.exp2(s - jnp.concatenate([shift] * (width(r) // HEAD_COLS), axis=1))
            m_ref[mp, r:r + ts] = m_new
            probs[mp] = (p.astype(BF16), alpha)
            if mp == 0:
                return
            v_ones = jnp.concatenate([v_ref[pl.ds(key_start + key_lo, width(r)), :],
                                      jnp.ones((width(r), HEAD_COLS), BF16)], axis=1)
            pv_pair = jnp.dot(jnp.concatenate([probs[0][0], probs[1][0]], axis=0), v_ones,
                              preferred_element_type=F32)
            for mq in range(2):
                pv = pv_pair[mq * ts:(mq + 1) * ts]
                alpha = probs[mq][1]
                if alpha is not None:
                    pv = jnp.concatenate([alpha, alpha], axis=1) * acc_ref[mq, r:r + ts] + pv
                acc_ref[mq, r:r + ts] = pv

        pending = [scores(*st) for st in streams[:ATTN_LOOKAHEAD]]
        for i, st in enumerate(streams):
            if i + ATTN_LOOKAHEAD < len(streams):
                pending.append(scores(*streams[i + ATTN_LOOKAHEAD]))
            update(*st, pending.pop(0))

    n_spans = lax.shift_right_logical(qi * tq, TK_SHIFT)
    tail_start = pl.multiple_of(n_spans * tk, tk)
    for extra in range(0, tk, tq):
        @pl.when(qi * tq - tail_start == extra)
        def _diag_step(extra=extra):
            step(tail_start, extra)

    def row_block_bound(hf):
        m_low = jnp.minimum(m_ref[0, hf * th:(hf + 1) * th], m_ref[1, hf * th:(hf + 1) * th])
        m_low = jnp.min(jnp.min(m_low, axis=0, keepdims=True), axis=1, keepdims=True)[0, 0]
        q_norm = qnorm_ref[seq_norm_row + qi * (tq // th) + hf, head]
        return q_norm, m_low - ALIBI_SKIP_BITS

    row_blocks = [row_block_bound(hf) for hf in range(tq // th)]

    def full_span(kj, carry):
        key_start = pl.multiple_of(kj * tk, tk)
        rows_needed = jnp.int32(0)
        keys_from = jnp.int32(tk)
        for kh in reversed(range(tk // th)):
            k_norm = knorm_ref[seq_norm_row + kj * (tk // th) + kh, head]
            alibi_max = (key_start + ((kh + 1) * th - 1) - qi * tq).astype(F32) * slope
            for hf, (q_norm, floor) in enumerate(row_blocks):
                live = q_norm * k_norm + alibi_max > floor
                rows_needed = jnp.where(live, jnp.maximum(rows_needed, (hf + 1) * th),
                                        rows_needed)
                keys_from = jnp.where(live, kh * th, keys_from)
        for hf in range(tq // th):
            for kh in range(tk // th):
                @pl.when(jnp.logical_and(rows_needed == (hf + 1) * th, keys_from == kh * th))
                def _fold_span(hf=hf, kh=kh):
                    step(key_start, None, row_stop=(hf + 1) * th, key_lo=kh * th)

        return carry

    lax.fori_loop(0, n_spans, full_span, 0)

    for mp, o_ref in enumerate((o0_ref, o1_ref)):
        o_ref[...] = (acc_ref[mp, :, 0:HEAD_COLS]
                      / acc_ref[mp, :, HEAD_COLS:2 * HEAD_COLS]).astype(BF16)


def _diff_attn(q, ka, kb, v, slopes, qnorm, knorm, *, batch, seq):
    nq = seq // ATTN_Q_ROWS
    seq_block = pl.BlockSpec((seq, HEAD_COLS), lambda b, h, i: (b, h))
    q_block = pl.BlockSpec((ATTN_Q_ROWS, HEAD_COLS), lambda b, h, i: (b * nq + i, h))
    return pl.pallas_call(
        _diff_attn_kernel,
        grid=(batch, ATTN_HEADS, nq),
        in_specs=[
            pl.BlockSpec(memory_space=pltpu.SMEM), pl.BlockSpec(memory_space=pltpu.SMEM),
            pl.BlockSpec(memory_space=pltpu.SMEM),
            q_block, seq_block, seq_block, seq_block,
        ],
        out_specs=[q_block, q_block],
        out_shape=[jax.ShapeDtypeStruct((batch * seq, ATTN_WIDTH), BF16)] * 2,
        scratch_shapes=[pltpu.VMEM((2, ATTN_Q_ROWS, HEAD_COLS), F32),
                        pltpu.VMEM((2, ATTN_Q_ROWS, 2 * HEAD_COLS), F32)],
        compiler_params=pltpu.CompilerParams(
            dimension_semantics=("arbitrary", "arbitrary", "arbitrary"),
            vmem_limit_bytes=VMEM_LIMIT_BYTES),
        name="diff_attn",
    )(slopes, qnorm, knorm, q, ka, kb, v)


def _sigmoid(z):
    return 0.5 * jnp.tanh(0.5 * z) + 0.5


def _out_proj_kernel(x_ref, o0_ref, o1_ref, lq1_ref, lk1_ref, lq2_ref, lk2_ref, subln_ref,
                     gpre_ref, wmid_ref, whi_ref, cw_ref, wa_ref, wc_ref, bm_ref, wo_ref, gp_ref,
                     out_ref, h_ref, u_ref, *, tiles_per_seq, lam_init):
    tm, ts, cwid = OUT_ROWS, OUT_SUB_ROWS, CONV_WIDTH

    @pl.when(pl.program_id(0) % tiles_per_seq == 0)
    def _sequence_start():
        u_ref[0:HALO_ROWS, :] = jnp.zeros((HALO_ROWS, cwid), F32)

    blocks = range(0, tm, ts)

    def proj(r, w_ref, col, cols):
        return jnp.dot(h_ref[r:r + ts, :], w_ref[:, col:col + cols], preferred_element_type=F32)

    for r in blocks:
        h_ref[r:r + ts, :] = _rms_norm_bf16(x_ref[r:r + ts, :], gpre_ref[...])

    for r in blocks:
        u = proj(r, wmid_ref, ATTN_WIDTH + cwid, cwid) * proj(r, whi_ref, 0, cwid)
        u_ref[HALO_ROWS + r:HALO_ROWS + r + ts, :] = u
    y_conv = {}
    for r in blocks:
        conv = cw_ref[CONV_K - 1:CONV_K, :] * u_ref[HALO_ROWS + r:HALO_ROWS + r + ts, :]
        for k in range(CONV_K - 1):
            first_row = HALO_ROWS + r - (CONV_K - 1 - k)
            conv = conv + cw_ref[k:k + 1, :] * u_ref[first_row:first_row + ts, :]
        z_c = proj(r, whi_ref, cwid, cwid)
        yc_in = (proj(r, wmid_ref, ATTN_WIDTH, cwid) * conv) * (z_c * _sigmoid(z_c))
        y_conv[r] = jnp.dot(yc_in.astype(BF16), wc_ref[...], preferred_element_type=F32)
    u_ref[0:HALO_ROWS, :] = u_ref[tm:tm + HALO_ROWS, :]

    g_a = {}
    for r in blocks:
        g_a[r] = _sigmoid(proj(r, whi_ref, 2 * cwid, D_MODEL) + bm_ref[:, 0:D_MODEL])

    lam = (jnp.exp(jnp.sum(lq1_ref[...] * lk1_ref[...], axis=-1, keepdims=True))
           - jnp.exp(jnp.sum(lq2_ref[...] * lk2_ref[...], axis=-1, keepdims=True))
           + lam_init)
    subln_gain = subln_ref[...] * (1.0 - lam_init)
    y_merged = {}
    for r in blocks:
        o = o0_ref[r:r + ts, :].astype(F32) - lam * o1_ref[r:r + ts, :].astype(F32)
        heads = []
        for hd in range(ATTN_HEADS):
            oh = o[:, hd * HEAD_COLS:(hd + 1) * HEAD_COLS]
            ms = jnp.mean(oh * oh, axis=-1, keepdims=True)
            heads.append(oh * lax.rsqrt(ms + NORM_EPS) * subln_gain)
        z_a = proj(r, wmid_ref, 0, ATTN_WIDTH)
        ya_in = jnp.concatenate(heads, axis=1) * (z_a * _sigmoid(z_a))
        y_attn = jnp.dot(ya_in.astype(BF16), wa_ref[...], preferred_element_type=F32)
        y_merged[r] = g_a[r] * y_attn

    for r in blocks:
        g_c = _sigmoid(proj(r, whi_ref, 2 * cwid + D_MODEL, D_MODEL)
                       + bm_ref[:, D_MODEL:2 * D_MODEL])
        y_merged[r] = (y_merged[r] + g_c * y_conv[r]).astype(BF16)
    for r in blocks:
        out = jnp.dot(y_merged[r], wo_ref[...], preferred_element_type=F32)
        ms = jnp.mean(out * out, axis=-1, keepdims=True)
        out_ref[r:r + ts, :] = x_ref[r:r + ts, :] + out * lax.rsqrt(ms + NORM_EPS) * gp_ref[...]


def _out_proj(x2d, o0, o1, lq1, lk1, lq2, lk2, subln_gain, g_pre, w_in_bf16, conv_w, wa, wc,
              b_merge, wo, g_post, *, seq, lam_init):
    n = x2d.shape[0]
    mid_cols = ATTN_WIDTH + 2 * CONV_WIDTH
    hi_cols = 2 * CONV_WIDTH + GATE_COLS
    assert QKV_COLS == mid_cols and QKV_COLS + mid_cols == hi_cols == IN_COLS - hi_cols
    row_block = lambda cols: pl.BlockSpec((OUT_ROWS, cols), lambda i: (i, 0))
    const = lambda shape, col_block=0: pl.BlockSpec(shape, lambda i: (0, col_block),
                                                   pipeline_mode=pl.Buffered(1))
    return pl.pallas_call(
        functools.partial(_out_proj_kernel, tiles_per_seq=seq // OUT_ROWS, lam_init=lam_init),
        grid=(n // OUT_ROWS,),
        in_specs=[
            row_block(D_MODEL), row_block(ATTN_WIDTH), row_block(ATTN_WIDTH),
            const((1, HEAD_DIM)), const((1, HEAD_DIM)), const((1, HEAD_DIM)), const((1, HEAD_DIM)),
            const((1, HEAD_COLS)), const((1, D_MODEL)),
            const((D_MODEL, mid_cols), 1), const((D_MODEL, hi_cols), 1),
            const((CONV_K, CONV_WIDTH)), const((ATTN_WIDTH, D_MODEL)),
            const((CONV_WIDTH, D_MODEL)), const((1, 2 * D_MODEL)), const((D_MODEL, D_MODEL)),
            const((1, D_MODEL)),
        ],
        out_specs=row_block(D_MODEL),
        out_shape=jax.ShapeDtypeStruct((n, D_MODEL), F32),
        scratch_shapes=[pltpu.VMEM((OUT_ROWS, D_MODEL), BF16),
                        pltpu.VMEM((HALO_ROWS + OUT_ROWS, CONV_WIDTH), F32)],
        compiler_params=pltpu.CompilerParams(
            dimension_semantics=("arbitrary",), vmem_limit_bytes=VMEM_LIMIT_BYTES),
        name="out_proj",
    )(x2d, o0, o1, lq1, lk1, lq2, lk2, subln_gain, g_pre, w_in_bf16, w_in_bf16, conv_w, wa, wc,
      b_merge, wo, g_post)


def _layer(x, layer_idx, w_in, lq1, lk1, lq2, lk2, subln_gain, conv_w, w_attn_o, w_conv_o,
           b_merge, w_out, g_pre, g_post):
    batch, seq, d = x.shape
    x2d = x.reshape(batch * seq, d)
    row = lambda a: a.reshape(1, -1).astype(F32)
    slopes = jnp.asarray([2.0 ** (-8.0 * (i + 1) / ATTN_HEADS) for i in range(ATTN_HEADS)], F32)

    q, ka, kb, v, qnorm, knorm, w_in_bf16, wa_bf16, wc_bf16, wo_bf16 = _qkv_proj(
        slopes, x2d, row(g_pre), w_in, w_attn_o, w_conv_o, w_out)
    o0, o1 = _diff_attn(q, ka, kb, v, slopes, qnorm, knorm, batch=batch, seq=seq)
    out = _out_proj(x2d, o0, o1, row(lq1), row(lk1), row(lq2), row(lk2), row(subln_gain),
                    row(g_pre), w_in_bf16, conv_w.astype(F32), wa_bf16, wc_bf16, row(b_merge),
                    wo_bf16, row(g_post), seq=seq, lam_init=_lambda_init(layer_idx))
    return out.reshape(batch, seq, d)


def kernel(x, w_in, lambda_q1, lambda_k1, lambda_q2, lambda_k2, subln_gain, conv_w, w_attn_o,
           w_conv_o, b_merge, w_out, g_pre, g_post):
    for l in range(w_in.shape[0]):
        x = _layer(x, l, w_in[l], lambda_q1[l], lambda_k1[l], lambda_q2[l], lambda_k2[l],
                   subln_gain[l], conv_w[l], w_attn_o[l], w_conv_o[l], b_merge[l], w_out[l],
                   g_pre[l], g_post[l])
    return x
```

```python
import functools
import math

import jax
import jax.numpy as jnp
from jax import lax
from jax.experimental import pallas as pl
from jax.experimental.pallas import tpu as pltpu

D_MODEL = 1024
ATTN_HEADS = 4
HEAD_DIM = 64
HEAD_COLS = 2 * HEAD_DIM
ATTN_WIDTH = ATTN_HEADS * HEAD_COLS
CONV_WIDTH = D_MODEL // 2
CONV_K = 3
NORM_EPS = 1e-6
QKV_COLS = 3 * ATTN_WIDTH
CONV_COLS = 4 * CONV_WIDTH
GATE_COLS = 2 * D_MODEL
IN_COLS = QKV_COLS + ATTN_WIDTH + CONV_COLS + GATE_COLS
MASK_VALUE = -1e30
LOG2_E = math.log2(math.e)
Q_SCALE_LOG2 = HEAD_DIM ** -0.5 * LOG2_E
ALIBI_SKIP_BITS = 160.0
NORM_SLACK = 1.01

PROJ_ROWS = 1024
PROJ_SUB_ROWS = 512
X_BUFFERS = 4
ATTN_Q_ROWS = 1024
ATTN_K_ROWS = 1024
TK_SHIFT = ATTN_K_ROWS.bit_length() - 1
ATTN_SUB_ROWS = 256
ATTN_LOOKAHEAD = 3
assert ATTN_K_ROWS == 1 << TK_SHIFT and ATTN_K_ROWS % ATTN_Q_ROWS == 0
assert PROJ_ROWS == ATTN_K_ROWS
assert ATTN_Q_ROWS % PROJ_SUB_ROWS == 0 and PROJ_SUB_ROWS % ATTN_SUB_ROWS == 0
OUT_ROWS = 1024
OUT_SUB_ROWS = 512
BF16_SUBLANES = 16
HALO_ROWS = BF16_SUBLANES
assert HALO_ROWS >= CONV_K - 1
V7X_VMEM_BYTES = 64 * 1024 * 1024
VMEM_LIMIT_BYTES = V7X_VMEM_BYTES * 7 // 8

BF16 = jnp.bfloat16
F32 = jnp.float32


def _lambda_init(layer_idx):
    return 0.8 - 0.6 * math.exp(-0.3 * layer_idx)


def _rms_norm_bf16(x, gain):
    ms = jnp.mean(x * x, axis=-1, keepdims=True)
    return (x * lax.rsqrt(ms + NORM_EPS) * gain).astype(BF16)


def _max_map_norm(x, lane):
    sq = x * x
    n0 = jnp.sum(jnp.where(lane < HEAD_DIM, sq, 0.0), axis=-1, keepdims=True)
    n1 = jnp.sum(jnp.where(lane >= HEAD_DIM, sq, 0.0), axis=-1, keepdims=True)
    return jnp.sqrt(jnp.max(jnp.maximum(n0, n1), axis=0, keepdims=True))[0, 0] * NORM_SLACK


def _qkv_proj_kernel(slopes_ref, x_hbm_ref, g_ref, wqkv_ref, win_ref, wa_ref, wc_ref, wo_ref,
                     q_ref, ka_ref, kb_ref, v_ref, qnorm_ref, knorm_ref, win16_ref, wa16_ref,
                     wc16_ref, wo16_ref, h_ref, biasa_ref, biasb_ref, w_ref, xbuf_ref, xsem_ref):
    step = pl.program_id(0)

    def x_copy(tile):
        tile = jnp.asarray(tile, jnp.int32)
        slot = lax.rem(tile, X_BUFFERS)
        return pltpu.make_async_copy(
            x_hbm_ref.at[pl.ds(pl.multiple_of(tile * PROJ_ROWS, PROJ_ROWS), PROJ_ROWS), :],
            xbuf_ref.at[slot], xsem_ref.at[slot])

    @pl.when(step == 0)
    def _prime_ring():
        for tile in range(X_BUFFERS - 1):
            x_copy(tile).start()

    @pl.when(step + (X_BUFFERS - 1) < pl.num_programs(0))
    def _prefetch():
        x_copy(step + (X_BUFFERS - 1)).start()

    x_copy(step).wait()
    x_ref = xbuf_ref.at[lax.rem(step, X_BUFFERS)]

    for w32_ref, w16_ref in ((win_ref, win16_ref), (wa_ref, wa16_ref), (wc_ref, wc16_ref),
                             (wo_ref, wo16_ref)):
        w16_ref[...] = w32_ref[...].astype(BF16)

    @pl.when(pl.program_id(0) == 0)
    def _first_step():
        w_ref[...] = wqkv_ref[...].astype(BF16)
        lane = lax.broadcasted_iota(jnp.int32, (PROJ_ROWS, HEAD_COLS), 1)
        row = lax.broadcasted_iota(jnp.int32, (PROJ_ROWS, HEAD_COLS), 0).astype(F32)
        for hd in range(ATTN_HEADS):
            b = row * (slopes_ref[hd] * LOG2_E)
            b1 = b.astype(BF16).astype(F32)
            b2 = (b - b1).astype(BF16).astype(F32)
            b3 = (b - b1 - b2).astype(BF16).astype(F32)
            for bias_ref, first in ((biasa_ref, HEAD_DIM), (biasb_ref, 0)):
                bias_ref[hd] = jnp.where(lane == first, b1, jnp.where(
                    lane == first + 1, b2, jnp.where(lane == first + 2, b3, 0.0)))

    lane = lax.broadcasted_iota(jnp.int32, (PROJ_SUB_ROWS, HEAD_COLS), 1)
    for r in range(0, PROJ_ROWS, PROJ_SUB_ROWS):
        rows = slice(r, r + PROJ_SUB_ROWS)
        norm_row = pl.program_id(0) * (PROJ_ROWS // PROJ_SUB_ROWS) + r // PROJ_SUB_ROWS
        h_ref[rows, :] = _rms_norm_bf16(x_ref[rows, :], g_ref[...])

        def proj(col, rows=rows):
            return jnp.dot(h_ref[rows, :], w_ref[:, col:col + ATTN_WIDTH],
                           preferred_element_type=F32)

        q = proj(0) * Q_SCALE_LOG2
        q_ref[rows, :] = q.astype(BF16)
        k = proj(ATTN_WIDTH)
        for hd in range(ATTN_HEADS):
            cols = slice(hd * HEAD_COLS, (hd + 1) * HEAD_COLS)
            kh = k[:, cols]
            ka_ref[rows, cols] = jnp.where(lane < HEAD_DIM, kh, biasa_ref[hd, rows, :]).astype(BF16)
            kb_ref[rows, cols] = jnp.where(lane >= HEAD_DIM, kh, biasb_ref[hd, rows, :]).astype(BF16)
            qnorm_ref[norm_row, hd] = _max_map_norm(q[:, cols], lane)
            knorm_ref[norm_row, hd] = _max_map_norm(kh, lane)
        v_ref[rows, :] = proj(2 * ATTN_WIDTH).astype(BF16)


def _qkv_proj(slopes, x2d, g_pre, w_in, w_attn_o, w_conv_o, w_out):
    n = x2d.shape[0]
    steps = n // PROJ_ROWS
    row_block = lambda cols: pl.BlockSpec((PROJ_ROWS, cols), lambda i: (i, 0))
    act = jax.ShapeDtypeStruct((n, ATTN_WIDTH), BF16)
    norms = jax.ShapeDtypeStruct((n // PROJ_SUB_ROWS, ATTN_HEADS), F32)
    weights = (w_in, w_attn_o, w_conv_o, w_out)
    assert all(w.shape[0] % (steps * BF16_SUBLANES) == 0 for w in weights)
    assert steps >= X_BUFFERS - 1
    slab = lambda w: pl.BlockSpec((w.shape[0] // steps, w.shape[1]), lambda i: (i, 0))
    return pl.pallas_call(
        _qkv_proj_kernel,
        grid=(steps,),
        in_specs=[
            pl.BlockSpec(memory_space=pltpu.SMEM),
            pl.BlockSpec(memory_space=pl.ANY),
            pl.BlockSpec((1, D_MODEL), lambda i: (0, 0)),
            pl.BlockSpec((D_MODEL, QKV_COLS), lambda i: (0, 0), pipeline_mode=pl.Buffered(1)),
        ] + [slab(w) for w in weights],
        out_specs=[row_block(ATTN_WIDTH), row_block(ATTN_WIDTH), row_block(ATTN_WIDTH),
                   row_block(ATTN_WIDTH), pl.BlockSpec(memory_space=pltpu.SMEM),
                   pl.BlockSpec(memory_space=pltpu.SMEM)]
        + [slab(w) for w in weights],
        out_shape=[act, act, act, act, norms, norms]
        + [jax.ShapeDtypeStruct(w.shape, BF16) for w in weights],
        scratch_shapes=[pltpu.VMEM((PROJ_ROWS, D_MODEL), BF16),
                        pltpu.VMEM((ATTN_HEADS, PROJ_ROWS, HEAD_COLS), F32),
                        pltpu.VMEM((ATTN_HEADS, PROJ_ROWS, HEAD_COLS), F32),
                        pltpu.VMEM((D_MODEL, QKV_COLS), BF16),
                        pltpu.VMEM((X_BUFFERS, PROJ_ROWS, D_MODEL), F32),
                        pltpu.SemaphoreType.DMA((X_BUFFERS,))],
        compiler_params=pltpu.CompilerParams(
            dimension_semantics=("arbitrary",), vmem_limit_bytes=VMEM_LIMIT_BYTES),
        name="qkv_proj",
    )(slopes, x2d, g_pre, w_in, w_in, w_attn_o, w_conv_o, w_out)


def _diff_attn_kernel(slopes_ref, qnorm_ref, knorm_ref, q_ref, ka_ref, kb_ref, v_ref,
                      o0_ref, o1_ref, m_ref, acc_ref):
    tq, tk, ts = ATTN_Q_ROWS, ATTN_K_ROWS, ATTN_SUB_ROWS
    th = PROJ_SUB_ROWS
    batch = pl.program_id(0)
    head = pl.program_id(1)
    qi = pl.program_id(2)
    slope = slopes_ref[head] * LOG2_E
    seq_norm_row = batch * (ka_ref.shape[0] // th)

    q = q_ref[...].astype(F32)
    qlane = lax.broadcasted_iota(jnp.int32, q.shape, 1)
    q_maps = (
        jnp.where(qlane < HEAD_DIM, q, jnp.where(qlane < HEAD_DIM + 3, 1.0, 0.0)).astype(BF16),
        jnp.where(qlane >= HEAD_DIM, q, jnp.where(qlane < 3, 1.0, 0.0)).astype(BF16),
    )
    key_refs = (ka_ref, kb_ref)

    tri_keep = (lax.broadcasted_iota(jnp.int32, (ts, ts), 1)
                <= lax.broadcasted_iota(jnp.int32, (ts, ts), 0))

    def step(key_start, diag_cols, row_stop=tq, key_lo=0):
        span_bias = (key_start - qi * tq).astype(F32) * slope
        first = diag_cols is not None
        streams = [(r, mp)
                   for r in (range(tq - ts, -ts, -ts) if first else range(0, row_stop, ts))
                   for mp in range(2)]

        def width(r):
            return tk - key_lo if diag_cols is None else diag_cols + r + ts

        def scores(r, mp):
            keys = key_refs[mp][pl.ds(key_start + key_lo, width(r)), :]
            s = lax.dot_general(q_maps[mp][r:r + ts], keys, (((1,), (1,)), ((), ())),
                                preferred_element_type=F32)
            if diag_cols is not None:
                below = width(r) - ts
                blocks = [s[:, :below]] if below else []
                s = jnp.concatenate(blocks + [jnp.where(tri_keep, s[:, below:], MASK_VALUE)],
                                    axis=1)
            return s

        probs = {}

        def update(r, mp, s):
            m_cur = jnp.max(s, axis=-1, keepdims=True) + span_bias
            if first:
                m_new = jnp.broadcast_to(m_cur, (ts, HEAD_COLS))
                alpha = None
            else:
                m_prev = m_ref[mp, r:r + ts]
                m_new = jnp.maximum(m_prev, m_cur)
                alpha = jnp.exp2(m_prev - m_new)
            shift = m_new - span_bias
            p = jnp.exp2(s - jnp.concatenate([shift] * (width(r) // HEAD_COLS), axis=1))
            m_ref[mp, r:r + ts] = m_new
            probs[mp] = (p.astype(BF16), alpha)
            if mp == 0:
                return
            v_ones = jnp.concatenate([v_ref[pl.ds(key_start + key_lo, width(r)), :],
                                      jnp.ones((width(r), HEAD_COLS), BF16)], axis=1)
            pv_pair = jnp.dot(jnp.concatenate([probs[0][0], probs[1][0]], axis=0), v_ones,
                              preferred_element_type=F32)
            for mq in range(2):
                pv = pv_pair[mq * ts:(mq + 1) * ts]
                alpha = probs[mq][1]
                if alpha is not None:
                    pv = jnp.concatenate([alpha, alpha], axis=1) * acc_ref[mq, r:r + ts] + pv
                acc_ref[mq, r:r + ts] = pv

        pending = [scores(*st) for st in streams[:ATTN_LOOKAHEAD]]
        for i, st in enumerate(streams):
            if i + ATTN_LOOKAHEAD < len(streams):
                pending.append(scores(*streams[i + ATTN_LOOKAHEAD]))
            update(*st, pending.pop(0))

    n_spans = lax.shift_right_logical(qi * tq, TK_SHIFT)
    tail_start = pl.multiple_of(n_spans * tk, tk)
    for extra in range(0, tk, tq):
        @pl.when(qi * tq - tail_start == extra)
        def _diag_step(extra=extra):
            step(tail_start, extra)

    def row_block_bound(hf):
        m_low = jnp.minimum(m_ref[0, hf * th:(hf + 1) * th], m_ref[1, hf * th:(hf + 1) * th])
        m_low = jnp.min(jnp.min(m_low, axis=0, keepdims=True), axis=1, keepdims=True)[0, 0]
        q_norm = qnorm_ref[seq_norm_row + qi * (tq // th) + hf, head]
        return q_norm, m_low - ALIBI_SKIP_BITS

    row_blocks = [row_block_bound(hf) for hf in range(tq // th)]

    def full_span(kj, carry):
        key_start = pl.multiple_of(kj * tk, tk)
        rows_needed = jnp.int32(0)
        keys_from = jnp.int32(tk)
        for kh in reversed(range(tk // th)):
            k_norm = knorm_ref[seq_norm_row + kj * (tk // th) + kh, head]
            alibi_max = (key_start + ((kh + 1) * th - 1) - qi * tq).astype(F32) * slope
            for hf, (q_norm, floor) in enumerate(row_blocks):
                live = q_norm * k_norm + alibi_max > floor
                rows_needed = jnp.where(live, jnp.maximum(rows_needed, (hf + 1) * th),
                                        rows_needed)
                keys_from = jnp.where(live, kh * th, keys_from)
        for hf in range(tq // th):
            for kh in range(tk // th):
                @pl.when(jnp.logical_and(rows_needed == (hf + 1) * th, keys_from == kh * th))
                def _fold_span(hf=hf, kh=kh):
                    step(key_start, None, row_stop=(hf + 1) * th, key_lo=kh * th)

        return carry

    lax.fori_loop(0, n_spans, full_span, 0)

    for mp, o_ref in enumerate((o0_ref, o1_ref)):
        o_ref[...] = (acc_ref[mp, :, 0:HEAD_COLS]
                      / acc_ref[mp, :, HEAD_COLS:2 * HEAD_COLS]).astype(BF16)


def _diff_attn(q, ka, kb, v, slopes, qnorm, knorm, *, batch, seq):
    nq = seq // ATTN_Q_ROWS
    seq_block = pl.BlockSpec((seq, HEAD_COLS), lambda b, h, i: (b, h))
    q_block = pl.BlockSpec((ATTN_Q_ROWS, HEAD_COLS), lambda b, h, i: (b * nq + i, h))
    return pl.pallas_call(
        _diff_attn_kernel,
        grid=(batch, ATTN_HEADS, nq),
        in_specs=[
            pl.BlockSpec(memory_space=pltpu.SMEM), pl.BlockSpec(memory_space=pltpu.SMEM),
            pl.BlockSpec(memory_space=pltpu.SMEM),
            q_block, seq_block, seq_block, seq_block,
        ],
        out_specs=[q_block, q_block],
        out_shape=[jax.ShapeDtypeStruct((batch * seq, ATTN_WIDTH), BF16)] * 2,
        scratch_shapes=[pltpu.VMEM((2, ATTN_Q_ROWS, HEAD_COLS), F32),
                        pltpu.VMEM((2, ATTN_Q_ROWS, 2 * HEAD_COLS), F32)],
        compiler_params=pltpu.CompilerParams(
            dimension_semantics=("arbitrary", "arbitrary", "arbitrary"),
            vmem_limit_bytes=VMEM_LIMIT_BYTES),
        name="diff_attn",
    )(slopes, qnorm, knorm, q, ka, kb, v)


def _sigmoid(z):
    return 0.5 * jnp.tanh(0.5 * z) + 0.5


def _out_proj_kernel(x_ref, o0_ref, o1_ref, lq1_ref, lk1_ref, lq2_ref, lk2_ref, subln_ref,
                     gpre_ref, wmid_ref, whi_ref, cw_ref, wa_ref, wc_ref, bm_ref, wo_ref, gp_ref,
                     out_ref, h_ref, u_ref, *, tiles_per_seq, lam_init):
    tm, ts, cwid = OUT_ROWS, OUT_SUB_ROWS, CONV_WIDTH

    @pl.when(pl.program_id(0) % tiles_per_seq == 0)
    def _sequence_start():
        u_ref[0:HALO_ROWS, :] = jnp.zeros((HALO_ROWS, cwid), F32)

    blocks = range(0, tm, ts)

    def proj(r, w_ref, col, cols):
        return jnp.dot(h_ref[r:r + ts, :], w_ref[:, col:col + cols], preferred_element_type=F32)

    for r in blocks:
        h_ref[r:r + ts, :] = _rms_norm_bf16(x_ref[r:r + ts, :], gpre_ref[...])

    for r in blocks:
        u = proj(r, wmid_ref, ATTN_WIDTH + cwid, cwid) * proj(r, whi_ref, 0, cwid)
        u_ref[HALO_ROWS + r:HALO_ROWS + r + ts, :] = u
    y_conv = {}
    for r in blocks:
        conv = cw_ref[CONV_K - 1:CONV_K, :] * u_ref[HALO_ROWS + r:HALO_ROWS + r + ts, :]
        for k in range(CONV_K - 1):
            first_row = HALO_ROWS + r - (CONV_K - 1 - k)
            conv = conv + cw_ref[k:k + 1, :] * u_ref[first_row:first_row + ts, :]
        z_c = proj(r, whi_ref, cwid, cwid)
        yc_in = (proj(r, wmid_ref, ATTN_WIDTH, cwid) * conv) * (z_c * _sigmoid(z_c))
        y_conv[r] = jnp.dot(yc_in.astype(BF16), wc_ref[...], preferred_element_type=F32)
    u_ref[0:HALO_ROWS, :] = u_ref[tm:tm + HALO_ROWS, :]

    g_a = {}
    for r in blocks:
        g_a[r] = _sigmoid(proj(r, whi_ref, 2 * cwid, D_MODEL) + bm_ref[:, 0:D_MODEL])

    lam = (jnp.exp(jnp.sum(lq1_ref[...] * lk1_ref[...], axis=-1, keepdims=True))
           - jnp.exp(jnp.sum(lq2_ref[...] * lk2_ref[...], axis=-1, keepdims=True))
           + lam_init)
    subln_gain = subln_ref[...] * (1.0 - lam_init)
    y_merged = {}
    for r in blocks:
        o = o0_ref[r:r + ts, :].astype(F32) - lam * o1_ref[r:r + ts, :].astype(F32)
        heads = []
        for hd in range(ATTN_HEADS):
            oh = o[:, hd * HEAD_COLS:(hd + 1) * HEAD_COLS]
            ms = jnp.mean(oh * oh, axis=-1, keepdims=True)
            heads.append(oh * lax.rsqrt(ms + NORM_EPS) * subln_gain)
        z_a = proj(r, wmid_ref, 0, ATTN_WIDTH)
        ya_in = jnp.concatenate(heads, axis=1) * (z_a * _sigmoid(z_a))
        y_attn = jnp.dot(ya_in.astype(BF16), wa_ref[...], preferred_element_type=F32)
        y_merged[r] = g_a[r] * y_attn

    for r in blocks:
        g_c = _sigmoid(proj(r, whi_ref, 2 * cwid + D_MODEL, D_MODEL)
                       + bm_ref[:, D_MODEL:2 * D_MODEL])
        y_merged[r] = (y_merged[r] + g_c * y_conv[r]).astype(BF16)
    for r in blocks:
        out = jnp.dot(y_merged[r], wo_ref[...], preferred_element_type=F32)
        ms = jnp.mean(out * out, axis=-1, keepdims=True)
        out_ref[r:r + ts, :] = x_ref[r:r + ts, :] + out * lax.rsqrt(ms + NORM_EPS) * gp_ref[...]


def _out_proj(x2d, o0, o1, lq1, lk1, lq2, lk2, subln_gain, g_pre, w_in_bf16, conv_w, wa, wc,
              b_merge, wo, g_post, *, seq, lam_init):
    n = x2d.shape[0]
    mid_cols = ATTN_WIDTH + 2 * CONV_WIDTH
    hi_cols = 2 * CONV_WIDTH + GATE_COLS
    assert QKV_COLS == mid_cols and QKV_COLS + mid_cols == hi_cols == IN_COLS - hi_cols
    row_block = lambda cols: pl.BlockSpec((OUT_ROWS, cols), lambda i: (i, 0))
    const = lambda shape, col_block=0: pl.BlockSpec(shape, lambda i: (0, col_block),
                                                   pipeline_mode=pl.Buffered(1))
    return pl.pallas_call(
        functools.partial(_out_proj_kernel, tiles_per_seq=seq // OUT_ROWS, lam_init=lam_init),
        grid=(n // OUT_ROWS,),
        in_specs=[
            row_block(D_MODEL), row_block(ATTN_WIDTH), row_block(ATTN_WIDTH),
            const((1, HEAD_DIM)), const((1, HEAD_DIM)), const((1, HEAD_DIM)), const((1, HEAD_DIM)),
            const((1, HEAD_COLS)), const((1, D_MODEL)),
            const((D_MODEL, mid_cols), 1), const((D_MODEL, hi_cols), 1),
            const((CONV_K, CONV_WIDTH)), const((ATTN_WIDTH, D_MODEL)),
            const((CONV_WIDTH, D_MODEL)), const((1, 2 * D_MODEL)), const((D_MODEL, D_MODEL)),
            const((1, D_MODEL)),
        ],
        out_specs=row_block(D_MODEL),
        out_shape=jax.ShapeDtypeStruct((n, D_MODEL), F32),
        scratch_shapes=[pltpu.VMEM((OUT_ROWS, D_MODEL), BF16),
                        pltpu.VMEM((HALO_ROWS + OUT_ROWS, CONV_WIDTH), F32)],
        compiler_params=pltpu.CompilerParams(
            dimension_semantics=("arbitrary",), vmem_limit_bytes=VMEM_LIMIT_BYTES),
        name="out_proj",
    )(x2d, o0, o1, lq1, lk1, lq2, lk2, subln_gain, g_pre, w_in_bf16, w_in_bf16, conv_w, wa, wc,
      b_merge, wo, g_post)


def _layer(x, layer_idx, w_in, lq1, lk1, lq2, lk2, subln_gain, conv_w, w_attn_o, w_conv_o,
           b_merge, w_out, g_pre, g_post):
    batch, seq, d = x.shape
    x2d = x.reshape(batch * seq, d)
    row = lambda a: a.reshape(1, -1).astype(F32)
    slopes = jnp.asarray([2.0 ** (-8.0 * (i + 1) / ATTN_HEADS) for i in range(ATTN_HEADS)], F32)

    q, ka, kb, v, qnorm, knorm, w_in_bf16, wa_bf16, wc_bf16, wo_bf16 = _qkv_proj(
        slopes, x2d, row(g_pre), w_in, w_attn_o, w_conv_o, w_out)
    o0, o1 = _diff_attn(q, ka, kb, v, slopes, qnorm, knorm, batch=batch, seq=seq)
    out = _out_proj(x2d, o0, o1, row(lq1), row(lk1), row(lq2), row(lk2), row(subln_gain),
                    row(g_pre), w_in_bf16, conv_w.astype(F32), wa_bf16, wc_bf16, row(b_merge),
                    wo_bf16, row(g_post), seq=seq, lam_init=_lambda_init(layer_idx))
    return out.reshape(batch, seq, d)


def kernel(x, w_in, lambda_q1, lambda_k1, lambda_q2, lambda_k2, subln_gain, conv_w, w_attn_o,
           w_conv_o, b_merge, w_out, g_pre, g_post):
    for l in range(w_in.shape[0]):
        x = _layer(x, l, w_in[l], lambda_q1[l], lambda_k1[l], lambda_q2[l], lambda_k2[l],
                   subln_gain[l], conv_w[l], w_attn_o[l], w_conv_o[l], b_merge[l], w_out[l],
                   g_pre[l], g_post[l])
    return x
```

```python
import functools
import math

import jax
import jax.numpy as jnp
from jax import lax
from jax.experimental import pallas as pl
from jax.experimental.pallas import tpu as pltpu

D_MODEL = 1024
ATTN_HEADS = 4
HEAD_DIM = 64
HEAD_COLS = 2 * HEAD_DIM
ATTN_WIDTH = ATTN_HEADS * HEAD_COLS
CONV_WIDTH = D_MODEL // 2
CONV_K = 3
NORM_EPS = 1e-6
QKV_COLS = 3 * ATTN_WIDTH
CONV_COLS = 4 * CONV_WIDTH
GATE_COLS = 2 * D_MODEL
IN_COLS = QKV_COLS + ATTN_WIDTH + CONV_COLS + GATE_COLS
MASK_VALUE = -1e30
LOG2_E = math.log2(math.e)
Q_SCALE_LOG2 = HEAD_DIM ** -0.5 * LOG2_E
ALIBI_SKIP_BITS = 160.0
NORM_SLACK = 1.01

PROJ_ROWS = 1024
PROJ_SUB_ROWS = 512
X_BUFFERS = 3
ATTN_Q_ROWS = 1024
ATTN_K_ROWS = 1024
TK_SHIFT = ATTN_K_ROWS.bit_length() - 1
ATTN_SUB_ROWS = 256
ATTN_LOOKAHEAD = 3
assert ATTN_K_ROWS == 1 << TK_SHIFT and ATTN_K_ROWS % ATTN_Q_ROWS == 0
assert PROJ_ROWS == ATTN_K_ROWS
assert ATTN_Q_ROWS % PROJ_SUB_ROWS == 0 and PROJ_SUB_ROWS % ATTN_SUB_ROWS == 0
OUT_ROWS = 1024
OUT_SUB_ROWS = 512
BF16_SUBLANES = 16
HALO_ROWS = BF16_SUBLANES
assert HALO_ROWS >= CONV_K - 1
V7X_VMEM_BYTES = 64 * 1024 * 1024
VMEM_LIMIT_BYTES = V7X_VMEM_BYTES * 7 // 8

BF16 = jnp.bfloat16
F32 = jnp.float32


def _lambda_init(layer_idx):
    return 0.8 - 0.6 * math.exp(-0.3 * layer_idx)


def _rms_norm_bf16(x, gain):
    ms = jnp.mean(x * x, axis=-1, keepdims=True)
    return (x * lax.rsqrt(ms + NORM_EPS) * gain).astype(BF16)


def _max_map_norm(x, lane):
    sq = x * x
    n0 = jnp.sum(jnp.where(lane < HEAD_DIM, sq, 0.0), axis=-1, keepdims=True)
    n1 = jnp.sum(jnp.where(lane >= HEAD_DIM, sq, 0.0), axis=-1, keepdims=True)
    return jnp.sqrt(jnp.max(jnp.maximum(n0, n1), axis=0, keepdims=True))[0, 0] * NORM_SLACK


def _qkv_proj_kernel(slopes_ref, x_hbm_ref, g_ref, win_hbm_ref, win_ref, wa_ref, wc_ref, wo_ref,
                     q_ref, ka_ref, kb_ref, v_ref, qnorm_ref, knorm_ref, win16_ref, wa16_ref,
                     wc16_ref, wo16_ref, h_ref, biasa_ref, biasb_ref, w_ref, xbuf_ref, xsem_ref,
                     wqkv_ref, wsem_ref):
    step = pl.program_id(0)

    def x_copy(tile):
        tile = jnp.asarray(tile, jnp.int32)
        slot = lax.rem(tile, X_BUFFERS)
        return pltpu.make_async_copy(
            x_hbm_ref.at[pl.ds(pl.multiple_of(tile * PROJ_ROWS, PROJ_ROWS), PROJ_ROWS), :],
            xbuf_ref.at[slot], xsem_ref.at[slot])

    def wqkv_copy():
        return pltpu.make_async_copy(win_hbm_ref.at[:, pl.ds(0, QKV_COLS)], wqkv_ref,
                                     wsem_ref.at[0])

    @pl.when(step == 0)
    def _prime_ring():
        wqkv_copy().start()
        for tile in range(X_BUFFERS - 1):
            x_copy(tile).start()

    @pl.when(step + (X_BUFFERS - 1) < pl.num_programs(0))
    def _prefetch():
        x_copy(step + (X_BUFFERS - 1)).start()

    for w32_ref, w16_ref in ((win_ref, win16_ref), (wa_ref, wa16_ref), (wc_ref, wc16_ref),
                             (wo_ref, wo16_ref)):
        w16_ref[...] = w32_ref[...].astype(BF16)

    @pl.when(pl.program_id(0) == 0)
    def _first_step():
        lane = lax.broadcasted_iota(jnp.int32, (PROJ_ROWS, HEAD_COLS), 1)
        row = lax.broadcasted_iota(jnp.int32, (PROJ_ROWS, HEAD_COLS), 0).astype(F32)
        for hd in range(ATTN_HEADS):
            b = row * (slopes_ref[hd] * LOG2_E)
            b1 = b.astype(BF16).astype(F32)
            b2 = (b - b1).astype(BF16).astype(F32)
            b3 = (b - b1 - b2).astype(BF16).astype(F32)
            for bias_ref, first in ((biasa_ref, HEAD_DIM), (biasb_ref, 0)):
                bias_ref[hd] = jnp.where(lane == first, b1, jnp.where(
                    lane == first + 1, b2, jnp.where(lane == first + 2, b3, 0.0)))
        wqkv_copy().wait()
        w_ref[...] = wqkv_ref[...].astype(BF16)

    x_copy(step).wait()
    x_ref = xbuf_ref.at[lax.rem(step, X_BUFFERS)]

    lane = lax.broadcasted_iota(jnp.int32, (PROJ_SUB_ROWS, HEAD_COLS), 1)
    for r in range(0, PROJ_ROWS, PROJ_SUB_ROWS):
        rows = slice(r, r + PROJ_SUB_ROWS)
        norm_row = pl.program_id(0) * (PROJ_ROWS // PROJ_SUB_ROWS) + r // PROJ_SUB_ROWS
        h_ref[rows, :] = _rms_norm_bf16(x_ref[rows, :], g_ref[...])

        def proj(col, rows=rows):
            return jnp.dot(h_ref[rows, :], w_ref[:, col:col + ATTN_WIDTH],
                           preferred_element_type=F32)

        q = proj(0) * Q_SCALE_LOG2
        q_ref[rows, :] = q.astype(BF16)
        k = proj(ATTN_WIDTH)
        for hd in range(ATTN_HEADS):
            cols = slice(hd * HEAD_COLS, (hd + 1) * HEAD_COLS)
            kh = k[:, cols]
            ka_ref[rows, cols] = jnp.where(lane < HEAD_DIM, kh, biasa_ref[hd, rows, :]).astype(BF16)
            kb_ref[rows, cols] = jnp.where(lane >= HEAD_DIM, kh, biasb_ref[hd, rows, :]).astype(BF16)
            qnorm_ref[norm_row, hd] = _max_map_norm(q[:, cols], lane)
            knorm_ref[norm_row, hd] = _max_map_norm(kh, lane)
        v_ref[rows, :] = proj(2 * ATTN_WIDTH).astype(BF16)


def _qkv_proj(slopes, x2d, g_pre, w_in, w_attn_o, w_conv_o, w_out):
    n = x2d.shape[0]
    steps = n // PROJ_ROWS
    row_block = lambda cols: pl.BlockSpec((PROJ_ROWS, cols), lambda i: (i, 0))
    act = jax.ShapeDtypeStruct((n, ATTN_WIDTH), BF16)
    norms = jax.ShapeDtypeStruct((n // PROJ_SUB_ROWS, ATTN_HEADS), F32)
    weights = (w_in, w_attn_o, w_conv_o, w_out)
    assert all(w.shape[0] % (steps * BF16_SUBLANES) == 0 for w in weights)
    assert steps >= X_BUFFERS - 1
    slab = lambda w: pl.BlockSpec((w.shape[0] // steps, w.shape[1]), lambda i: (i, 0))
    return pl.pallas_call(
        _qkv_proj_kernel,
        grid=(steps,),
        in_specs=[
            pl.BlockSpec(memory_space=pltpu.SMEM),
            pl.BlockSpec(memory_space=pl.ANY),
            pl.BlockSpec((1, D_MODEL), lambda i: (0, 0)),
            pl.BlockSpec(memory_space=pl.ANY),
        ] + [slab(w) for w in weights],
        out_specs=[row_block(ATTN_WIDTH), row_block(ATTN_WIDTH), row_block(ATTN_WIDTH),
                   row_block(ATTN_WIDTH), pl.BlockSpec(memory_space=pltpu.SMEM),
                   pl.BlockSpec(memory_space=pltpu.SMEM)]
        + [slab(w) for w in weights],
        out_shape=[act, act, act, act, norms, norms]
        + [jax.ShapeDtypeStruct(w.shape, BF16) for w in weights],
        scratch_shapes=[pltpu.VMEM((PROJ_ROWS, D_MODEL), BF16),
                        pltpu.VMEM((ATTN_HEADS, PROJ_ROWS, HEAD_COLS), F32),
                        pltpu.VMEM((ATTN_HEADS, PROJ_ROWS, HEAD_COLS), F32),
                        pltpu.VMEM((D_MODEL, QKV_COLS), BF16),
                        pltpu.VMEM((X_BUFFERS, PROJ_ROWS, D_MODEL), F32),
                        pltpu.SemaphoreType.DMA((X_BUFFERS,)),
                        pltpu.VMEM((D_MODEL, QKV_COLS), F32),
                        pltpu.SemaphoreType.DMA((1,))],
        compiler_params=pltpu.CompilerParams(
            dimension_semantics=("arbitrary",), vmem_limit_bytes=VMEM_LIMIT_BYTES),
        name="qkv_proj",
    )(slopes, x2d, g_pre, w_in, w_in, w_attn_o, w_conv_o, w_out)


def _diff_attn_kernel(slopes_ref, qnorm_ref, knorm_ref, q_ref, ka_ref, kb_ref, v_ref,
                      o0_ref, o1_ref, m_ref, acc_ref):
    tq, tk, ts = ATTN_Q_ROWS, ATTN_K_ROWS, ATTN_SUB_ROWS
    th = PROJ_SUB_ROWS
    batch = pl.program_id(0)
    head = pl.program_id(1)
    qi = pl.program_id(2)
    slope = slopes_ref[head] * LOG2_E
    seq_norm_row = batch * (ka_ref.shape[0] // th)

    q = q_ref[...].astype(F32)
    qlane = lax.broadcasted_iota(jnp.int32, q.shape, 1)
    q_maps = (
        jnp.where(qlane < HEAD_DIM, q, jnp.where(qlane < HEAD_DIM + 3, 1.0, 0.0)).astype(BF16),
        jnp.where(qlane >= HEAD_DIM, q, jnp.where(qlane < 3, 1.0, 0.0)).astype(BF16),
    )
    key_refs = (ka_ref, kb_ref)

    tri_keep = (lax.broadcasted_iota(jnp.int32, (ts, ts), 1)
                <= lax.broadcasted_iota(jnp.int32, (ts, ts), 0))

    def step(key_start, diag_cols, row_stop=tq, key_lo=0):
        span_bias = (key_start - qi * tq).astype(F32) * slope
        first = diag_cols is not None
        streams = [(r, mp)
                   for r in (range(tq - ts, -ts, -ts) if first else range(0, row_stop, ts))
                   for mp in range(2)]

        def width(r):
            return tk - key_lo if diag_cols is None else diag_cols + r + ts

        def scores(r, mp):
            keys = key_refs[mp][pl.ds(key_start + key_lo, width(r)), :]
            s = lax.dot_general(q_maps[mp][r:r + ts], keys, (((1,), (1,)), ((), ())),
                                preferred_element_type=F32)
            if diag_cols is not None:
                below = width(r) - ts
                blocks = [s[:, :below]] if below else []
                s = jnp.concatenate(blocks + [jnp.where(tri_keep, s[:, below:], MASK_VALUE)],
                                    axis=1)
            return s

        probs = {}

        def update(r, mp, s):
            m_cur = jnp.max(s, axis=-1, keepdims=True) + span_bias
            if first:
                m_new = jnp.broadcast_to(m_cur, (ts, HEAD_COLS))
                alpha = None
            else:
                m_prev = m_ref[mp, r:r + ts]
                m_new = jnp.maximum(m_prev, m_cur)
                alpha = jnp.exp2(m_prev - m_new)
            shift = m_new - span_bias
            p = jnp.exp2(s - jnp.concatenate([shift] * (width(r) // HEAD_COLS), axis=1))
            m_ref[mp, r:r + ts] = m_new
            probs[mp] = (p.astype(BF16), alpha)
            if mp == 0:
                return
            v_ones = jnp.concatenate([v_ref[pl.ds(key_start + key_lo, width(r)), :],
                                      jnp.ones((width(r), HEAD_COLS), BF16)], axis=1)
            pv_pair = jnp.dot(jnp.concatenate([probs[0][0], probs[1][0]], axis=0), v_ones,
                              preferred_element_type=F32)
            for mq in range(2):
                pv = pv_pair[mq * ts:(mq + 1) * ts]
                alpha = probs[mq][1]
                if alpha is not None:
                    pv = jnp.concatenate([alpha, alpha], axis=1) * acc_ref[mq, r:r + ts] + pv
                acc_ref[mq, r:r + ts] = pv

        pending = [scores(*st) for st in streams[:ATTN_LOOKAHEAD]]
        for i, st in enumerate(streams):
            if i + ATTN_LOOKAHEAD < len(streams):
                pending.append(scores(*streams[i + ATTN_LOOKAHEAD]))
            update(*st, pending.pop(0))

    n_spans = lax.shift_right_logical(qi * tq, TK_SHIFT)
    tail_start = pl.multiple_of(n_spans * tk, tk)
    for extra in range(0, tk, tq):
        @pl.when(qi * tq - tail_start == extra)
        def _diag_step(extra=extra):
            step(tail_start, extra)

    def row_block_bound(hf):
        m_low = jnp.minimum(m_ref[0, hf * th:(hf + 1) * th], m_ref[1, hf * th:(hf + 1) * th])
        m_low = jnp.min(jnp.min(m_low, axis=0, keepdims=True), axis=1, keepdims=True)[0, 0]
        q_norm = qnorm_ref[seq_norm_row + qi * (tq // th) + hf, head]
        return q_norm, m_low - ALIBI_SKIP_BITS

    row_blocks = [row_block_bound(hf) for hf in range(tq // th)]

    def full_span(kj, carry):
        key_start = pl.multiple_of(kj * tk, tk)
        rows_needed = jnp.int32(0)
        keys_from = jnp.int32(tk)
        for kh in reversed(range(tk // th)):
            k_norm = knorm_ref[seq_norm_row + kj * (tk // th) + kh, head]
            alibi_max = (key_start + ((kh + 1) * th - 1) - qi * tq).astype(F32) * slope
            for hf, (q_norm, floor) in enumerate(row_blocks):
                live = q_norm * k_norm + alibi_max > floor
                rows_needed = jnp.where(live, jnp.maximum(rows_needed, (hf + 1) * th),
                                        rows_needed)
                keys_from = jnp.where(live, kh * th, keys_from)
        for hf in range(tq // th):
            for kh in range(tk // th):
                @pl.when(jnp.logical_and(rows_needed == (hf + 1) * th, keys_from == kh * th))
                def _fold_span(hf=hf, kh=kh):
                    step(key_start, None, row_stop=(hf + 1) * th, key_lo=kh * th)

        return carry

    lax.fori_loop(0, n_spans, full_span, 0)

    for mp, o_ref in enumerate((o0_ref, o1_ref)):
        o_ref[...] = (acc_ref[mp, :, 0:HEAD_COLS]
                      / acc_ref[mp, :, HEAD_COLS:2 * HEAD_COLS]).astype(BF16)


def _diff_attn(q, ka, kb, v, slopes, qnorm, knorm, *, batch, seq):
    nq = seq // ATTN_Q_ROWS
    seq_block = pl.BlockSpec((seq, HEAD_COLS), lambda b, h, i: (b, h))
    q_block = pl.BlockSpec((ATTN_Q_ROWS, HEAD_COLS), lambda b, h, i: (b * nq + i, h))
    return pl.pallas_call(
        _diff_attn_kernel,
        grid=(batch, ATTN_HEADS, nq),
        in_specs=[
            pl.BlockSpec(memory_space=pltpu.SMEM), pl.BlockSpec(memory_space=pltpu.SMEM),
            pl.BlockSpec(memory_space=pltpu.SMEM),
            q_block, seq_block, seq_block, seq_block,
        ],
        out_specs=[q_block, q_block],
        out_shape=[jax.ShapeDtypeStruct((batch * seq, ATTN_WIDTH), BF16)] * 2,
        scratch_shapes=[pltpu.VMEM((2, ATTN_Q_ROWS, HEAD_COLS), F32),
                        pltpu.VMEM((2, ATTN_Q_ROWS, 2 * HEAD_COLS), F32)],
        compiler_params=pltpu.CompilerParams(
            dimension_semantics=("arbitrary", "arbitrary", "arbitrary"),
            vmem_limit_bytes=VMEM_LIMIT_BYTES),
        name="diff_attn",
    )(slopes, qnorm, knorm, q, ka, kb, v)


def _sigmoid(z):
    return 0.5 * jnp.tanh(0.5 * z) + 0.5


def _out_proj_kernel(x_ref, o0_ref, o1_ref, lq1_ref, lk1_ref, lq2_ref, lk2_ref, subln_ref,
                     gpre_ref, wmid_ref, whi_ref, cw_ref, wa_ref, wc_ref, bm_ref, wo_ref, gp_ref,
                     out_ref, h_ref, u_ref, *, tiles_per_seq, lam_init):
    tm, ts, cwid = OUT_ROWS, OUT_SUB_ROWS, CONV_WIDTH

    @pl.when(pl.program_id(0) % tiles_per_seq == 0)
    def _sequence_start():
        u_ref[0:HALO_ROWS, :] = jnp.zeros((HALO_ROWS, cwid), F32)

    blocks = range(0, tm, ts)

    def proj(r, w_ref, col, cols):
        return jnp.dot(h_ref[r:r + ts, :], w_ref[:, col:col + cols], preferred_element_type=F32)

    for r in blocks:
        h_ref[r:r + ts, :] = _rms_norm_bf16(x_ref[r:r + ts, :], gpre_ref[...])

    for r in blocks:
        u = proj(r, wmid_ref, ATTN_WIDTH + cwid, cwid) * proj(r, whi_ref, 0, cwid)
        u_ref[HALO_ROWS + r:HALO_ROWS + r + ts, :] = u
    y_conv = {}
    for r in blocks:
        conv = cw_ref[CONV_K - 1:CONV_K, :] * u_ref[HALO_ROWS + r:HALO_ROWS + r + ts, :]
        for k in range(CONV_K - 1):
            first_row = HALO_ROWS + r - (CONV_K - 1 - k)
            conv = conv + cw_ref[k:k + 1, :] * u_ref[first_row:first_row + ts, :]
        z_c = proj(r, whi_ref, cwid, cwid)
        yc_in = (proj(r, wmid_ref, ATTN_WIDTH, cwid) * conv) * (z_c * _sigmoid(z_c))
        y_conv[r] = jnp.dot(yc_in.astype(BF16), wc_ref[...], preferred_element_type=F32)
    u_ref[0:HALO_ROWS, :] = u_ref[tm:tm + HALO_ROWS, :]

    g_a = {}
    for r in blocks:
        g_a[r] = _sigmoid(proj(r, whi_ref, 2 * cwid, D_MODEL) + bm_ref[:, 0:D_MODEL])

    lam = (jnp.exp(jnp.sum(lq1_ref[...] * lk1_ref[...], axis=-1, keepdims=True))
           - jnp.exp(jnp.sum(lq2_ref[...] * lk2_ref[...], axis=-1, keepdims=True))
           + lam_init)
    subln_gain = subln_ref[...] * (1.0 - lam_init)
    y_merged = {}
    for r in blocks:
        o = o0_ref[r:r + ts, :].astype(F32) - lam * o1_ref[r:r + ts, :].astype(F32)
        heads = []
        for hd in range(ATTN_HEADS):
            oh = o[:, hd * HEAD_COLS:(hd + 1) * HEAD_COLS]
            ms = jnp.mean(oh * oh, axis=-1, keepdims=True)
            heads.append(oh * lax.rsqrt(ms + NORM_EPS) * subln_gain)
        z_a = proj(r, wmid_ref, 0, ATTN_WIDTH)
        ya_in = jnp.concatenate(heads, axis=1) * (z_a * _sigmoid(z_a))
        y_attn = jnp.dot(ya_in.astype(BF16), wa_ref[...], preferred_element_type=F32)
        y_merged[r] = g_a[r] * y_attn

    for r in blocks:
        g_c = _sigmoid(proj(r, whi_ref, 2 * cwid + D_MODEL, D_MODEL)
                       + bm_ref[:, D_MODEL:2 * D_MODEL])
        y_merged[r] = (y_merged[r] + g_c * y_conv[r]).astype(BF16)
    for r in blocks:
        out = jnp.dot(y_merged[r], wo_ref[...], preferred_element_type=F32)
        ms = jnp.mean(out * out, axis=-1, keepdims=True)
        out_ref[r:r + ts, :] = x_ref[r:r + ts, :] + out * lax.rsqrt(ms + NORM_EPS) * gp_ref[...]


def _out_proj(x2d, o0, o1, lq1, lk1, lq2, lk2, subln_gain, g_pre, w_in_bf16, conv_w, wa, wc,
              b_merge, wo, g_post, *, seq, lam_init):
    n = x2d.shape[0]
    mid_cols = ATTN_WIDTH + 2 * CONV_WIDTH
    hi_cols = 2 * CONV_WIDTH + GATE_COLS
    assert QKV_COLS == mid_cols and QKV_COLS + mid_cols == hi_cols == IN_COLS - hi_cols
    row_block = lambda cols: pl.BlockSpec((OUT_ROWS, cols), lambda i: (i, 0))
    const = lambda shape, col_block=0: pl.BlockSpec(shape, lambda i: (0, col_block),
                                                   pipeline_mode=pl.Buffered(1))
    return pl.pallas_call(
        functools.partial(_out_proj_kernel, tiles_per_seq=seq // OUT_ROWS, lam_init=lam_init),
        grid=(n // OUT_ROWS,),
        in_specs=[
            row_block(D_MODEL), row_block(ATTN_WIDTH), row_block(ATTN_WIDTH),
            const((1, HEAD_DIM)), const((1, HEAD_DIM)), const((1, HEAD_DIM)), const((1, HEAD_DIM)),
            const((1, HEAD_COLS)), const((1, D_MODEL)),
            const((D_MODEL, mid_cols), 1), const((D_MODEL, hi_cols), 1),
            const((CONV_K, CONV_WIDTH)), const((ATTN_WIDTH, D_MODEL)),
            const((CONV_WIDTH, D_MODEL)), const((1, 2 * D_MODEL)), const((D_MODEL, D_MODEL)),
            const((1, D_MODEL)),
        ],
        out_specs=row_block(D_MODEL),
        out_shape=jax.ShapeDtypeStruct((n, D_MODEL), F32),
        scratch_shapes=[pltpu.VMEM((OUT_ROWS, D_MODEL), BF16),
                        pltpu.VMEM((HALO_ROWS + OUT_ROWS, CONV_WIDTH), F32)],
        compiler_params=pltpu.CompilerParams(
            dimension_semantics=("arbitrary",), vmem_limit_bytes=VMEM_LIMIT_BYTES),
        name="out_proj",
    )(x2d, o0, o1, lq1, lk1, lq2, lk2, subln_gain, g_pre, w_in_bf16, w_in_bf16, conv_w, wa, wc,
      b_merge, wo, g_post)


def _layer(x, layer_idx, w_in, lq1, lk1, lq2, lk2, subln_gain, conv_w, w_attn_o, w_conv_o,
           b_merge, w_out, g_pre, g_post):
    batch, seq, d = x.shape
    x2d = x.reshape(batch * seq, d)
    row = lambda a: a.reshape(1, -1).astype(F32)
    slopes = jnp.asarray([2.0 ** (-8.0 * (i + 1) / ATTN_HEADS) for i in range(ATTN_HEADS)], F32)

    q, ka, kb, v, qnorm, knorm, w_in_bf16, wa_bf16, wc_bf16, wo_bf16 = _qkv_proj(
        slopes, x2d, row(g_pre), w_in, w_attn_o, w_conv_o, w_out)
    o0, o1 = _diff_attn(q, ka, kb, v, slopes, qnorm, knorm, batch=batch, seq=seq)
    out = _out_proj(x2d, o0, o1, row(lq1), row(lk1), row(lq2), row(lk2), row(subln_gain),
                    row(g_pre), w_in_bf16, conv_w.astype(F32), wa_bf16, wc_bf16, row(b_merge),
                    wo_bf16, row(g_post), seq=seq, lam_init=_lambda_init(layer_idx))
    return out.reshape(batch, seq, d)


def kernel(x, w_in, lambda_q1, lambda_k1, lambda_q2, lambda_k2, subln_gain, conv_w, w_attn_o,
           w_conv_o, b_merge, w_out, g_pre, g_post):
    for l in range(w_in.shape[0]):
        x = _layer(x, l, w_in[l], lambda_q1[l], lambda_k1[l], lambda_q2[l], lambda_k2[l],
                   subln_gain[l], conv_w[l], w_attn_o[l], w_conv_o[l], b_merge[l], w_out[l],
                   g_pre[l], g_post[l])
    return x
```
